```python
import jax, jax.numpy as jnp
from jax import lax
import numpy as np

D_MODEL = 1024
BATCH = 4
SEQ = 8192
DEPTH = 1

D_LRU = D_MODEL // 2
D_RET = D_MODEL // 2
D_MIX = D_LRU + D_RET
LRU_HEADS = 8
LRU_HEAD_DIM = D_LRU // LRU_HEADS
CONV_WIDTH = 4
LRU_C = 8.0
RET_HEADS = 4
RET_HEAD_DIM = D_RET // RET_HEADS
RET_CHUNK = 128
ROPE_THETA = 10000.0
MAX_POS_OFFSET = 1024
D_IN = 2 * D_LRU + 4 * D_RET
N_EXPERTS = 32
TOP_K = 4
D_FF = D_MODEL
SWIGLU_ALPHA = 1.702
SWIGLU_LIMIT = 7.0
MOE_BLOCK = 128
NORM_EPS = 1e-5

kernel_name = "hymba_rglru_retention_moe"


def rmsnorm(x, g):
    xf = x.astype(jnp.float32)
    y = xf * lax.rsqrt(jnp.mean(xf * xf, axis=-1, keepdims=True) + NORM_EPS)
    return (y * g.astype(jnp.float32)).astype(x.dtype)


def causal_conv1d(x, w, b):
    c = x.shape[-1]
    y = lax.conv_general_dilated(x, w[:, None, :].astype(x.dtype), window_strides=(1,),
                                 padding=[(CONV_WIDTH - 1, 0)],
                                 dimension_numbers=('NWC', 'WIO', 'NWC'),
                                 feature_group_count=c)
    return y + b.astype(x.dtype)


def rg_lru(x, wa, ba, wx, bx, lam):
    bsz, seq, _ = x.shape
    xh = x.reshape(bsz, seq, LRU_HEADS, LRU_HEAD_DIM)
    r = jax.nn.sigmoid(jnp.einsum('bshi,hij->bshj', xh, wa).reshape(bsz, seq, D_LRU) + ba)
    i = jax.nn.sigmoid(jnp.einsum('bshi,hij->bshj', xh, wx).reshape(bsz, seq, D_LRU) + bx)
    log_a = -LRU_C * r.astype(jnp.float32) * jax.nn.softplus(-lam.astype(jnp.float32))
    a = jnp.exp(log_a)
    b_in = jnp.sqrt(-jnp.expm1(2.0 * log_a)) * (i * x).astype(jnp.float32)

    def combine(c1, c2):
        a1, b1 = c1
        a2, b2 = c2
        return a1 * a2, a2 * b1 + b2

    _, h = lax.associative_scan(combine, (a, b_in), axis=1)
    return h.astype(x.dtype)


def rotary(x, positions):
    half = RET_HEAD_DIM // 2
    inv_freq = ROPE_THETA ** (-jnp.arange(half, dtype=jnp.float32) / half)
    ang = positions.astype(jnp.float32)[..., None] * inv_freq
    cos = jnp.cos(ang)[:, :, None, :]
    sin = jnp.sin(ang)[:, :, None, :]
    xf = x.astype(jnp.float32)
    x1, x2 = xf[..., :half], xf[..., half:]
    return jnp.concatenate([x1 * cos - x2 * sin, x2 * cos + x1 * sin], axis=-1)


def chunkwise_retention(q, k, v):
    bsz, seq, _, _ = q.shape
    nc = seq // RET_CHUNK

    def to_chunks(t):
        return t.reshape(bsz, nc, RET_CHUNK, RET_HEADS, RET_HEAD_DIM).transpose(0, 3, 1, 2, 4)

    qc, kc, vc = to_chunks(q), to_chunks(k), to_chunks(v)
    log_gamma = jnp.log1p(-jnp.exp2(-5.0 - jnp.arange(RET_HEADS, dtype=jnp.float32)))
    pos = jnp.arange(RET_CHUNK, dtype=jnp.float32)
    rel = pos[:, None] - pos[None, :]
    causal = rel >= 0
    decay = jnp.where(causal, jnp.exp(log_gamma[:, None, None] * jnp.where(causal, rel, 0.0)), 0.0)
    scores = jnp.einsum('bhncd,bhnmd->bhncm', qc, kc) * decay[None, :, None]
    intra = jnp.einsum('bhncm,bhnmd->bhncd', scores, vc)
    zeta = jnp.exp(log_gamma[:, None] * (RET_CHUNK - 1.0 - pos))
    chunk_kv = jnp.einsum('bhnmd,bhnme->bhnde', kc * zeta[None, :, None, :, None], vc)
    chunk_decay = jnp.exp(log_gamma * RET_CHUNK)[None, :, None, None]

    def step(state, kv_n):
        return chunk_decay * state + kv_n, state

    init = jnp.zeros((bsz, RET_HEADS, RET_HEAD_DIM, RET_HEAD_DIM), jnp.float32)
    _, prev = lax.scan(step, init, jnp.moveaxis(chunk_kv, 2, 0))
    prev = jnp.moveaxis(prev, 0, 2)
    xi = jnp.exp(log_gamma[:, None] * (pos + 1.0))
    cross = jnp.einsum('bhncd,bhnde->bhnce', qc, prev) * xi[None, :, None, :, None]
    out = intra + cross
    return out.transpose(0, 2, 3, 1, 4).reshape(bsz, seq, RET_HEADS, RET_HEAD_DIM)


def head_groupnorm(o, g, b):
    mu = jnp.mean(o, axis=-1, keepdims=True)
    var = jnp.mean(jnp.square(o - mu), axis=-1, keepdims=True)
    on = (o - mu) * lax.rsqrt(var + NORM_EPS)
    bsz, seq = o.shape[:2]
    return on.reshape(bsz, seq, D_RET) * g.astype(jnp.float32) + b.astype(jnp.float32)


def hybrid_mixer(x, positions, norm_g, w_in, conv_w, conv_b, lru_wa, lru_ba, lru_wx, lru_bx,
                 lru_lambda, lru_norm_g, ret_norm_g, ret_norm_b, w_out):
    bsz, seq, _ = x.shape
    h = rmsnorm(x, norm_g)
    proj = h @ w_in
    lru_x, lru_gate, q, k, v, g = jnp.split(
        proj, [D_LRU, 2 * D_LRU, 2 * D_LRU + D_RET, 2 * D_LRU + 2 * D_RET, 2 * D_LRU + 3 * D_RET], axis=-1)
    hs = rg_lru(causal_conv1d(lru_x, conv_w, conv_b), lru_wa, lru_ba, lru_wx, lru_bx, lru_lambda)
    y_lru = rmsnorm(hs * jax.nn.gelu(lru_gate), lru_norm_g)
    qh = rotary(q.reshape(bsz, seq, RET_HEADS, RET_HEAD_DIM), positions)
    kh = rotary(k.reshape(bsz, seq, RET_HEADS, RET_HEAD_DIM), positions) * (RET_HEAD_DIM ** -0.5)
    vh = v.reshape(bsz, seq, RET_HEADS, RET_HEAD_DIM).astype(jnp.float32)
    o = head_groupnorm(chunkwise_retention(qh, kh, vh), ret_norm_g, ret_norm_b)
    y_ret = (jax.nn.silu(g.astype(jnp.float32)) * o).astype(x.dtype)
    y = jnp.concatenate([y_lru, y_ret], axis=-1) @ w_out
    return x + y.astype(x.dtype)


def moe_ffn(hf, top_idx, gates, w_gate, b_gate, w_up, b_up, w_down, b_down):
    n_tok, d = hf.shape
    n_assign = n_tok * TOP_K
    n_pad = n_assign + N_EXPERTS * MOE_BLOCK
    n_blocks = n_pad // MOE_BLOCK
    flat_e = top_idx.reshape(-1).astype(jnp.int32)
    flat_tok = jnp.arange(n_assign, dtype=jnp.int32) // TOP_K
    flat_w = gates.reshape(-1)
    order = jnp.argsort(flat_e)
    sorted_e = flat_e[order]
    counts = jnp.bincount(flat_e, length=N_EXPERTS)
    padded = (counts + MOE_BLOCK - 1) // MOE_BLOCK * MOE_BLOCK
    start = jnp.cumsum(counts) - counts
    pstart = jnp.cumsum(padded) - padded
    pend = pstart + padded
    dest = pstart[sorted_e] + jnp.arange(n_assign, dtype=jnp.int32) - start[sorted_e]
    row_tok = jnp.full((n_pad,), n_tok, jnp.int32).at[dest].set(flat_tok[order])
    row_w = jnp.zeros((n_pad,), flat_w.dtype).at[dest].set(flat_w[order])
    block_start = jnp.arange(n_blocks, dtype=jnp.int32) * MOE_BLOCK
    block_e = jnp.minimum(jnp.sum(block_start[:, None] >= pend[None, :], axis=1), N_EXPERTS - 1)
    x_pad = jnp.concatenate([hf, jnp.zeros((1, d), hf.dtype)], axis=0)
    xb = x_pad[row_tok].reshape(n_blocks, MOE_BLOCK, d)

    def expert_block(args):
        xblk, e = args
        gate = jnp.minimum(xblk @ w_gate[e] + b_gate[e], SWIGLU_LIMIT)
        up = jnp.clip(xblk @ w_up[e] + b_up[e], -SWIGLU_LIMIT, SWIGLU_LIMIT)
        act = gate * jax.nn.sigmoid(SWIGLU_ALPHA * gate) * (up + 1.0)
        return act @ w_down[e] + b_down[e]

    yb = lax.map(expert_block, (xb, block_e)).reshape(n_pad, d)
    y = jnp.zeros((n_tok + 1, d), yb.dtype).at[row_tok].add(yb * row_w[:, None].astype(yb.dtype))
    return y[:n_tok]


def moe_layer(x, norm_g, router_w, router_b, w_gate, b_gate, w_up, b_up, w_down, b_down):
    bsz, seq, d = x.shape
    hf = rmsnorm(x, norm_g).reshape(bsz * seq, d)
    logits = (hf @ router_w + router_b).astype(jnp.float32)
    top_vals, top_idx = lax.top_k(logits, TOP_K)
    gates = jax.nn.softmax(top_vals, axis=-1)
    y = moe_ffn(hf, top_idx, gates, w_gate, b_gate, w_up, b_up, w_down, b_down)
    return x + y.reshape(bsz, seq, d).astype(x.dtype)


def setup_inputs(seed: int = 0) -> dict:
    key = jax.random.key(seed)
    ks = jax.random.split(key, 24)
    f32 = jnp.float32
    L = DEPTH

    def nrm(k, shape, scale):
        return jax.random.normal(k, shape, f32) * scale

    x = nrm(ks[0], (BATCH, SEQ, D_MODEL), 1.0)
    positions = (jax.random.randint(ks[1], (BATCH, 1), 0, MAX_POS_OFFSET, dtype=jnp.int32)
                 + jnp.arange(SEQ, dtype=jnp.int32)[None, :])
    u = jax.random.uniform(ks[9], (L, D_LRU), f32, 0.9, 0.999)
    a0 = u ** (1.0 / LRU_C)
    return {
        'x': x,
        'positions': positions,
        'attn_norm_g': 1.0 + nrm(ks[2], (L, D_MODEL), 0.02),
        'w_in': nrm(ks[3], (L, D_MODEL, D_IN), D_MODEL ** -0.5),
        'conv_w': nrm(ks[4], (L, CONV_WIDTH, D_LRU), CONV_WIDTH ** -0.5),
        'conv_b': nrm(ks[5], (L, D_LRU), 0.01),
        'lru_wa': nrm(ks[6], (L, LRU_HEADS, LRU_HEAD_DIM, LRU_HEAD_DIM), LRU_HEAD_DIM ** -0.5),
        'lru_ba': nrm(ks[7], (L, D_LRU), 0.01),
        'lru_wx': nrm(ks[8], (L, LRU_HEADS, LRU_HEAD_DIM, LRU_HEAD_DIM), LRU_HEAD_DIM ** -0.5),
        'lru_bx': nrm(ks[10], (L, D_LRU), 0.01),
        'lru_lambda': jnp.log(a0) - jnp.log1p(-a0),
        'lru_norm_g': 1.0 + nrm(ks[11], (L, D_LRU), 0.02),
        'ret_norm_g': 1.0 + nrm(ks[12], (L, D_RET), 0.02),
        'ret_norm_b': nrm(ks[13], (L, D_RET), 0.01),
        'w_out': nrm(ks[14], (L, D_MIX, D_MODEL), D_MIX ** -0.5),
        'ffn_norm_g': 1.0 + nrm(ks[15], (L, D_MODEL), 0.02),
        'router_w': nrm(ks[16], (L, D_MODEL, N_EXPERTS), D_MODEL ** -0.5),
        'router_b': nrm(ks[17], (L, N_EXPERTS), 0.01),
        'moe_w_gate': nrm(ks[18], (L, N_EXPERTS, D_MODEL, D_FF), D_MODEL ** -0.5),
        'moe_b_gate': nrm(ks[19], (L, N_EXPERTS, D_FF), 0.01),
        'moe_w_up': nrm(ks[20], (L, N_EXPERTS, D_MODEL, D_FF), D_MODEL ** -0.5),
        'moe_b_up': nrm(ks[21], (L, N_EXPERTS, D_FF), 0.01),
        'moe_w_down': nrm(ks[22], (L, N_EXPERTS, D_FF, D_MODEL), D_FF ** -0.5),
        'moe_b_down': nrm(ks[23], (L, N_EXPERTS, D_MODEL), 0.01),
        'final_norm_g': 1.0 + nrm(jax.random.fold_in(key, 99), (D_MODEL,), 0.02),
    }


def reference(x, positions, attn_norm_g, w_in, conv_w, conv_b, lru_wa, lru_ba, lru_wx, lru_bx,
              lru_lambda, lru_norm_g, ret_norm_g, ret_norm_b, w_out, ffn_norm_g, router_w, router_b,
              moe_w_gate, moe_b_gate, moe_w_up, moe_b_up, moe_w_down, moe_b_down, final_norm_g):
    for l in range(DEPTH):
        x = hybrid_mixer(x, positions, attn_norm_g[l], w_in[l], conv_w[l], conv_b[l], lru_wa[l], lru_ba[l],
                         lru_wx[l], lru_bx[l], lru_lambda[l], lru_norm_g[l], ret_norm_g[l], ret_norm_b[l],
                         w_out[l])
        x = moe_layer(x, ffn_norm_g[l], router_w[l], router_b[l], moe_w_gate[l], moe_b_gate[l],
                      moe_w_up[l], moe_b_up[l], moe_w_down[l], moe_b_down[l])
    return rmsnorm(x, final_norm_g)
```

```python
import functools
import math

import numpy as np
import jax
import jax.numpy as jnp
from jax import lax
from jax.experimental import pallas as pl
from jax.experimental.pallas import tpu as pltpu

F32 = jnp.float32
BF16 = jnp.bfloat16

D_MODEL = 1024
D_LRU = 512
D_RET = 512
LRU_HEADS = 8
LRU_HEAD_DIM = D_LRU // LRU_HEADS
CONV_WIDTH = 4
LRU_C = 8.0
RET_HEADS = 4
RET_HEAD_DIM = D_RET // RET_HEADS
ROPE_THETA = 10000.0
D_IN = 2 * D_LRU + 4 * D_RET
N_EXPERTS = 32
TOP_K = 4
SWIGLU_ALPHA = 1.702
SWIGLU_LIMIT = 7.0
NORM_EPS = 1e-5

LANES = 128
SUBLANES = 8
VMEM_LIMIT = 48 * 1024 * 1024

TM_PROJ = 512
TC_SEQ = 256
BM = 512
TD = 256
TCMB = 128

LOG_GAMMA = [math.log1p(-(2.0 ** (-5.0 - h))) for h in range(RET_HEADS)]


def _cparams(sem):
    return pltpu.CompilerParams(dimension_semantics=sem, vmem_limit_bytes=VMEM_LIMIT)


def _inproj_kernel(x_ref, g_ref, w_ref, o_ref):
    x = x_ref[...]
    ms = jnp.mean(x * x, axis=-1, keepdims=True)
    h = x * lax.rsqrt(ms + NORM_EPS) * g_ref[...]
    o_ref[...] = jnp.dot(h.astype(BF16), w_ref[...],
                         preferred_element_type=F32).astype(o_ref.dtype)


def _inproj(x2, g, w_in_bf16):
    t = x2.shape[0]
    return pl.pallas_call(
        _inproj_kernel,
        grid=(t // TM_PROJ,),
        in_specs=[
            pl.BlockSpec((TM_PROJ, D_MODEL), lambda i: (i, 0)),
            pl.BlockSpec((1, D_MODEL), lambda i: (0, 0)),
            pl.BlockSpec((D_MODEL, D_IN), lambda i: (0, 0)),
        ],
        out_specs=pl.BlockSpec((TM_PROJ, D_IN), lambda i: (i, 0)),
        out_shape=jax.ShapeDtypeStruct((t, D_IN), BF16),
        compiler_params=_cparams(("arbitrary",)),
        name="inproj",
    )(x2, g, w_in_bf16)


def _lru_kernel(x_ref, gate_ref, cw_ref, cb_ref, wg_ref, ba_ref, bx_ref, sp_ref, ng_ref,
                o_ref, xext_ref, h_ref):
    s = pl.program_id(1)
    tc = x_ref.shape[0]

    @pl.when(s == 0)
    def _():
        xext_ref[0:SUBLANES, :] = jnp.zeros((SUBLANES, D_LRU), F32)
        h_ref[...] = jnp.zeros_like(h_ref)

    xext_ref[SUBLANES:SUBLANES + tc, :] = x_ref[...].astype(F32)
    xc = cb_ref[...] + cw_ref[CONV_WIDTH - 1:CONV_WIDTH, :] * xext_ref[SUBLANES:SUBLANES + tc, :]
    for j in range(CONV_WIDTH - 1):
        off = SUBLANES - (CONV_WIDTH - 1) + j
        xc = xc + cw_ref[j:j + 1, :] * xext_ref[off:off + tc, :]
    xext_ref[0:SUBLANES, :] = xext_ref[tc:tc + SUBLANES, :]

    gates = jnp.dot(xc.astype(BF16), wg_ref[...], preferred_element_type=F32)
    r = jax.nn.sigmoid(gates[:, :D_LRU] + ba_ref[...])
    ig = jax.nn.sigmoid(gates[:, D_LRU:] + bx_ref[...])
    log_a = (-LRU_C) * r * sp_ref[...]
    a = jnp.exp(log_a)
    b = jnp.sqrt(1.0 - a * a) * (ig * xc)

    rows = lax.broadcasted_iota(jnp.int32, (tc, 1), 0)
    d = 1
    while d < tc:
        a_s = pltpu.roll(a, d, 0)
        b_s = pltpu.roll(b, d, 0)
        m = rows >= d
        b = jnp.where(m, a * b_s + b, b)
        a = jnp.where(m, a * a_s, a)
        d *= 2
    h = b + a * h_ref[0:1, :]
    h_ref[0:1, :] = h[tc - 1:tc, :]

    y = h * jax.nn.gelu(gate_ref[...].astype(F32))
    ms = jnp.mean(y * y, axis=-1, keepdims=True)
    o_ref[...] = (y * lax.rsqrt(ms + NORM_EPS) * ng_ref[...]).astype(o_ref.dtype)


def _lru_branch(proj, bsz, seq, conv_w, conv_b, wg_bd, ba, bx, softplus_neg_lam, norm_g):
    ns = seq // TC_SEQ
    row = lambda b, s: (b * ns + s, 0)
    const = lambda b, s: (0, 0)
    return pl.pallas_call(
        _lru_kernel,
        grid=(bsz, ns),
        in_specs=[
            pl.BlockSpec((TC_SEQ, D_LRU), lambda b, s: (b * ns + s, 0)),
            pl.BlockSpec((TC_SEQ, D_LRU), lambda b, s: (b * ns + s, 1)),
            pl.BlockSpec((CONV_WIDTH, D_LRU), const),
            pl.BlockSpec((1, D_LRU), const),
            pl.BlockSpec((D_LRU, 2 * D_LRU), const),
            pl.BlockSpec((1, D_LRU), const),
            pl.BlockSpec((1, D_LRU), const),
            pl.BlockSpec((1, D_LRU), const),
            pl.BlockSpec((1, D_LRU), const),
        ],
        out_specs=pl.BlockSpec((TC_SEQ, D_LRU), row),
        out_shape=jax.ShapeDtypeStruct((bsz * seq, D_LRU), BF16),
        scratch_shapes=[
            pltpu.VMEM((TC_SEQ + SUBLANES, D_LRU), F32),
            pltpu.VMEM((SUBLANES, D_LRU), F32),
        ],
        compiler_params=_cparams(("arbitrary", "arbitrary")),
        name="lru_branch",
    )(proj, proj, conv_w, conv_b, wg_bd, ba, bx, softplus_neg_lam, norm_g)


def _ret_kernel(pos_ref, invf_ref, sgn_ref, q_ref, k_ref, v_ref, g_ref, gg_ref, gb_ref,
                o_ref, state_ref, decay_ref):
    s = pl.program_id(1)
    tc = q_ref.shape[0]
    dh = RET_HEAD_DIM

    @pl.when(jnp.logical_and(pl.program_id(0) == 0, s == 0))
    def _():
        ri = lax.broadcasted_iota(jnp.int32, (tc, tc), 0)
        ci = lax.broadcasted_iota(jnp.int32, (tc, tc), 1)
        rel = (ri - ci).astype(F32)
        causal = rel >= 0.0
        relc = jnp.where(causal, rel, 0.0)
        for h in range(RET_HEADS):
            decay_ref[h] = jnp.where(causal, jnp.exp(LOG_GAMMA[h] * relc), 0.0)

    @pl.when(s == 0)
    def _():
        state_ref[...] = jnp.zeros_like(state_ref)

    ang = pos_ref[...].astype(F32) * invf_ref[...]
    cos = jnp.cos(ang)
    sin_signed = jnp.sin(ang) * sgn_ref[...]
    rowf = lax.broadcasted_iota(jnp.int32, (tc, 1), 0).astype(F32)
    scale = dh ** -0.5

    for h in range(RET_HEADS):
        sl = slice(h * dh, (h + 1) * dh)
        q = q_ref[:, sl].astype(F32)
        k = k_ref[:, sl].astype(F32)
        v = v_ref[:, sl]
        qr = q * cos + pltpu.roll(q, dh // 2, 1) * sin_signed
        kr = (k * cos + pltpu.roll(k, dh // 2, 1) * sin_signed) * scale
        qb = qr.astype(BF16)
        kb = kr.astype(BF16)
        lg = LOG_GAMMA[h]
        scores = lax.dot_general(qb, kb, (((1,), (1,)), ((), ())),
                                 preferred_element_type=F32) * decay_ref[h]
        intra = jnp.dot(scores.astype(BF16), v, preferred_element_type=F32)
        st = state_ref[h]
        xi = jnp.exp(lg * (rowf + 1.0))
        cross = jnp.dot(qb, st.astype(BF16), preferred_element_type=F32) * xi
        o = intra + cross
        zeta = jnp.exp(lg * (float(tc) - 1.0 - rowf))
        kz = (kr * zeta).astype(BF16)
        kv = lax.dot_general(kz, v, (((0,), (0,)), ((), ())), preferred_element_type=F32)
        state_ref[h] = math.exp(lg * tc) * st + kv

        mu = jnp.mean(o, axis=-1, keepdims=True)
        oc = o - mu
        var = jnp.mean(oc * oc, axis=-1, keepdims=True)
        on = oc * lax.rsqrt(var + NORM_EPS) * gg_ref[:, sl] + gb_ref[:, sl]
        gate = g_ref[:, sl].astype(F32)
        o_ref[:, sl] = (gate * jax.nn.sigmoid(gate) * on).astype(o_ref.dtype)


def _ret_branch(proj, pos_col, bsz, seq, inv_freq, sign, gn_g, gn_b):
    ns = seq // TC_SEQ
    const = lambda b, s: (0, 0)
    col = lambda c: (lambda b, s: (b * ns + s, c))
    return pl.pallas_call(
        _ret_kernel,
        grid=(bsz, ns),
        in_specs=[
            pl.BlockSpec((TC_SEQ, 1), lambda b, s: (b * ns + s, 0)),
            pl.BlockSpec((1, RET_HEAD_DIM), const),
            pl.BlockSpec((1, RET_HEAD_DIM), const),
            pl.BlockSpec((TC_SEQ, D_RET), col(2)),
            pl.BlockSpec((TC_SEQ, D_RET), col(3)),
            pl.BlockSpec((TC_SEQ, D_RET), col(4)),
            pl.BlockSpec((TC_SEQ, D_RET), col(5)),
            pl.BlockSpec((1, D_RET), const),
            pl.BlockSpec((1, D_RET), const),
        ],
        out_specs=pl.BlockSpec((TC_SEQ, D_RET), lambda b, s: (b * ns + s, 0)),
        out_shape=jax.ShapeDtypeStruct((bsz * seq, D_RET), BF16),
        scratch_shapes=[
            pltpu.VMEM((RET_HEADS, RET_HEAD_DIM, RET_HEAD_DIM), F32),
            pltpu.VMEM((RET_HEADS, TC_SEQ, TC_SEQ), F32),
        ],
        compiler_params=_cparams(("arbitrary", "arbitrary")),
        name="ret_branch",
    )(pos_col, inv_freq, sign, proj, proj, proj, proj, gn_g, gn_b)


def _outproj_router_kernel(x_ref, yl_ref, yr_ref, wo_ref, ng_ref, rw_ref, rb_ref, tri_ref,
                           x1_ref, hf_ref, meta_ref, cnt_ref, carry_ref):
    i = pl.program_id(0)
    tm = x_ref.shape[0]

    @pl.when(i == 0)
    def _():
        carry_ref[...] = jnp.zeros_like(carry_ref)

    y = jnp.dot(yl_ref[...], wo_ref[0:D_LRU, :], preferred_element_type=F32)
    y = y + jnp.dot(yr_ref[...], wo_ref[D_LRU:, :], preferred_element_type=F32)
    x1 = x_ref[...] + y
    x1_ref[...] = x1
    ms = jnp.mean(x1 * x1, axis=-1, keepdims=True)
    hf = x1 * lax.rsqrt(ms + NORM_EPS) * ng_ref[...]
    hf_ref[...] = hf

    logits = jnp.dot(hf.astype(BF16), rw_ref[...], preferred_element_type=F32) + rb_ref[...]
    lane = lax.broadcasted_iota(jnp.int32, (tm, LANES), 1)
    lane_f = lane.astype(F32)
    work = logits
    vals, idxs = [], []
    onehot = jnp.zeros((tm, LANES), F32)
    for _ in range(TOP_K):
        m = jnp.max(work, axis=-1, keepdims=True)
        idx = jnp.min(jnp.where(work == m, lane_f, float(LANES)), axis=-1, keepdims=True)
        sel = lane_f == idx
        work = jnp.where(sel, -jnp.inf, work)
        onehot = jnp.where(sel, 1.0, onehot)
        vals.append(m)
        idxs.append(idx)
    exps = [jnp.exp(v - vals[0]) for v in vals]
    denom = exps[0] + exps[1] + exps[2] + exps[3]
    gates = [e / denom for e in exps]

    before = jnp.dot(tri_ref[...], onehot.astype(BF16), preferred_element_type=F32)
    rank_mat = before + carry_ref[0:1, :]
    meta = jnp.zeros((tm, LANES), F32)
    for kk in range(TOP_K):
        rk = jnp.sum(jnp.where(lane_f == idxs[kk], rank_mat, 0.0), axis=-1, keepdims=True)
        meta = jnp.where(lane == kk, idxs[kk], meta)
        meta = jnp.where(lane == TOP_K + kk, rk, meta)
        meta = jnp.where(lane == 2 * TOP_K + kk, gates[kk], meta)
    meta_ref[...] = meta
    new_carry = carry_ref[0:1, :] + jnp.sum(onehot, axis=0, keepdims=True)
    carry_ref[0:1, :] = new_carry
    cnt_ref[...] = jnp.broadcast_to(new_carry, cnt_ref.shape)


def _outproj_router(x2, y_lru, y_ret, w_out_bf16, ng, rw_pad, rb_pad, tri):
    t = x2.shape[0]
    tm = TM_PROJ
    const = lambda i: (0, 0)
    row = lambda i: (i, 0)
    return pl.pallas_call(
        _outproj_router_kernel,
        grid=(t // tm,),
        in_specs=[
            pl.BlockSpec((tm, D_MODEL), row),
            pl.BlockSpec((tm, D_LRU), row),
            pl.BlockSpec((tm, D_RET), row),
            pl.BlockSpec((D_MODEL, D_MODEL), const),
            pl.BlockSpec((1, D_MODEL), const),
            pl.BlockSpec((D_MODEL, LANES), const),
            pl.BlockSpec((1, LANES), const),
            pl.BlockSpec((tm, tm), const),
        ],
        out_specs=[
            pl.BlockSpec((tm, D_MODEL), row),
            pl.BlockSpec((tm, D_MODEL), row),
            pl.BlockSpec((tm, LANES), row),
            pl.BlockSpec((SUBLANES, LANES), const),
        ],
        out_shape=[
            jax.ShapeDtypeStruct((t, D_MODEL), F32),
            jax.ShapeDtypeStruct((t, D_MODEL), F32),
            jax.ShapeDtypeStruct((t, LANES), F32),
            jax.ShapeDtypeStruct((SUBLANES, LANES), F32),
        ],
        scratch_shapes=[pltpu.VMEM((SUBLANES, LANES), F32)],
        compiler_params=_cparams(("arbitrary",)),
        name="outproj_router",
    )(x2, y_lru, y_ret, w_out_bf16, ng, rw_pad, rb_pad, tri)


def _dispatch_kernel(pend_ref, dest_ref, hf_hbm, xs_hbm, zero_ref, sem):
    i = pl.program_id(0)
    n_rows = dest_ref.shape[-1]
    td = n_rows // TOP_K
    nb = xs_hbm.shape[0] // BM

    @pl.when(i == 0)
    def _():
        zero_ref[...] = jnp.zeros_like(zero_ref)

        def zcopy(start):
            return pltpu.make_async_copy(
                zero_ref, xs_hbm.at[pl.ds(pl.multiple_of(start, BM), BM)], sem)

        for e in range(N_EXPERTS):
            zcopy(jnp.maximum(pend_ref[e] - BM, 0)).start()
        for e in range(N_EXPERTS):
            zcopy(jnp.maximum(pend_ref[e] - BM, 0)).wait()
        for b in range(nb - N_EXPERTS, nb):
            @pl.when(b * BM >= pend_ref[N_EXPERTS - 1])
            def _():
                zcopy(b * BM).start()
                zcopy(b * BM).wait()

    base = i * td

    def row_copy(r, kk):
        d = dest_ref[0, 0, r * TOP_K + kk]
        return pltpu.make_async_copy(hf_hbm.at[pl.ds(base + r, 1)], xs_hbm.at[pl.ds(d, 1)], sem)

    def issue(r, c):
        for kk in range(TOP_K):
            row_copy(r, kk).start()
        return c

    lax.fori_loop(0, td, issue, 0, unroll=8)

    def drain(r, c):
        for kk in range(TOP_K):
            row_copy(r, kk).wait()
        return c

    lax.fori_loop(0, td, drain, 0, unroll=8)


def _dispatch(pend, dest3, hf, n_pad_rows):
    nsteps = dest3.shape[0]
    return pl.pallas_call(
        _dispatch_kernel,
        grid_spec=pltpu.PrefetchScalarGridSpec(
            num_scalar_prefetch=1,
            grid=(nsteps,),
            in_specs=[
                pl.BlockSpec((1, 1, TD * TOP_K), lambda i, pe: (i, 0, 0), memory_space=pltpu.SMEM),
                pl.BlockSpec(memory_space=pl.ANY),
            ],
            out_specs=pl.BlockSpec(memory_space=pl.ANY),
            scratch_shapes=[pltpu.VMEM((BM, D_MODEL), F32), pltpu.SemaphoreType.DMA(())],
        ),
        out_shape=jax.ShapeDtypeStruct((n_pad_rows, D_MODEL), F32),
        compiler_params=_cparams(("arbitrary",)),
        name="dispatch",
    )(pend, dest3, hf)


def _ffn_kernel(be_ref, nv_ref, x_ref, wg_ref, bg_ref, wu_ref, bu_ref, wd_ref, bd_ref, o_ref):
    @pl.when(pl.program_id(0) < nv_ref[0])
    def _():
        x = x_ref[...].astype(BF16)
        g = jnp.dot(x, wg_ref[...], preferred_element_type=F32) + bg_ref[...]
        g = jnp.minimum(g, SWIGLU_LIMIT)
        u = jnp.dot(x, wu_ref[...], preferred_element_type=F32) + bu_ref[...]
        u = jnp.clip(u, -SWIGLU_LIMIT, SWIGLU_LIMIT)
        act = g * jax.nn.sigmoid(SWIGLU_ALPHA * g) * (u + 1.0)
        o_ref[...] = jnp.dot(act.astype(BF16), wd_ref[...], preferred_element_type=F32) + bd_ref[...]

    @pl.when(pl.program_id(0) >= nv_ref[0])
    def _():
        o_ref[...] = jnp.zeros_like(o_ref)


def _expert_ffn(block_e, nvalid, xs, wg, bg, wu, bu, wd, bd):
    n_rows = xs.shape[0]
    nb = n_rows // BM

    def blk(i, be, nv):
        return jnp.minimum(i, nv[0] - 1)

    xmap = lambda i, be, nv: (blk(i, be, nv), 0)
    wmap = lambda i, be, nv: (be[blk(i, be, nv)], 0, 0)
    return pl.pallas_call(
        _ffn_kernel,
        grid_spec=pltpu.PrefetchScalarGridSpec(
            num_scalar_prefetch=2,
            grid=(nb,),
            in_specs=[
                pl.BlockSpec((BM, D_MODEL), xmap),
                pl.BlockSpec((None, D_MODEL, D_MODEL), wmap),
                pl.BlockSpec((None, 1, D_MODEL), wmap),
                pl.BlockSpec((None, D_MODEL, D_MODEL), wmap),
                pl.BlockSpec((None, 1, D_MODEL), wmap),
                pl.BlockSpec((None, D_MODEL, D_MODEL), wmap),
                pl.BlockSpec((None, 1, D_MODEL), wmap),
            ],
            out_specs=pl.BlockSpec((BM, D_MODEL), lambda i, be, nv: (i, 0)),
        ),
        out_shape=jax.ShapeDtypeStruct((n_rows, D_MODEL), F32),
        compiler_params=_cparams(("arbitrary",)),
        name="expert_ffn",
    )(block_e, nvalid, xs, wg, bg, wu, bu, wd, bd)


def _combine_kernel(dest_ref, x1_ref, meta_ref, g_ref, yb_hbm, o_ref, buf_ref, sem):
    tc = x1_ref.shape[0]

    def row_copy(r, kk):
        d = dest_ref[0, 0, r * TOP_K + kk]
        return pltpu.make_async_copy(yb_hbm.at[pl.ds(d, 1)],
                                     buf_ref.at[pl.ds(kk * tc + r, 1)], sem)

    def issue(r, c):
        for kk in range(TOP_K):
            row_copy(r, kk).start()
        return c

    lax.fori_loop(0, tc, issue, 0, unroll=8)

    def drain(r, c):
        for kk in range(TOP_K):
            row_copy(r, kk).wait()
        return c

    lax.fori_loop(0, tc, drain, 0, unroll=8)

    acc = x1_ref[...]
    meta = meta_ref[...]
    for kk in range(TOP_K):
        w = meta[:, 2 * TOP_K + kk:2 * TOP_K + kk + 1]
        acc = acc + w * buf_ref[kk * tc:(kk + 1) * tc, :]
    ms = jnp.mean(acc * acc, axis=-1, keepdims=True)
    o_ref[...] = acc * lax.rsqrt(ms + NORM_EPS) * g_ref[...]


def _combine(dest3, x1, meta, final_g, yb):
    t = x1.shape[0]
    row = lambda i: (i, 0)
    return pl.pallas_call(
        _combine_kernel,
        grid=(t // TCMB,),
        in_specs=[
            pl.BlockSpec((1, 1, TCMB * TOP_K), lambda i: (i, 0, 0), memory_space=pltpu.SMEM),
            pl.BlockSpec((TCMB, D_MODEL), row),
            pl.BlockSpec((TCMB, LANES), row),
            pl.BlockSpec((1, D_MODEL), lambda i: (0, 0)),
            pl.BlockSpec(memory_space=pl.ANY),
        ],
        out_specs=pl.BlockSpec((TCMB, D_MODEL), row),
        out_shape=jax.ShapeDtypeStruct((t, D_MODEL), F32),
        scratch_shapes=[pltpu.VMEM((TOP_K * TCMB, D_MODEL), F32), pltpu.SemaphoreType.DMA(())],
        compiler_params=_cparams(("arbitrary",)),
        name="combine",
    )(dest3, x1, meta, final_g, yb)


def _block_diag(w):
    h, d, _ = w.shape
    eye = jnp.eye(h, dtype=w.dtype)
    return (eye[:, None, :, None] * w[:, :, None, :]).reshape(h * d, h * d)


def kernel(x, positions, attn_norm_g, w_in, conv_w, conv_b, lru_wa, lru_ba, lru_wx, lru_bx, lru_lambda, lru_norm_g, ret_norm_g, ret_norm_b, w_out, ffn_norm_g, router_w, router_b, moe_w_gate, moe_b_gate, moe_w_up, moe_b_up, moe_w_down, moe_b_down, final_norm_g):
    bsz, seq, d = x.shape
    depth = w_in.shape[0]
    t = bsz * seq
    assert depth == 1 and d == D_MODEL and seq % TC_SEQ == 0 and t % TM_PROJ == 0
    n_assign = t * TOP_K
    n_pad = n_assign + N_EXPERTS * BM
    nb = n_pad // BM

    half = RET_HEAD_DIM // 2
    inv_freq = ROPE_THETA ** (-jnp.arange(half, dtype=F32) / half)
    inv_freq = jnp.concatenate([inv_freq, inv_freq]).reshape(1, RET_HEAD_DIM)
    sign = jnp.concatenate([-jnp.ones((half,), F32), jnp.ones((half,), F32)]).reshape(1, RET_HEAD_DIM)
    pos_col = positions.reshape(t, 1).astype(jnp.int32)
    tri = jnp.tril(jnp.ones((TM_PROJ, TM_PROJ), F32), -1).astype(BF16)

    x2 = x.reshape(t, d)
    for l in range(depth):
        proj = _inproj(x2, attn_norm_g[l].reshape(1, d), w_in[l].astype(BF16))

        wg_bd = jnp.concatenate([_block_diag(lru_wa[l]), _block_diag(lru_wx[l])], axis=1).astype(BF16)
        lam = lru_lambda[l].astype(F32)
        sp = (jnp.maximum(-lam, 0.0) + jnp.log1p(jnp.exp(-jnp.abs(lam)))).reshape(1, D_LRU)
        y_lru = _lru_branch(proj, bsz, seq, conv_w[l], conv_b[l].reshape(1, D_LRU), wg_bd,
                            lru_ba[l].reshape(1, D_LRU), lru_bx[l].reshape(1, D_LRU), sp,
                            lru_norm_g[l].reshape(1, D_LRU))
        y_ret = _ret_branch(proj, pos_col, bsz, seq, inv_freq, sign,
                            ret_norm_g[l].reshape(1, D_RET), ret_norm_b[l].reshape(1, D_RET))

        rw_pad = jnp.zeros((d, LANES), F32).at[:, :N_EXPERTS].set(router_w[l]).astype(BF16)
        rb_pad = jnp.full((1, LANES), -1e30, F32).at[0, :N_EXPERTS].set(router_b[l])
        x1, hf, meta, cnt = _outproj_router(x2, y_lru, y_ret, w_out[l].astype(BF16),
                                            ffn_norm_g[l].reshape(1, d), rw_pad, rb_pad, tri)

        counts = cnt[0, :N_EXPERTS].astype(jnp.int32)
        padded = (counts + BM - 1) // BM * BM
        pend = jnp.cumsum(padded).astype(jnp.int32)
        pstart = pend - padded
        top_idx = meta[:, 0:TOP_K].astype(jnp.int32)
        rank = meta[:, TOP_K:2 * TOP_K].astype(jnp.int32)
        dest = pstart[top_idx] + rank
        block_start = jnp.arange(nb, dtype=jnp.int32) * BM
        block_e = jnp.minimum(jnp.sum(block_start[:, None] >= pend[None, :], axis=1),
                              N_EXPERTS - 1).astype(jnp.int32)
        nvalid = (pend[N_EXPERTS - 1:] // BM).astype(jnp.int32)

        xs = _dispatch(pend, dest.reshape(t // TD, 1, TD * TOP_K), hf, n_pad)
        yb = _expert_ffn(block_e, nvalid, xs,
                         moe_w_gate[l].astype(BF16), moe_b_gate[l].reshape(N_EXPERTS, 1, d),
                         moe_w_up[l].astype(BF16), moe_b_up[l].reshape(N_EXPERTS, 1, d),
                         moe_w_down[l].astype(BF16), moe_b_down[l].reshape(N_EXPERTS, 1, d))
        x2 = _combine(dest.reshape(t // TCMB, 1, TCMB * TOP_K), x1, meta,
                      final_norm_g.reshape(1, d), yb)
    return x2.reshape(bsz, seq, d)
```

```python
import functools
import math

import numpy as np
import jax
import jax.numpy as jnp
from jax import lax
from jax.experimental import pallas as pl
from jax.experimental.pallas import tpu as pltpu

F32 = jnp.float32
BF16 = jnp.bfloat16

D_MODEL = 1024
D_LRU = 512
D_RET = 512
LRU_HEADS = 8
LRU_HEAD_DIM = D_LRU // LRU_HEADS
CONV_WIDTH = 4
LRU_C = 8.0
RET_HEADS = 4
RET_HEAD_DIM = D_RET // RET_HEADS
ROPE_THETA = 10000.0
D_IN = 2 * D_LRU + 4 * D_RET
N_EXPERTS = 32
TOP_K = 4
SWIGLU_ALPHA = 1.702
SWIGLU_LIMIT = 7.0
NORM_EPS = 1e-5

LANES = 128
SUBLANES = 8
VMEM_LIMIT = 48 * 1024 * 1024

TM_PROJ = 512
TC_SEQ = 256
BM = 512
FFN_COLS = 512
FFN_VMEM_LIMIT = 56 * 1024 * 1024
TD = 256
TCMB = 128

LOG_GAMMA = [math.log1p(-(2.0 ** (-5.0 - h))) for h in range(RET_HEADS)]


def _cparams(sem):
    return pltpu.CompilerParams(dimension_semantics=sem, vmem_limit_bytes=VMEM_LIMIT)


def _inproj_kernel(x_ref, g_ref, w_ref, o_ref):
    x = x_ref[...]
    ms = jnp.mean(x * x, axis=-1, keepdims=True)
    h = x * lax.rsqrt(ms + NORM_EPS) * g_ref[...]
    o_ref[...] = jnp.dot(h.astype(BF16), w_ref[...],
                         preferred_element_type=F32).astype(o_ref.dtype)


def _inproj(x2, g, w_in_bf16):
    t = x2.shape[0]
    return pl.pallas_call(
        _inproj_kernel,
        grid=(t // TM_PROJ,),
        in_specs=[
            pl.BlockSpec((TM_PROJ, D_MODEL), lambda i: (i, 0)),
            pl.BlockSpec((1, D_MODEL), lambda i: (0, 0)),
            pl.BlockSpec((D_MODEL, D_IN), lambda i: (0, 0)),
        ],
        out_specs=pl.BlockSpec((TM_PROJ, D_IN), lambda i: (i, 0)),
        out_shape=jax.ShapeDtypeStruct((t, D_IN), BF16),
        compiler_params=_cparams(("arbitrary",)),
        name="inproj",
    )(x2, g, w_in_bf16)


def _lru_kernel(x_ref, gate_ref, cw_ref, cb_ref, wg_ref, ba_ref, bx_ref, sp_ref, ng_ref,
                o_ref, xext_ref, h_ref):
    s = pl.program_id(1)
    tc = x_ref.shape[0]

    @pl.when(s == 0)
    def _():
        xext_ref[0:SUBLANES, :] = jnp.zeros((SUBLANES, D_LRU), F32)
        h_ref[...] = jnp.zeros_like(h_ref)

    xext_ref[SUBLANES:SUBLANES + tc, :] = x_ref[...].astype(F32)
    xc = cb_ref[...] + cw_ref[CONV_WIDTH - 1:CONV_WIDTH, :] * xext_ref[SUBLANES:SUBLANES + tc, :]
    for j in range(CONV_WIDTH - 1):
        off = SUBLANES - (CONV_WIDTH - 1) + j
        xc = xc + cw_ref[j:j + 1, :] * xext_ref[off:off + tc, :]
    xext_ref[0:SUBLANES, :] = xext_ref[tc:tc + SUBLANES, :]

    gates = jnp.dot(xc.astype(BF16), wg_ref[...], preferred_element_type=F32)
    r = jax.nn.sigmoid(gates[:, :D_LRU] + ba_ref[...])
    ig = jax.nn.sigmoid(gates[:, D_LRU:] + bx_ref[...])
    log_a = (-LRU_C) * r * sp_ref[...]
    a = jnp.exp(log_a)
    b = jnp.sqrt(1.0 - a * a) * (ig * xc)

    rows = lax.broadcasted_iota(jnp.int32, (tc, 1), 0)
    d = 1
    while d < tc:
        a_s = pltpu.roll(a, d, 0)
        b_s = pltpu.roll(b, d, 0)
        m = rows >= d
        b = jnp.where(m, a * b_s + b, b)
        a = jnp.where(m, a * a_s, a)
        d *= 2
    h = b + a * h_ref[0:1, :]
    h_ref[0:1, :] = h[tc - 1:tc, :]

    y = h * jax.nn.gelu(gate_ref[...].astype(F32))
    ms = jnp.mean(y * y, axis=-1, keepdims=True)
    o_ref[...] = (y * lax.rsqrt(ms + NORM_EPS) * ng_ref[...]).astype(o_ref.dtype)


def _lru_branch(proj, bsz, seq, conv_w, conv_b, wg_bd, ba, bx, softplus_neg_lam, norm_g):
    ns = seq // TC_SEQ
    row = lambda b, s: (b * ns + s, 0)
    const = lambda b, s: (0, 0)
    return pl.pallas_call(
        _lru_kernel,
        grid=(bsz, ns),
        in_specs=[
            pl.BlockSpec((TC_SEQ, D_LRU), lambda b, s: (b * ns + s, 0)),
            pl.BlockSpec((TC_SEQ, D_LRU), lambda b, s: (b * ns + s, 1)),
            pl.BlockSpec((CONV_WIDTH, D_LRU), const),
            pl.BlockSpec((1, D_LRU), const),
            pl.BlockSpec((D_LRU, 2 * D_LRU), const),
            pl.BlockSpec((1, D_LRU), const),
            pl.BlockSpec((1, D_LRU), const),
            pl.BlockSpec((1, D_LRU), const),
            pl.BlockSpec((1, D_LRU), const),
        ],
        out_specs=pl.BlockSpec((TC_SEQ, D_LRU), row),
        out_shape=jax.ShapeDtypeStruct((bsz * seq, D_LRU), BF16),
        scratch_shapes=[
            pltpu.VMEM((TC_SEQ + SUBLANES, D_LRU), F32),
            pltpu.VMEM((SUBLANES, D_LRU), F32),
        ],
        compiler_params=_cparams(("arbitrary", "arbitrary")),
        name="lru_branch",
    )(proj, proj, conv_w, conv_b, wg_bd, ba, bx, softplus_neg_lam, norm_g)


def _ret_kernel(pos_ref, invf_ref, sgn_ref, q_ref, k_ref, v_ref, g_ref, gg_ref, gb_ref,
                o_ref, state_ref, decay_ref):
    s = pl.program_id(1)
    tc = q_ref.shape[0]
    dh = RET_HEAD_DIM

    @pl.when(jnp.logical_and(pl.program_id(0) == 0, s == 0))
    def _():
        ri = lax.broadcasted_iota(jnp.int32, (tc, tc), 0)
        ci = lax.broadcasted_iota(jnp.int32, (tc, tc), 1)
        rel = (ri - ci).astype(F32)
        causal = rel >= 0.0
        relc = jnp.where(causal, rel, 0.0)
        for h in range(RET_HEADS):
            decay_ref[h] = jnp.where(causal, jnp.exp(LOG_GAMMA[h] * relc), 0.0)

    @pl.when(s == 0)
    def _():
        state_ref[...] = jnp.zeros_like(state_ref)

    ang = pos_ref[...].astype(F32) * invf_ref[...]
    cos = jnp.cos(ang)
    sin_signed = jnp.sin(ang) * sgn_ref[...]
    rowf = lax.broadcasted_iota(jnp.int32, (tc, 1), 0).astype(F32)
    scale = dh ** -0.5

    for h in range(RET_HEADS):
        sl = slice(h * dh, (h + 1) * dh)
        q = q_ref[:, sl].astype(F32)
        k = k_ref[:, sl].astype(F32)
        v = v_ref[:, sl]
        qr = q * cos + pltpu.roll(q, dh // 2, 1) * sin_signed
        kr = (k * cos + pltpu.roll(k, dh // 2, 1) * sin_signed) * scale
        qb = qr.astype(BF16)
        kb = kr.astype(BF16)
        lg = LOG_GAMMA[h]
        scores = lax.dot_general(qb, kb, (((1,), (1,)), ((), ())),
                                 preferred_element_type=F32) * decay_ref[h]
        intra = jnp.dot(scores.astype(BF16), v, preferred_element_type=F32)
        st = state_ref[h]
        xi = jnp.exp(lg * (rowf + 1.0))
        cross = jnp.dot(qb, st.astype(BF16), preferred_element_type=F32) * xi
        o = intra + cross
        zeta = jnp.exp(lg * (float(tc) - 1.0 - rowf))
        kz = (kr * zeta).astype(BF16)
        kv = lax.dot_general(kz, v, (((0,), (0,)), ((), ())), preferred_element_type=F32)
        state_ref[h] = math.exp(lg * tc) * st + kv

        mu = jnp.mean(o, axis=-1, keepdims=True)
        oc = o - mu
        var = jnp.mean(oc * oc, axis=-1, keepdims=True)
        on = oc * lax.rsqrt(var + NORM_EPS) * gg_ref[:, sl] + gb_ref[:, sl]
        gate = g_ref[:, sl].astype(F32)
        o_ref[:, sl] = (gate * jax.nn.sigmoid(gate) * on).astype(o_ref.dtype)


def _ret_branch(proj, pos_col, bsz, seq, inv_freq, sign, gn_g, gn_b):
    ns = seq // TC_SEQ
    const = lambda b, s: (0, 0)
    col = lambda c: (lambda b, s: (b * ns + s, c))
    return pl.pallas_call(
        _ret_kernel,
        grid=(bsz, ns),
        in_specs=[
            pl.BlockSpec((TC_SEQ, 1), lambda b, s: (b * ns + s, 0)),
            pl.BlockSpec((1, RET_HEAD_DIM), const),
            pl.BlockSpec((1, RET_HEAD_DIM), const),
            pl.BlockSpec((TC_SEQ, D_RET), col(2)),
            pl.BlockSpec((TC_SEQ, D_RET), col(3)),
            pl.BlockSpec((TC_SEQ, D_RET), col(4)),
            pl.BlockSpec((TC_SEQ, D_RET), col(5)),
            pl.BlockSpec((1, D_RET), const),
            pl.BlockSpec((1, D_RET), const),
        ],
        out_specs=pl.BlockSpec((TC_SEQ, D_RET), lambda b, s: (b * ns + s, 0)),
        out_shape=jax.ShapeDtypeStruct((bsz * seq, D_RET), BF16),
        scratch_shapes=[
            pltpu.VMEM((RET_HEADS, RET_HEAD_DIM, RET_HEAD_DIM), F32),
            pltpu.VMEM((RET_HEADS, TC_SEQ, TC_SEQ), F32),
        ],
        compiler_params=_cparams(("arbitrary", "arbitrary")),
        name="ret_branch",
    )(pos_col, inv_freq, sign, proj, proj, proj, proj, gn_g, gn_b)


def _outproj_router_kernel(x_ref, yl_ref, yr_ref, wo_ref, ng_ref, rw_ref, rb_ref, tri_ref,
                           x1_ref, hf_ref, meta_ref, cnt_ref, carry_ref):
    i = pl.program_id(0)
    tm = x_ref.shape[0]

    @pl.when(i == 0)
    def _():
        carry_ref[...] = jnp.zeros_like(carry_ref)

    y = jnp.dot(yl_ref[...], wo_ref[0:D_LRU, :], preferred_element_type=F32)
    y = y + jnp.dot(yr_ref[...], wo_ref[D_LRU:, :], preferred_element_type=F32)
    x1 = x_ref[...] + y
    x1_ref[...] = x1
    ms = jnp.mean(x1 * x1, axis=-1, keepdims=True)
    hf = x1 * lax.rsqrt(ms + NORM_EPS) * ng_ref[...]
    hf_ref[...] = hf

    logits = jnp.dot(hf.astype(BF16), rw_ref[...], preferred_element_type=F32) + rb_ref[...]
    lane = lax.broadcasted_iota(jnp.int32, (tm, LANES), 1)
    lane_f = lane.astype(F32)
    work = logits
    vals, idxs = [], []
    onehot = jnp.zeros((tm, LANES), F32)
    for _ in range(TOP_K):
        m = jnp.max(work, axis=-1, keepdims=True)
        idx = jnp.min(jnp.where(work == m, lane_f, float(LANES)), axis=-1, keepdims=True)
        sel = lane_f == idx
        work = jnp.where(sel, -jnp.inf, work)
        onehot = jnp.where(sel, 1.0, onehot)
        vals.append(m)
        idxs.append(idx)
    exps = [jnp.exp(v - vals[0]) for v in vals]
    denom = exps[0] + exps[1] + exps[2] + exps[3]
    gates = [e / denom for e in exps]

    before = jnp.dot(tri_ref[...], onehot.astype(BF16), preferred_element_type=F32)
    rank_mat = before + carry_ref[0:1, :]
    meta = jnp.zeros((tm, LANES), F32)
    for kk in range(TOP_K):
        rk = jnp.sum(jnp.where(lane_f == idxs[kk], rank_mat, 0.0), axis=-1, keepdims=True)
        meta = jnp.where(lane == kk, idxs[kk], meta)
        meta = jnp.where(lane == TOP_K + kk, rk, meta)
        meta = jnp.where(lane == 2 * TOP_K + kk, gates[kk], meta)
    meta_ref[...] = meta
    new_carry = carry_ref[0:1, :] + jnp.sum(onehot, axis=0, keepdims=True)
    carry_ref[0:1, :] = new_carry
    cnt_ref[...] = jnp.broadcast_to(new_carry, cnt_ref.shape)


def _outproj_router(x2, y_lru, y_ret, w_out_bf16, ng, rw_pad, rb_pad, tri):
    t = x2.shape[0]
    tm = TM_PROJ
    const = lambda i: (0, 0)
    row = lambda i: (i, 0)
    return pl.pallas_call(
        _outproj_router_kernel,
        grid=(t // tm,),
        in_specs=[
            pl.BlockSpec((tm, D_MODEL), row),
            pl.BlockSpec((tm, D_LRU), row),
            pl.BlockSpec((tm, D_RET), row),
            pl.BlockSpec((D_MODEL, D_MODEL), const),
            pl.BlockSpec((1, D_MODEL), const),
            pl.BlockSpec((D_MODEL, LANES), const),
            pl.BlockSpec((1, LANES), const),
            pl.BlockSpec((tm, tm), const),
        ],
        out_specs=[
            pl.BlockSpec((tm, D_MODEL), row),
            pl.BlockSpec((tm, D_MODEL), row),
            pl.BlockSpec((tm, LANES), row),
            pl.BlockSpec((SUBLANES, LANES), const),
        ],
        out_shape=[
            jax.ShapeDtypeStruct((t, D_MODEL), F32),
            jax.ShapeDtypeStruct((t, D_MODEL), F32),
            jax.ShapeDtypeStruct((t, LANES), F32),
            jax.ShapeDtypeStruct((SUBLANES, LANES), F32),
        ],
        scratch_shapes=[pltpu.VMEM((SUBLANES, LANES), F32)],
        compiler_params=_cparams(("arbitrary",)),
        name="outproj_router",
    )(x2, y_lru, y_ret, w_out_bf16, ng, rw_pad, rb_pad, tri)


def _dispatch_kernel(pend_ref, dest_ref, hf_ref, xs_hbm, zero_ref, sem):
    i = pl.program_id(0)
    n_rows = dest_ref.shape[-1]
    td = n_rows // TOP_K
    nb = xs_hbm.shape[0] // BM

    @pl.when(i == 0)
    def _():
        zero_ref[...] = jnp.zeros_like(zero_ref)

        def zcopy(start):
            return pltpu.make_async_copy(
                zero_ref, xs_hbm.at[pl.ds(pl.multiple_of(start, BM), BM)], sem)

        for e in range(N_EXPERTS):
            zcopy(jnp.maximum(pend_ref[e] - BM, 0)).start()
        for e in range(N_EXPERTS):
            zcopy(jnp.maximum(pend_ref[e] - BM, 0)).wait()
        for b in range(nb - N_EXPERTS, nb):
            @pl.when(b * BM >= pend_ref[N_EXPERTS - 1])
            def _():
                zcopy(b * BM).start()
                zcopy(b * BM).wait()

    def row_copy(r, kk):
        d = dest_ref[0, 0, r * TOP_K + kk]
        return pltpu.make_async_copy(hf_ref.at[pl.ds(r, 1)], xs_hbm.at[pl.ds(d, 1)], sem)

    def issue(r, c):
        for kk in range(TOP_K):
            row_copy(r, kk).start(priority=kk % 2)
        return c

    lax.fori_loop(0, td, issue, 0, unroll=8)

    def drain(r, c):
        for kk in range(TOP_K):
            row_copy(r, kk).wait()
        return c

    lax.fori_loop(0, td, drain, 0, unroll=8)


def _dispatch(pend, dest3, hf, n_pad_rows):
    nsteps = dest3.shape[0]
    return pl.pallas_call(
        _dispatch_kernel,
        grid_spec=pltpu.PrefetchScalarGridSpec(
            num_scalar_prefetch=1,
            grid=(nsteps,),
            in_specs=[
                pl.BlockSpec((1, 1, TD * TOP_K), lambda i, pe: (i, 0, 0), memory_space=pltpu.SMEM),
                pl.BlockSpec((TD, D_MODEL), lambda i, pe: (i, 0)),
            ],
            out_specs=pl.BlockSpec(memory_space=pl.ANY),
            scratch_shapes=[pltpu.VMEM((BM, D_MODEL), F32), pltpu.SemaphoreType.DMA(())],
        ),
        out_shape=jax.ShapeDtypeStruct((n_pad_rows, D_MODEL), F32),
        compiler_params=_cparams(("arbitrary",)),
        name="dispatch",
    )(pend, dest3, hf)


def _ffn_kernel(be_ref, nv_ref, x_ref, wg_ref, bg_ref, wu_ref, bu_ref, wd_ref, bd_ref, o_ref,
                wbf_ref):
    i = pl.program_id(0)

    @pl.when(i < nv_ref[0])
    def _():
        @pl.when(jnp.logical_or(i == 0, be_ref[i] != be_ref[jnp.maximum(i - 1, 0)]))
        def _():
            wbf_ref[0] = wg_ref[...].astype(BF16)
            wbf_ref[1] = wu_ref[...].astype(BF16)
            wbf_ref[2] = wd_ref[...].astype(BF16)

        x = x_ref[...].astype(BF16)
        y = jnp.broadcast_to(bd_ref[...], o_ref.shape)
        for c in range(D_MODEL // FFN_COLS):
            cs = slice(c * FFN_COLS, (c + 1) * FFN_COLS)
            g = jnp.dot(x, wbf_ref[0, :, cs], preferred_element_type=F32) + bg_ref[:, cs]
            g = jnp.minimum(g, SWIGLU_LIMIT)
            u = jnp.dot(x, wbf_ref[1, :, cs], preferred_element_type=F32) + bu_ref[:, cs]
            u = jnp.clip(u, -SWIGLU_LIMIT, SWIGLU_LIMIT)
            act = g * jax.nn.sigmoid(SWIGLU_ALPHA * g) * (u + 1.0)
            y = y + jnp.dot(act.astype(BF16), wbf_ref[2, cs, :], preferred_element_type=F32)
        o_ref[...] = y

    @pl.when(pl.program_id(0) >= nv_ref[0])
    def _():
        o_ref[...] = jnp.zeros_like(o_ref)


def _expert_ffn(block_e, nvalid, xs, wg, bg, wu, bu, wd, bd):
    n_rows = xs.shape[0]
    nb = n_rows // BM

    def blk(i, be, nv):
        return jnp.minimum(i, nv[0] - 1)

    xmap = lambda i, be, nv: (blk(i, be, nv), 0)
    wmap = lambda i, be, nv: (be[blk(i, be, nv)], 0, 0)
    return pl.pallas_call(
        _ffn_kernel,
        grid_spec=pltpu.PrefetchScalarGridSpec(
            num_scalar_prefetch=2,
            grid=(nb,),
            in_specs=[
                pl.BlockSpec((BM, D_MODEL), xmap),
                pl.BlockSpec((None, D_MODEL, D_MODEL), wmap),
                pl.BlockSpec((None, 1, D_MODEL), wmap),
                pl.BlockSpec((None, D_MODEL, D_MODEL), wmap),
                pl.BlockSpec((None, 1, D_MODEL), wmap),
                pl.BlockSpec((None, D_MODEL, D_MODEL), wmap),
                pl.BlockSpec((None, 1, D_MODEL), wmap),
            ],
            out_specs=pl.BlockSpec((BM, D_MODEL), lambda i, be, nv: (i, 0)),
            scratch_shapes=[pltpu.VMEM((3, D_MODEL, D_MODEL), BF16)],
        ),
        out_shape=jax.ShapeDtypeStruct((n_rows, D_MODEL), F32),
        compiler_params=pltpu.CompilerParams(dimension_semantics=("arbitrary",),
                                             vmem_limit_bytes=FFN_VMEM_LIMIT),
        name="expert_ffn",
    )(block_e, nvalid, xs, wg, bg, wu, bu, wd, bd)


def _combine_kernel(dest_ref, x1_ref, meta_ref, g_ref, yb_hbm, o_ref, buf_ref, sem):
    tc = x1_ref.shape[0]

    def row_copy(r, kk):
        d = dest_ref[0, 0, r * TOP_K + kk]
        return pltpu.make_async_copy(yb_hbm.at[pl.ds(d, 1)],
                                     buf_ref.at[pl.ds(kk * tc + r, 1)], sem)

    def issue(r, c):
        for kk in range(TOP_K):
            row_copy(r, kk).start(priority=kk % 2)
        return c

    lax.fori_loop(0, tc, issue, 0, unroll=8)

    def drain(r, c):
        for kk in range(TOP_K):
            row_copy(r, kk).wait()
        return c

    lax.fori_loop(0, tc, drain, 0, unroll=8)

    acc = x1_ref[...]
    meta = meta_ref[...]
    for kk in range(TOP_K):
        w = meta[:, 2 * TOP_K + kk:2 * TOP_K + kk + 1]
        acc = acc + w * buf_ref[kk * tc:(kk + 1) * tc, :]
    ms = jnp.mean(acc * acc, axis=-1, keepdims=True)
    o_ref[...] = acc * lax.rsqrt(ms + NORM_EPS) * g_ref[...]


def _combine(dest3, x1, meta, final_g, yb):
    t = x1.shape[0]
    row = lambda i: (i, 0)
    return pl.pallas_call(
        _combine_kernel,
        grid=(t // TCMB,),
        in_specs=[
            pl.BlockSpec((1, 1, TCMB * TOP_K), lambda i: (i, 0, 0), memory_space=pltpu.SMEM),
            pl.BlockSpec((TCMB, D_MODEL), row),
            pl.BlockSpec((TCMB, LANES), row),
            pl.BlockSpec((1, D_MODEL), lambda i: (0, 0)),
            pl.BlockSpec(memory_space=pl.ANY),
        ],
        out_specs=pl.BlockSpec((TCMB, D_MODEL), row),
        out_shape=jax.ShapeDtypeStruct((t, D_MODEL), F32),
        scratch_shapes=[pltpu.VMEM((TOP_K * TCMB, D_MODEL), F32), pltpu.SemaphoreType.DMA(())],
        compiler_params=_cparams(("arbitrary",)),
        name="combine",
    )(dest3, x1, meta, final_g, yb)


def _block_diag(w):
    h, d, _ = w.shape
    eye = jnp.eye(h, dtype=w.dtype)
    return (eye[:, None, :, None] * w[:, :, None, :]).reshape(h * d, h * d)


def kernel(x, positions, attn_norm_g, w_in, conv_w, conv_b, lru_wa, lru_ba, lru_wx, lru_bx, lru_lambda, lru_norm_g, ret_norm_g, ret_norm_b, w_out, ffn_norm_g, router_w, router_b, moe_w_gate, moe_b_gate, moe_w_up, moe_b_up, moe_w_down, moe_b_down, final_norm_g):
    bsz, seq, d = x.shape
    depth = w_in.shape[0]
    t = bsz * seq
    assert depth == 1 and d == D_MODEL and seq % TC_SEQ == 0 and t % TM_PROJ == 0
    n_assign = t * TOP_K
    n_pad = n_assign + N_EXPERTS * BM
    nb = n_pad // BM

    half = RET_HEAD_DIM // 2
    inv_freq = ROPE_THETA ** (-jnp.arange(half, dtype=F32) / half)
    inv_freq = jnp.concatenate([inv_freq, inv_freq]).reshape(1, RET_HEAD_DIM)
    sign = jnp.concatenate([-jnp.ones((half,), F32), jnp.ones((half,), F32)]).reshape(1, RET_HEAD_DIM)
    pos_col = positions.reshape(t, 1).astype(jnp.int32)
    tri = jnp.tril(jnp.ones((TM_PROJ, TM_PROJ), F32), -1).astype(BF16)

    x2 = x.reshape(t, d)
    for l in range(depth):
        proj = _inproj(x2, attn_norm_g[l].reshape(1, d), w_in[l].astype(BF16))

        wg_bd = jnp.concatenate([_block_diag(lru_wa[l]), _block_diag(lru_wx[l])], axis=1).astype(BF16)
        lam = lru_lambda[l].astype(F32)
        sp = (jnp.maximum(-lam, 0.0) + jnp.log1p(jnp.exp(-jnp.abs(lam)))).reshape(1, D_LRU)
        y_lru = _lru_branch(proj, bsz, seq, conv_w[l], conv_b[l].reshape(1, D_LRU), wg_bd,
                            lru_ba[l].reshape(1, D_LRU), lru_bx[l].reshape(1, D_LRU), sp,
                            lru_norm_g[l].reshape(1, D_LRU))
        y_ret = _ret_branch(proj, pos_col, bsz, seq, inv_freq, sign,
                            ret_norm_g[l].reshape(1, D_RET), ret_norm_b[l].reshape(1, D_RET))

        rw_pad = jnp.zeros((d, LANES), F32).at[:, :N_EXPERTS].set(router_w[l]).astype(BF16)
        rb_pad = jnp.full((1, LANES), -1e30, F32).at[0, :N_EXPERTS].set(router_b[l])
        x1, hf, meta, cnt = _outproj_router(x2, y_lru, y_ret, w_out[l].astype(BF16),
                                            ffn_norm_g[l].reshape(1, d), rw_pad, rb_pad, tri)

        counts = cnt[0, :N_EXPERTS].astype(jnp.int32)
        padded = (counts + BM - 1) // BM * BM
        pend = jnp.cumsum(padded).astype(jnp.int32)
        pstart = pend - padded
        top_idx = meta[:, 0:TOP_K].astype(jnp.int32)
        rank = meta[:, TOP_K:2 * TOP_K].astype(jnp.int32)
        dest = pstart[top_idx] + rank
        block_start = jnp.arange(nb, dtype=jnp.int32) * BM
        block_e = jnp.minimum(jnp.sum(block_start[:, None] >= pend[None, :], axis=1),
                              N_EXPERTS - 1).astype(jnp.int32)
        nvalid = (pend[N_EXPERTS - 1:] // BM).astype(jnp.int32)

        xs = _dispatch(pend, dest.reshape(t // TD, 1, TD * TOP_K), hf, n_pad)
        yb = _expert_ffn(block_e, nvalid, xs,
                         moe_w_gate[l], moe_b_gate[l].reshape(N_EXPERTS, 1, d),
                         moe_w_up[l], moe_b_up[l].reshape(N_EXPERTS, 1, d),
                         moe_w_down[l], moe_b_down[l].reshape(N_EXPERTS, 1, d))
        x2 = _combine(dest.reshape(t // TCMB, 1, TCMB * TOP_K), x1, meta,
                      final_norm_g.reshape(1, d), yb)
    return x2.reshape(bsz, seq, d)
```

```python
import functools
import math

import numpy as np
import jax
import jax.numpy as jnp
from jax import lax
from jax.experimental import pallas as pl
from jax.experimental.pallas import tpu as pltpu

F32 = jnp.float32
BF16 = jnp.bfloat16

D_MODEL = 1024
D_LRU = 512
D_RET = 512
LRU_HEADS = 8
LRU_HEAD_DIM = D_LRU // LRU_HEADS
CONV_WIDTH = 4
LRU_C = 8.0
RET_HEADS = 4
RET_HEAD_DIM = D_RET // RET_HEADS
ROPE_THETA = 10000.0
D_IN = 2 * D_LRU + 4 * D_RET
N_EXPERTS = 32
TOP_K = 4
SWIGLU_ALPHA = 1.702
SWIGLU_LIMIT = 7.0
NORM_EPS = 1e-5

LANES = 128
SUBLANES = 8
VMEM_LIMIT = 48 * 1024 * 1024

TM_PROJ = 512
TC_SEQ = 256
BM = 512
FFN_COLS = 512
FFN_VMEM_LIMIT = 56 * 1024 * 1024
SEG = SUBLANES
RL = TM_PROJ * TOP_K + N_EXPERTS * SEG
PCOLS = 256

LOG_GAMMA = [math.log1p(-(2.0 ** (-5.0 - h))) for h in range(RET_HEADS)]


def _cparams(sem):
    return pltpu.CompilerParams(dimension_semantics=sem, vmem_limit_bytes=VMEM_LIMIT)


def _inproj_kernel(x_ref, g_ref, w_ref, o_ref):
    x = x_ref[...]
    ms = jnp.mean(x * x, axis=-1, keepdims=True)
    h = x * lax.rsqrt(ms + NORM_EPS) * g_ref[...]
    o_ref[...] = jnp.dot(h.astype(BF16), w_ref[...],
                         preferred_element_type=F32).astype(o_ref.dtype)


def _inproj(x2, g, w_in_bf16):
    t = x2.shape[0]
    return pl.pallas_call(
        _inproj_kernel,
        grid=(t // TM_PROJ,),
        in_specs=[
            pl.BlockSpec((TM_PROJ, D_MODEL), lambda i: (i, 0)),
            pl.BlockSpec((1, D_MODEL), lambda i: (0, 0)),
            pl.BlockSpec((D_MODEL, D_IN), lambda i: (0, 0)),
        ],
        out_specs=pl.BlockSpec((TM_PROJ, D_IN), lambda i: (i, 0)),
        out_shape=jax.ShapeDtypeStruct((t, D_IN), BF16),
        compiler_params=_cparams(("arbitrary",)),
        name="inproj",
    )(x2, g, w_in_bf16)


def _lru_kernel(x_ref, gate_ref, cw_ref, cb_ref, wg_ref, ba_ref, bx_ref, sp_ref, ng_ref,
                o_ref, xext_ref, h_ref):
    s = pl.program_id(1)
    tc = x_ref.shape[0]

    @pl.when(s == 0)
    def _():
        xext_ref[0:SUBLANES, :] = jnp.zeros((SUBLANES, D_LRU), F32)
        h_ref[...] = jnp.zeros_like(h_ref)

    xext_ref[SUBLANES:SUBLANES + tc, :] = x_ref[...].astype(F32)
    xc = cb_ref[...] + cw_ref[CONV_WIDTH - 1:CONV_WIDTH, :] * xext_ref[SUBLANES:SUBLANES + tc, :]
    for j in range(CONV_WIDTH - 1):
        off = SUBLANES - (CONV_WIDTH - 1) + j
        xc = xc + cw_ref[j:j + 1, :] * xext_ref[off:off + tc, :]
    xext_ref[0:SUBLANES, :] = xext_ref[tc:tc + SUBLANES, :]

    gates = jnp.dot(xc.astype(BF16), wg_ref[...], preferred_element_type=F32)
    r = jax.nn.sigmoid(gates[:, :D_LRU] + ba_ref[...])
    ig = jax.nn.sigmoid(gates[:, D_LRU:] + bx_ref[...])
    log_a = (-LRU_C) * r * sp_ref[...]
    a = jnp.exp(log_a)
    b = jnp.sqrt(1.0 - a * a) * (ig * xc)

    rows = lax.broadcasted_iota(jnp.int32, (tc, 1), 0)
    d = 1
    while d < tc:
        a_s = pltpu.roll(a, d, 0)
        b_s = pltpu.roll(b, d, 0)
        m = rows >= d
        b = jnp.where(m, a * b_s + b, b)
        a = jnp.where(m, a * a_s, a)
        d *= 2
    h = b + a * h_ref[0:1, :]
    h_ref[0:1, :] = h[tc - 1:tc, :]

    y = h * jax.nn.gelu(gate_ref[...].astype(F32))
    ms = jnp.mean(y * y, axis=-1, keepdims=True)
    o_ref[...] = (y * lax.rsqrt(ms + NORM_EPS) * ng_ref[...]).astype(o_ref.dtype)


def _lru_branch(proj, bsz, seq, conv_w, conv_b, wg_bd, ba, bx, softplus_neg_lam, norm_g):
    ns = seq // TC_SEQ
    row = lambda b, s: (b * ns + s, 0)
    const = lambda b, s: (0, 0)
    return pl.pallas_call(
        _lru_kernel,
        grid=(bsz, ns),
        in_specs=[
            pl.BlockSpec((TC_SEQ, D_LRU), lambda b, s: (b * ns + s, 0)),
            pl.BlockSpec((TC_SEQ, D_LRU), lambda b, s: (b * ns + s, 1)),
            pl.BlockSpec((CONV_WIDTH, D_LRU), const),
            pl.BlockSpec((1, D_LRU), const),
            pl.BlockSpec((D_LRU, 2 * D_LRU), const),
            pl.BlockSpec((1, D_LRU), const),
            pl.BlockSpec((1, D_LRU), const),
            pl.BlockSpec((1, D_LRU), const),
            pl.BlockSpec((1, D_LRU), const),
        ],
        out_specs=pl.BlockSpec((TC_SEQ, D_LRU), row),
        out_shape=jax.ShapeDtypeStruct((bsz * seq, D_LRU), BF16),
        scratch_shapes=[
            pltpu.VMEM((TC_SEQ + SUBLANES, D_LRU), F32),
            pltpu.VMEM((SUBLANES, D_LRU), F32),
        ],
        compiler_params=_cparams(("arbitrary", "arbitrary")),
        name="lru_branch",
    )(proj, proj, conv_w, conv_b, wg_bd, ba, bx, softplus_neg_lam, norm_g)


def _ret_kernel(pos_ref, invf_ref, sgn_ref, q_ref, k_ref, v_ref, g_ref, gg_ref, gb_ref,
                o_ref, state_ref, decay_ref):
    s = pl.program_id(1)
    tc = q_ref.shape[0]
    dh = RET_HEAD_DIM

    @pl.when(jnp.logical_and(pl.program_id(0) == 0, s == 0))
    def _():
        ri = lax.broadcasted_iota(jnp.int32, (tc, tc), 0)
        ci = lax.broadcasted_iota(jnp.int32, (tc, tc), 1)
        rel = (ri - ci).astype(F32)
        causal = rel >= 0.0
        relc = jnp.where(causal, rel, 0.0)
        for h in range(RET_HEADS):
            decay_ref[h] = jnp.where(causal, jnp.exp(LOG_GAMMA[h] * relc), 0.0)

    @pl.when(s == 0)
    def _():
        state_ref[...] = jnp.zeros_like(state_ref)

    ang = pos_ref[...].astype(F32) * invf_ref[...]
    cos = jnp.cos(ang)
    sin_signed = jnp.sin(ang) * sgn_ref[...]
    rowf = lax.broadcasted_iota(jnp.int32, (tc, 1), 0).astype(F32)
    scale = dh ** -0.5

    for h in range(RET_HEADS):
        sl = slice(h * dh, (h + 1) * dh)
        q = q_ref[:, sl].astype(F32)
        k = k_ref[:, sl].astype(F32)
        v = v_ref[:, sl]
        qr = q * cos + pltpu.roll(q, dh // 2, 1) * sin_signed
        kr = (k * cos + pltpu.roll(k, dh // 2, 1) * sin_signed) * scale
        qb = qr.astype(BF16)
        kb = kr.astype(BF16)
        lg = LOG_GAMMA[h]
        scores = lax.dot_general(qb, kb, (((1,), (1,)), ((), ())),
                                 preferred_element_type=F32) * decay_ref[h]
        intra = jnp.dot(scores.astype(BF16), v, preferred_element_type=F32)
        st = state_ref[h]
        xi = jnp.exp(lg * (rowf + 1.0))
        cross = jnp.dot(qb, st.astype(BF16), preferred_element_type=F32) * xi
        o = intra + cross
        zeta = jnp.exp(lg * (float(tc) - 1.0 - rowf))
        kz = (kr * zeta).astype(BF16)
        kv = lax.dot_general(kz, v, (((0,), (0,)), ((), ())), preferred_element_type=F32)
        state_ref[h] = math.exp(lg * tc) * st + kv

        mu = jnp.mean(o, axis=-1, keepdims=True)
        oc = o - mu
        var = jnp.mean(oc * oc, axis=-1, keepdims=True)
        on = oc * lax.rsqrt(var + NORM_EPS) * gg_ref[:, sl] + gb_ref[:, sl]
        gate = g_ref[:, sl].astype(F32)
        o_ref[:, sl] = (gate * jax.nn.sigmoid(gate) * on).astype(o_ref.dtype)


def _ret_branch(proj, pos_col, bsz, seq, inv_freq, sign, gn_g, gn_b):
    ns = seq // TC_SEQ
    const = lambda b, s: (0, 0)
    col = lambda c: (lambda b, s: (b * ns + s, c))
    return pl.pallas_call(
        _ret_kernel,
        grid=(bsz, ns),
        in_specs=[
            pl.BlockSpec((TC_SEQ, 1), lambda b, s: (b * ns + s, 0)),
            pl.BlockSpec((1, RET_HEAD_DIM), const),
            pl.BlockSpec((1, RET_HEAD_DIM), const),
            pl.BlockSpec((TC_SEQ, D_RET), col(2)),
            pl.BlockSpec((TC_SEQ, D_RET), col(3)),
            pl.BlockSpec((TC_SEQ, D_RET), col(4)),
            pl.BlockSpec((TC_SEQ, D_RET), col(5)),
            pl.BlockSpec((1, D_RET), const),
            pl.BlockSpec((1, D_RET), const),
        ],
        out_specs=pl.BlockSpec((TC_SEQ, D_RET), lambda b, s: (b * ns + s, 0)),
        out_shape=jax.ShapeDtypeStruct((bsz * seq, D_RET), BF16),
        scratch_shapes=[
            pltpu.VMEM((RET_HEADS, RET_HEAD_DIM, RET_HEAD_DIM), F32),
            pltpu.VMEM((RET_HEADS, TC_SEQ, TC_SEQ), F32),
        ],
        compiler_params=_cparams(("arbitrary", "arbitrary")),
        name="ret_branch",
    )(pos_col, inv_freq, sign, proj, proj, proj, proj, gn_g, gn_b)


def _outproj_router_kernel(x_ref, yl_ref, yr_ref, wo_ref, ng_ref, rw_ref, rb_ref, tri_ref,
                           x1_ref, hf_ref, meta_ref, cnt_ref):
    tm = x_ref.shape[0]

    y = jnp.dot(yl_ref[...], wo_ref[0:D_LRU, :], preferred_element_type=F32)
    y = y + jnp.dot(yr_ref[...], wo_ref[D_LRU:, :], preferred_element_type=F32)
    x1 = x_ref[...] + y
    x1_ref[...] = x1
    ms = jnp.mean(x1 * x1, axis=-1, keepdims=True)
    hf = x1 * lax.rsqrt(ms + NORM_EPS) * ng_ref[...]
    hf_ref[...] = hf.astype(hf_ref.dtype)

    logits = jnp.dot(hf.astype(BF16), rw_ref[...], preferred_element_type=F32) + rb_ref[...]
    lane = lax.broadcasted_iota(jnp.int32, (tm, LANES), 1)
    lane_f = lane.astype(F32)
    work = logits
    vals, idxs = [], []
    onehot = jnp.zeros((tm, LANES), F32)
    for _ in range(TOP_K):
        m = jnp.max(work, axis=-1, keepdims=True)
        idx = jnp.min(jnp.where(work == m, lane_f, float(LANES)), axis=-1, keepdims=True)
        sel = lane_f == idx
        work = jnp.where(sel, -jnp.inf, work)
        onehot = jnp.where(sel, 1.0, onehot)
        vals.append(m)
        idxs.append(idx)
    exps = [jnp.exp(v - vals[0]) for v in vals]
    denom = exps[0] + exps[1] + exps[2] + exps[3]
    gates = [e / denom for e in exps]

    rank_mat = jnp.dot(tri_ref[...], onehot.astype(BF16), preferred_element_type=F32)
    meta = jnp.zeros((tm, LANES), F32)
    for kk in range(TOP_K):
        rk = jnp.sum(jnp.where(lane_f == idxs[kk], rank_mat, 0.0), axis=-1, keepdims=True)
        meta = jnp.where(lane == kk, idxs[kk], meta)
        meta = jnp.where(lane == TOP_K + kk, rk, meta)
        meta = jnp.where(lane == 2 * TOP_K + kk, gates[kk], meta)
    meta_ref[...] = meta
    cnt_ref[...] = jnp.broadcast_to(jnp.sum(onehot, axis=0, keepdims=True), cnt_ref.shape)


def _outproj_router(x2, y_lru, y_ret, w_out_bf16, ng, rw_pad, rb_pad, tri):
    t = x2.shape[0]
    tm = TM_PROJ
    const = lambda i: (0, 0)
    row = lambda i: (i, 0)
    return pl.pallas_call(
        _outproj_router_kernel,
        grid=(t // tm,),
        in_specs=[
            pl.BlockSpec((tm, D_MODEL), row),
            pl.BlockSpec((tm, D_LRU), row),
            pl.BlockSpec((tm, D_RET), row),
            pl.BlockSpec((D_MODEL, D_MODEL), const),
            pl.BlockSpec((1, D_MODEL), const),
            pl.BlockSpec((D_MODEL, LANES), const),
            pl.BlockSpec((1, LANES), const),
            pl.BlockSpec((tm, tm), const),
        ],
        out_specs=[
            pl.BlockSpec((tm, D_MODEL), row),
            pl.BlockSpec((tm, D_MODEL), row),
            pl.BlockSpec((tm, LANES), row),
            pl.BlockSpec((None, SUBLANES, LANES), lambda i: (i, 0, 0)),
        ],
        out_shape=[
            jax.ShapeDtypeStruct((t, D_MODEL), F32),
            jax.ShapeDtypeStruct((t, D_MODEL), BF16),
            jax.ShapeDtypeStruct((t, LANES), F32),
            jax.ShapeDtypeStruct((t // tm, SUBLANES, LANES), F32),
        ],
        compiler_params=_cparams(("arbitrary",)),
        name="outproj_router",
    )(x2, y_lru, y_ret, w_out_bf16, ng, rw_pad, rb_pad, tri)


def _local_positions(meta, lo8_row):
    lane_f = lax.broadcasted_iota(jnp.int32, meta.shape, 1).astype(F32)
    out = []
    for kk in range(TOP_K):
        idx = meta[:, kk:kk + 1]
        rank = meta[:, TOP_K + kk:TOP_K + kk + 1]
        lo = jnp.sum(jnp.where(lane_f == idx, lo8_row, 0.0), axis=-1, keepdims=True)
        out.append(lo + rank)
    return out


def _sort_kernel(nseg_ref, lo8_ref, goff_ref, nch_ref, pend_ref,
                 hf_ref, meta_ref, lo8v_ref, xs_hbm, sorted_ref, sems, *, min_blocks):
    i = pl.program_id(0)
    nt = pl.num_programs(0)
    b = i % 2
    tm = hf_ref.shape[0]
    nb = xs_hbm.shape[0] // BM

    def seg_copy(buf, lo, go):
        return pltpu.make_async_copy(
            sorted_ref.at[buf, pl.ds(pl.multiple_of(lo, SEG), SEG)],
            xs_hbm.at[pl.ds(pl.multiple_of(go, SEG), SEG)], sems.at[buf])

    def wait_tile(tile, buf):
        def w(j, c):
            seg_copy(buf, 0, 0).wait()
            return c
        lax.fori_loop(0, nch_ref[tile], w, 0)

    @pl.when(i == 0)
    def _():
        sorted_ref[1, 0:BM, :] = jnp.zeros((BM, D_MODEL), F32)

        def zcopy(start):
            return pltpu.make_async_copy(
                sorted_ref.at[1, pl.ds(0, BM)],
                xs_hbm.at[pl.ds(pl.multiple_of(start, BM), BM)], sems.at[1])

        for e in range(N_EXPERTS):
            zcopy(jnp.maximum(pend_ref[e] - BM, 0)).start()
        for e in range(N_EXPERTS):
            zcopy(jnp.maximum(pend_ref[e] - BM, 0)).wait()
        for blk in range(min_blocks, nb):
            @pl.when(blk * BM >= pend_ref[N_EXPERTS - 1])
            def _():
                zcopy(blk * BM).start()
                zcopy(blk * BM).wait()

    @pl.when(i >= 2)
    def _():
        wait_tile(i - 2, b)

    lpos = _local_positions(meta_ref[...], lo8v_ref[...])
    hfb = hf_ref[...]
    for j in range(RL // PCOLS):
        r = (lax.broadcasted_iota(jnp.int32, (tm, PCOLS), 1) + j * PCOLS).astype(F32)
        hit = (r == lpos[0]) | (r == lpos[1]) | (r == lpos[2]) | (r == lpos[3])
        pt = jnp.where(hit, 1.0, 0.0).astype(BF16)
        sorted_ref[b, j * PCOLS:(j + 1) * PCOLS, :] = lax.dot_general(
            pt, hfb, (((0,), (0,)), ((), ())), preferred_element_type=F32)

    for e in range(N_EXPERTS):
        q = i * N_EXPERTS + e
        lo = lo8_ref[q]
        go = goff_ref[q]

        def issue(j, c, lo=lo, go=go):
            seg_copy(b, lo + j * SEG, go + j * SEG).start()
            return c

        lax.fori_loop(0, nseg_ref[q], issue, 0)

    @pl.when(i == nt - 1)
    def _():
        @pl.when(i >= 1)
        def _():
            wait_tile(i - 1, 1 - b)
        wait_tile(i, b)


def _sort_dispatch(nseg, lo8, goff, nch, pend, hf, meta, lo8v, n_pad_rows):
    t = hf.shape[0]
    tm = TM_PROJ
    row = lambda i, *_: (i, 0)
    return pl.pallas_call(
        functools.partial(_sort_kernel, min_blocks=t * TOP_K // BM),
        grid_spec=pltpu.PrefetchScalarGridSpec(
            num_scalar_prefetch=5,
            grid=(t // tm,),
            in_specs=[
                pl.BlockSpec((tm, D_MODEL), row),
                pl.BlockSpec((tm, LANES), row),
                pl.BlockSpec((None, 1, LANES), lambda i, *_: (i, 0, 0)),
            ],
            out_specs=pl.BlockSpec(memory_space=pl.ANY),
            scratch_shapes=[pltpu.VMEM((2, RL, D_MODEL), F32), pltpu.SemaphoreType.DMA((2,))],
        ),
        out_shape=jax.ShapeDtypeStruct((n_pad_rows, D_MODEL), F32),
        compiler_params=_cparams(("arbitrary",)),
        name="sort_dispatch",
    )(nseg, lo8, goff, nch, pend, hf, meta, lo8v)


def _ffn_kernel(be_ref, nv_ref, x_ref, wg_ref, bg_ref, wu_ref, bu_ref, wd_ref, bd_ref, o_ref,
                wbf_ref):
    i = pl.program_id(0)

    @pl.when(i < nv_ref[0])
    def _():
        @pl.when(jnp.logical_or(i == 0, be_ref[i] != be_ref[jnp.maximum(i - 1, 0)]))
        def _():
            wbf_ref[0] = wg_ref[...].astype(BF16)
            wbf_ref[1] = wu_ref[...].astype(BF16)
            wbf_ref[2] = wd_ref[...].astype(BF16)

        x = x_ref[...].astype(BF16)
        y = jnp.broadcast_to(bd_ref[...], o_ref.shape)
        for c in range(D_MODEL // FFN_COLS):
            cs = slice(c * FFN_COLS, (c + 1) * FFN_COLS)
            g = jnp.dot(x, wbf_ref[0, :, cs], preferred_element_type=F32) + bg_ref[:, cs]
            g = jnp.minimum(g, SWIGLU_LIMIT)
            u = jnp.dot(x, wbf_ref[1, :, cs], preferred_element_type=F32) + bu_ref[:, cs]
            u = jnp.clip(u, -SWIGLU_LIMIT, SWIGLU_LIMIT)
            act = g * jax.nn.sigmoid(SWIGLU_ALPHA * g) * (u + 1.0)
            y = y + jnp.dot(act.astype(BF16), wbf_ref[2, cs, :], preferred_element_type=F32)
        o_ref[...] = y

    @pl.when(pl.program_id(0) >= nv_ref[0])
    def _():
        o_ref[...] = jnp.zeros_like(o_ref)


def _expert_ffn(block_e, nvalid, xs, wg, bg, wu, bu, wd, bd):
    n_rows = xs.shape[0]
    nb = n_rows // BM

    def blk(i, be, nv):
        return jnp.minimum(i, nv[0] - 1)

    xmap = lambda i, be, nv: (blk(i, be, nv), 0)
    wmap = lambda i, be, nv: (be[blk(i, be, nv)], 0, 0)
    return pl.pallas_call(
        _ffn_kernel,
        grid_spec=pltpu.PrefetchScalarGridSpec(
            num_scalar_prefetch=2,
            grid=(nb,),
            in_specs=[
                pl.BlockSpec((BM, D_MODEL), xmap),
                pl.BlockSpec((None, D_MODEL, D_MODEL), wmap),
                pl.BlockSpec((None, 1, D_MODEL), wmap),
                pl.BlockSpec((None, D_MODEL, D_MODEL), wmap),
                pl.BlockSpec((None, 1, D_MODEL), wmap),
                pl.BlockSpec((None, D_MODEL, D_MODEL), wmap),
                pl.BlockSpec((None, 1, D_MODEL), wmap),
            ],
            out_specs=pl.BlockSpec((BM, D_MODEL), lambda i, be, nv: (i, 0)),
            scratch_shapes=[pltpu.VMEM((3, D_MODEL, D_MODEL), BF16)],
        ),
        out_shape=jax.ShapeDtypeStruct((n_rows, D_MODEL), F32),
        compiler_params=pltpu.CompilerParams(dimension_semantics=("arbitrary",),
                                             vmem_limit_bytes=FFN_VMEM_LIMIT),
        name="expert_ffn",
    )(block_e, nvalid, xs, wg, bg, wu, bu, wd, bd)


def _combine_kernel(nseg_ref, lo8_ref, goff_ref, nch_ref,
                    x1_ref, meta_ref, lo8v_ref, g_ref, yb_hbm, o_ref, ybl_ref, sems):
    i = pl.program_id(0)
    nt = pl.num_programs(0)
    b = i % 2
    tm = x1_ref.shape[0]

    def seg_copy(buf, lo, go):
        return pltpu.make_async_copy(
            yb_hbm.at[pl.ds(pl.multiple_of(go, SEG), SEG)],
            ybl_ref.at[buf, pl.ds(pl.multiple_of(lo, SEG), SEG)], sems.at[buf])

    def issue_tile(tile, buf):
        for e in range(N_EXPERTS):
            q = tile * N_EXPERTS + e
            lo = lo8_ref[q]
            go = goff_ref[q]

            def issue(j, c, lo=lo, go=go):
                seg_copy(buf, lo + j * SEG, go + j * SEG).start()
                return c

            lax.fori_loop(0, nseg_ref[q], issue, 0)

    def wait_tile(tile, buf):
        def w(j, c):
            seg_copy(buf, 0, 0).wait()
            return c
        lax.fori_loop(0, nch_ref[tile], w, 0)

    @pl.when(i == 0)
    def _():
        ybl_ref[...] = jnp.zeros_like(ybl_ref)
        issue_tile(0, 0)

    @pl.when(i + 1 < nt)
    def _():
        issue_tile(i + 1, 1 - b)

    wait_tile(i, b)

    meta = meta_ref[...]
    lpos = _local_positions(meta, lo8v_ref[...])
    acc = x1_ref[...]
    for j in range(RL // PCOLS):
        r = (lax.broadcasted_iota(jnp.int32, (tm, PCOLS), 1) + j * PCOLS).astype(F32)
        gmat = jnp.zeros((tm, PCOLS), F32)
        for kk in range(TOP_K):
            gmat = jnp.where(r == lpos[kk], meta[:, 2 * TOP_K + kk:2 * TOP_K + kk + 1], gmat)
        rows = ybl_ref[b, j * PCOLS:(j + 1) * PCOLS, :].astype(BF16)
        acc = acc + jnp.dot(gmat.astype(BF16), rows, preferred_element_type=F32)
    ms = jnp.mean(acc * acc, axis=-1, keepdims=True)
    o_ref[...] = acc * lax.rsqrt(ms + NORM_EPS) * g_ref[...]


def _combine(nseg, lo8, goff, nch, x1, meta, lo8v, final_g, yb):
    t = x1.shape[0]
    tm = TM_PROJ
    row = lambda i, *_: (i, 0)
    return pl.pallas_call(
        _combine_kernel,
        grid_spec=pltpu.PrefetchScalarGridSpec(
            num_scalar_prefetch=4,
            grid=(t // tm,),
            in_specs=[
                pl.BlockSpec((tm, D_MODEL), row),
                pl.BlockSpec((tm, LANES), row),
                pl.BlockSpec((None, 1, LANES), lambda i, *_: (i, 0, 0)),
                pl.BlockSpec((1, D_MODEL), lambda i, *_: (0, 0)),
                pl.BlockSpec(memory_space=pl.ANY),
            ],
            out_specs=pl.BlockSpec((tm, D_MODEL), row),
            scratch_shapes=[pltpu.VMEM((2, RL, D_MODEL), F32), pltpu.SemaphoreType.DMA((2,))],
        ),
        out_shape=jax.ShapeDtypeStruct((t, D_MODEL), F32),
        compiler_params=_cparams(("arbitrary",)),
        name="combine",
    )(nseg, lo8, goff, nch, x1, meta, lo8v, final_g, yb)


def _block_diag(w):
    h, d, _ = w.shape
    eye = jnp.eye(h, dtype=w.dtype)
    return (eye[:, None, :, None] * w[:, :, None, :]).reshape(h * d, h * d)


def kernel(x, positions, attn_norm_g, w_in, conv_w, conv_b, lru_wa, lru_ba, lru_wx, lru_bx, lru_lambda, lru_norm_g, ret_norm_g, ret_norm_b, w_out, ffn_norm_g, router_w, router_b, moe_w_gate, moe_b_gate, moe_w_up, moe_b_up, moe_w_down, moe_b_down, final_norm_g):
    bsz, seq, d = x.shape
    depth = w_in.shape[0]
    t = bsz * seq
    assert depth == 1 and d == D_MODEL and seq % TC_SEQ == 0 and t % TM_PROJ == 0
    nt = t // TM_PROJ
    n_pad = t * TOP_K + nt * N_EXPERTS * SEG + N_EXPERTS * BM
    nb = n_pad // BM

    half = RET_HEAD_DIM // 2
    inv_freq = ROPE_THETA ** (-jnp.arange(half, dtype=F32) / half)
    inv_freq = jnp.concatenate([inv_freq, inv_freq]).reshape(1, RET_HEAD_DIM)
    sign = jnp.concatenate([-jnp.ones((half,), F32), jnp.ones((half,), F32)]).reshape(1, RET_HEAD_DIM)
    pos_col = positions.reshape(t, 1).astype(jnp.int32)
    tri = jnp.tril(jnp.ones((TM_PROJ, TM_PROJ), F32), -1).astype(BF16)

    x2 = x.reshape(t, d)
    for l in range(depth):
        proj = _inproj(x2, attn_norm_g[l].reshape(1, d), w_in[l].astype(BF16))

        wg_bd = jnp.concatenate([_block_diag(lru_wa[l]), _block_diag(lru_wx[l])], axis=1).astype(BF16)
        lam = lru_lambda[l].astype(F32)
        sp = (jnp.maximum(-lam, 0.0) + jnp.log1p(jnp.exp(-jnp.abs(lam)))).reshape(1, D_LRU)
        y_lru = _lru_branch(proj, bsz, seq, conv_w[l], conv_b[l].reshape(1, D_LRU), wg_bd,
                            lru_ba[l].reshape(1, D_LRU), lru_bx[l].reshape(1, D_LRU), sp,
                            lru_norm_g[l].reshape(1, D_LRU))
        y_ret = _ret_branch(proj, pos_col, bsz, seq, inv_freq, sign,
                            ret_norm_g[l].reshape(1, D_RET), ret_norm_b[l].reshape(1, D_RET))

        rw_pad = jnp.zeros((d, LANES), F32).at[:, :N_EXPERTS].set(router_w[l]).astype(BF16)
        rb_pad = jnp.full((1, LANES), -1e30, F32).at[0, :N_EXPERTS].set(router_b[l])
        x1, hf, meta, cnt_tile = _outproj_router(x2, y_lru, y_ret, w_out[l].astype(BF16),
                                            ffn_norm_g[l].reshape(1, d), rw_pad, rb_pad, tri)

        cnt = cnt_tile[:, 0, :N_EXPERTS].astype(jnp.int32)
        c8 = (cnt + SEG - 1) // SEG * SEG
        lo8 = jnp.cumsum(c8, axis=1) - c8
        padded = (jnp.sum(c8, axis=0) + BM - 1) // BM * BM
        pend = jnp.cumsum(padded).astype(jnp.int32)
        pstart = pend - padded
        goff = (pstart[None, :] + jnp.cumsum(c8, axis=0) - c8).astype(jnp.int32)
        nseg = (c8 // SEG).astype(jnp.int32)
        nch = jnp.sum(nseg, axis=1).astype(jnp.int32)
        lo8v = jnp.zeros((nt, 1, LANES), F32).at[:, 0, :N_EXPERTS].set(lo8.astype(F32))
        block_start = jnp.arange(nb, dtype=jnp.int32) * BM
        block_e = jnp.minimum(jnp.sum(block_start[:, None] >= pend[None, :], axis=1),
                              N_EXPERTS - 1).astype(jnp.int32)
        nvalid = (pend[N_EXPERTS - 1:] // BM).astype(jnp.int32)
        nseg, lo8, goff = nseg.reshape(-1), lo8.astype(jnp.int32).reshape(-1), goff.reshape(-1)

        xs = _sort_dispatch(nseg, lo8, goff, nch, pend, hf, meta, lo8v, n_pad)
        yb = _expert_ffn(block_e, nvalid, xs,
                         moe_w_gate[l], moe_b_gate[l].reshape(N_EXPERTS, 1, d),
                         moe_w_up[l], moe_b_up[l].reshape(N_EXPERTS, 1, d),
                         moe_w_down[l], moe_b_down[l].reshape(N_EXPERTS, 1, d))
        x2 = _combine(nseg, lo8, goff, nch, x1, meta, lo8v, final_norm_g.reshape(1, d), yb)
    return x2.reshape(bsz, seq, d)
```

```python
import functools
import math

import numpy as np
import jax
import jax.numpy as jnp
from jax import lax
from jax.experimental import pallas as pl
from jax.experimental.pallas import tpu as pltpu

F32 = jnp.float32
BF16 = jnp.bfloat16

D_MODEL = 1024
D_LRU = 512
D_RET = 512
LRU_HEADS = 8
LRU_HEAD_DIM = D_LRU // LRU_HEADS
CONV_WIDTH = 4
LRU_C = 8.0
RET_HEADS = 4
RET_HEAD_DIM = D_RET // RET_HEADS
ROPE_THETA = 10000.0
D_IN = 2 * D_LRU + 4 * D_RET
N_EXPERTS = 32
TOP_K = 4
SWIGLU_ALPHA = 1.702
SWIGLU_LIMIT = 7.0
NORM_EPS = 1e-5

LANES = 128
SUBLANES = 8
VMEM_LIMIT = 48 * 1024 * 1024

TM_PROJ = 512
TC_SEQ = 256
BM = 512
FFN_COLS = 512
FFN_VMEM_LIMIT = 56 * 1024 * 1024
SEG = SUBLANES
RL = TM_PROJ * TOP_K + N_EXPERTS * SEG
PCOLS = 256
PROWS = 768
BIG_SHIFT = 2
BIG = SEG << BIG_SHIFT
WAIT_SHIFT = 5
WAITBIG = SEG << WAIT_SHIFT

LOG_GAMMA = [math.log1p(-(2.0 ** (-5.0 - h))) for h in range(RET_HEADS)]


def _cparams(sem):
    return pltpu.CompilerParams(dimension_semantics=sem, vmem_limit_bytes=VMEM_LIMIT)


def _inproj_kernel(x_ref, g_ref, w_ref, o_ref):
    x = x_ref[...]
    ms = jnp.mean(x * x, axis=-1, keepdims=True)
    h = x * lax.rsqrt(ms + NORM_EPS) * g_ref[...]
    o_ref[...] = jnp.dot(h.astype(BF16), w_ref[...],
                         preferred_element_type=F32).astype(o_ref.dtype)


def _inproj(x2, g, w_in_bf16):
    t = x2.shape[0]
    return pl.pallas_call(
        _inproj_kernel,
        grid=(t // TM_PROJ,),
        in_specs=[
            pl.BlockSpec((TM_PROJ, D_MODEL), lambda i: (i, 0)),
            pl.BlockSpec((1, D_MODEL), lambda i: (0, 0)),
            pl.BlockSpec((D_MODEL, D_IN), lambda i: (0, 0)),
        ],
        out_specs=pl.BlockSpec((TM_PROJ, D_IN), lambda i: (i, 0)),
        out_shape=jax.ShapeDtypeStruct((t, D_IN), BF16),
        compiler_params=_cparams(("arbitrary",)),
        name="inproj",
    )(x2, g, w_in_bf16)


def _lru_kernel(x_ref, gate_ref, cw_ref, cb_ref, wg_ref, ba_ref, bx_ref, sp_ref, ng_ref,
                o_ref, xext_ref, h_ref):
    s = pl.program_id(1)
    tc = x_ref.shape[0]

    @pl.when(s == 0)
    def _():
        xext_ref[0:SUBLANES, :] = jnp.zeros((SUBLANES, D_LRU), F32)
        h_ref[...] = jnp.zeros_like(h_ref)

    xext_ref[SUBLANES:SUBLANES + tc, :] = x_ref[...].astype(F32)
    xc = cb_ref[...] + cw_ref[CONV_WIDTH - 1:CONV_WIDTH, :] * xext_ref[SUBLANES:SUBLANES + tc, :]
    for j in range(CONV_WIDTH - 1):
        off = SUBLANES - (CONV_WIDTH - 1) + j
        xc = xc + cw_ref[j:j + 1, :] * xext_ref[off:off + tc, :]
    xext_ref[0:SUBLANES, :] = xext_ref[tc:tc + SUBLANES, :]

    gates = jnp.dot(xc.astype(BF16), wg_ref[...], preferred_element_type=F32)
    r = jax.nn.sigmoid(gates[:, :D_LRU] + ba_ref[...])
    ig = jax.nn.sigmoid(gates[:, D_LRU:] + bx_ref[...])
    log_a = (-LRU_C) * r * sp_ref[...]
    a = jnp.exp(log_a)
    b = jnp.sqrt(1.0 - a * a) * (ig * xc)

    rows = lax.broadcasted_iota(jnp.int32, (tc, 1), 0)
    d = 1
    while d < tc:
        a_s = pltpu.roll(a, d, 0)
        b_s = pltpu.roll(b, d, 0)
        m = rows >= d
        b = jnp.where(m, a * b_s + b, b)
        a = jnp.where(m, a * a_s, a)
        d *= 2
    h = b + a * h_ref[0:1, :]
    h_ref[0:1, :] = h[tc - 1:tc, :]

    y = h * jax.nn.gelu(gate_ref[...].astype(F32))
    ms = jnp.mean(y * y, axis=-1, keepdims=True)
    o_ref[...] = (y * lax.rsqrt(ms + NORM_EPS) * ng_ref[...]).astype(o_ref.dtype)


def _lru_branch(proj, bsz, seq, conv_w, conv_b, wg_bd, ba, bx, softplus_neg_lam, norm_g):
    ns = seq // TC_SEQ
    row = lambda b, s: (b * ns + s, 0)
    const = lambda b, s: (0, 0)
    return pl.pallas_call(
        _lru_kernel,
        grid=(bsz, ns),
        in_specs=[
            pl.BlockSpec((TC_SEQ, D_LRU), lambda b, s: (b * ns + s, 0)),
            pl.BlockSpec((TC_SEQ, D_LRU), lambda b, s: (b * ns + s, 1)),
            pl.BlockSpec((CONV_WIDTH, D_LRU), const),
            pl.BlockSpec((1, D_LRU), const),
            pl.BlockSpec((D_LRU, 2 * D_LRU), const),
            pl.BlockSpec((1, D_LRU), const),
            pl.BlockSpec((1, D_LRU), const),
            pl.BlockSpec((1, D_LRU), const),
            pl.BlockSpec((1, D_LRU), const),
        ],
        out_specs=pl.BlockSpec((TC_SEQ, D_LRU), row),
        out_shape=jax.ShapeDtypeStruct((bsz * seq, D_LRU), BF16),
        scratch_shapes=[
            pltpu.VMEM((TC_SEQ + SUBLANES, D_LRU), F32),
            pltpu.VMEM((SUBLANES, D_LRU), F32),
        ],
        compiler_params=_cparams(("arbitrary", "arbitrary")),
        name="lru_branch",
    )(proj, proj, conv_w, conv_b, wg_bd, ba, bx, softplus_neg_lam, norm_g)


def _ret_kernel(pos_ref, invf_ref, sgn_ref, q_ref, k_ref, v_ref, g_ref, gg_ref, gb_ref,
                o_ref, state_ref, decay_ref):
    s = pl.program_id(1)
    tc = q_ref.shape[0]
    dh = RET_HEAD_DIM

    @pl.when(jnp.logical_and(pl.program_id(0) == 0, s == 0))
    def _():
        ri = lax.broadcasted_iota(jnp.int32, (tc, tc), 0)
        ci = lax.broadcasted_iota(jnp.int32, (tc, tc), 1)
        rel = (ri - ci).astype(F32)
        causal = rel >= 0.0
        relc = jnp.where(causal, rel, 0.0)
        for h in range(RET_HEADS):
            decay_ref[h] = jnp.where(causal, jnp.exp(LOG_GAMMA[h] * relc), 0.0)

    @pl.when(s == 0)
    def _():
        state_ref[...] = jnp.zeros_like(state_ref)

    ang = pos_ref[...].astype(F32) * invf_ref[...]
    cos = jnp.cos(ang)
    sin_signed = jnp.sin(ang) * sgn_ref[...]
    rowf = lax.broadcasted_iota(jnp.int32, (tc, 1), 0).astype(F32)
    scale = dh ** -0.5

    for h in range(RET_HEADS):
        sl = slice(h * dh, (h + 1) * dh)
        q = q_ref[:, sl].astype(F32)
        k = k_ref[:, sl].astype(F32)
        v = v_ref[:, sl]
        qr = q * cos + pltpu.roll(q, dh // 2, 1) * sin_signed
        kr = (k * cos + pltpu.roll(k, dh // 2, 1) * sin_signed) * scale
        qb = qr.astype(BF16)
        kb = kr.astype(BF16)
        lg = LOG_GAMMA[h]
        scores = lax.dot_general(qb, kb, (((1,), (1,)), ((), ())),
                                 preferred_element_type=F32) * decay_ref[h]
        intra = jnp.dot(scores.astype(BF16), v, preferred_element_type=F32)
        st = state_ref[h]
        xi = jnp.exp(lg * (rowf + 1.0))
        cross = jnp.dot(qb, st.astype(BF16), preferred_element_type=F32) * xi
        o = intra + cross
        zeta = jnp.exp(lg * (float(tc) - 1.0 - rowf))
        kz = (kr * zeta).astype(BF16)
        kv = lax.dot_general(kz, v, (((0,), (0,)), ((), ())), preferred_element_type=F32)
        state_ref[h] = math.exp(lg * tc) * st + kv

        mu = jnp.mean(o, axis=-1, keepdims=True)
        oc = o - mu
        var = jnp.mean(oc * oc, axis=-1, keepdims=True)
        on = oc * lax.rsqrt(var + NORM_EPS) * gg_ref[:, sl] + gb_ref[:, sl]
        gate = g_ref[:, sl].astype(F32)
        o_ref[:, sl] = (gate * jax.nn.sigmoid(gate) * on).astype(o_ref.dtype)


def _ret_branch(proj, pos_col, bsz, seq, inv_freq, sign, gn_g, gn_b):
    ns = seq // TC_SEQ
    const = lambda b, s: (0, 0)
    col = lambda c: (lambda b, s: (b * ns + s, c))
    return pl.pallas_call(
        _ret_kernel,
        grid=(bsz, ns),
        in_specs=[
            pl.BlockSpec((TC_SEQ, 1), lambda b, s: (b * ns + s, 0)),
            pl.BlockSpec((1, RET_HEAD_DIM), const),
            pl.BlockSpec((1, RET_HEAD_DIM), const),
            pl.BlockSpec((TC_SEQ, D_RET), col(2)),
            pl.BlockSpec((TC_SEQ, D_RET), col(3)),
            pl.BlockSpec((TC_SEQ, D_RET), col(4)),
            pl.BlockSpec((TC_SEQ, D_RET), col(5)),
            pl.BlockSpec((1, D_RET), const),
            pl.BlockSpec((1, D_RET), const),
        ],
        out_specs=pl.BlockSpec((TC_SEQ, D_RET), lambda b, s: (b * ns + s, 0)),
        out_shape=jax.ShapeDtypeStruct((bsz * seq, D_RET), BF16),
        scratch_shapes=[
            pltpu.VMEM((RET_HEADS, RET_HEAD_DIM, RET_HEAD_DIM), F32),
            pltpu.VMEM((RET_HEADS, TC_SEQ, TC_SEQ), F32),
        ],
        compiler_params=_cparams(("arbitrary", "arbitrary")),
        name="ret_branch",
    )(pos_col, inv_freq, sign, proj, proj, proj, proj, gn_g, gn_b)


def _outproj_router_kernel(x_ref, yl_ref, yr_ref, wo_ref, ng_ref, rw_ref, rb_ref, tri_ref,
                           x1_ref, hf_ref, meta_ref, cnt_ref):
    tm = x_ref.shape[0]

    y = jnp.dot(yl_ref[...], wo_ref[0:D_LRU, :], preferred_element_type=F32)
    y = y + jnp.dot(yr_ref[...], wo_ref[D_LRU:, :], preferred_element_type=F32)
    x1 = x_ref[...] + y
    x1_ref[...] = x1
    ms = jnp.mean(x1 * x1, axis=-1, keepdims=True)
    hf = x1 * lax.rsqrt(ms + NORM_EPS) * ng_ref[...]
    hf_ref[...] = hf.astype(hf_ref.dtype)

    logits = jnp.dot(hf.astype(BF16), rw_ref[...], preferred_element_type=F32) + rb_ref[...]
    lane = lax.broadcasted_iota(jnp.int32, (tm, LANES), 1)
    lane_f = lane.astype(F32)
    work = logits
    vals, idxs = [], []
    onehot = jnp.zeros((tm, LANES), F32)
    for _ in range(TOP_K):
        m = jnp.max(work, axis=-1, keepdims=True)
        idx = jnp.min(jnp.where(work == m, lane_f, float(LANES)), axis=-1, keepdims=True)
        sel = lane_f == idx
        work = jnp.where(sel, -jnp.inf, work)
        onehot = jnp.where(sel, 1.0, onehot)
        vals.append(m)
        idxs.append(idx)
    exps = [jnp.exp(v - vals[0]) for v in vals]
    denom = exps[0] + exps[1] + exps[2] + exps[3]
    gates = [e / denom for e in exps]

    rank_mat = jnp.dot(tri_ref[...], onehot.astype(BF16), preferred_element_type=F32)
    meta = jnp.zeros((tm, LANES), F32)
    for kk in range(TOP_K):
        rk = jnp.sum(jnp.where(lane_f == idxs[kk], rank_mat, 0.0), axis=-1, keepdims=True)
        meta = jnp.where(lane == kk, idxs[kk], meta)
        meta = jnp.where(lane == TOP_K + kk, rk, meta)
        meta = jnp.where(lane == 2 * TOP_K + kk, gates[kk], meta)
    meta_ref[...] = meta
    cnt_ref[...] = jnp.broadcast_to(jnp.sum(onehot, axis=0, keepdims=True), cnt_ref.shape)


def _outproj_router(x2, y_lru, y_ret, w_out_bf16, ng, rw_pad, rb_pad, tri):
    t = x2.shape[0]
    tm = TM_PROJ
    const = lambda i: (0, 0)
    row = lambda i: (i, 0)
    return pl.pallas_call(
        _outproj_router_kernel,
        grid=(t // tm,),
        in_specs=[
            pl.BlockSpec((tm, D_MODEL), row),
            pl.BlockSpec((tm, D_LRU), row),
            pl.BlockSpec((tm, D_RET), row),
            pl.BlockSpec((D_MODEL, D_MODEL), const),
            pl.BlockSpec((1, D_MODEL), const),
            pl.BlockSpec((D_MODEL, LANES), const),
            pl.BlockSpec((1, LANES), const),
            pl.BlockSpec((tm, tm), const),
        ],
        out_specs=[
            pl.BlockSpec((tm, D_MODEL), row),
            pl.BlockSpec((tm, D_MODEL), row),
            pl.BlockSpec((tm, LANES), row),
            pl.BlockSpec((None, SUBLANES, LANES), lambda i: (i, 0, 0)),
        ],
        out_shape=[
            jax.ShapeDtypeStruct((t, D_MODEL), F32),
            jax.ShapeDtypeStruct((t, D_MODEL), BF16),
            jax.ShapeDtypeStruct((t, LANES), F32),
            jax.ShapeDtypeStruct((t // tm, SUBLANES, LANES), F32),
        ],
        compiler_params=_cparams(("arbitrary",)),
        name="outproj_router",
    )(x2, y_lru, y_ret, w_out_bf16, ng, rw_pad, rb_pad, tri)


def _local_positions(meta, lo8_row):
    lane_f = lax.broadcasted_iota(jnp.int32, meta.shape, 1).astype(F32)
    out = []
    for kk in range(TOP_K):
        idx = meta[:, kk:kk + 1]
        rank = meta[:, TOP_K + kk:TOP_K + kk + 1]
        lo = jnp.sum(jnp.where(lane_f == idx, lo8_row, 0.0), axis=-1, keepdims=True)
        out.append(lo + rank)
    return out


def _issue_rows(copy, lo, go, nseg):
    nbig = lax.shift_right_logical(nseg, BIG_SHIFT)

    def big(j, c):
        copy(lo + j * BIG, go + j * BIG, BIG).start()
        return c

    lax.fori_loop(0, nbig, big, 0)
    done = nbig * BIG

    def small(j, c):
        copy(lo + done + j * SEG, go + done + j * SEG, SEG).start()
        return c

    lax.fori_loop(0, nseg - lax.shift_left(nbig, BIG_SHIFT), small, 0)


def _wait_rows(copy, nseg):
    nbig = lax.shift_right_logical(nseg, WAIT_SHIFT)

    def big(j, c):
        copy(WAITBIG).wait()
        return c

    lax.fori_loop(0, nbig, big, 0)

    def small(j, c):
        copy(SEG).wait()
        return c

    lax.fori_loop(0, nseg - lax.shift_left(nbig, WAIT_SHIFT), small, 0)


def _sort_kernel(nseg_ref, lo8_ref, goff_ref, nch_ref, pend_ref,
                 hf_ref, meta_ref, lo8v_ref, xs_hbm, sorted_ref, sems, *, min_blocks):
    i = pl.program_id(0)
    nt = pl.num_programs(0)
    b = i % 2
    tm = hf_ref.shape[0]
    nb = xs_hbm.shape[0] // BM

    def seg_copy(buf, lo, go, rows):
        return pltpu.make_async_copy(
            sorted_ref.at[buf, pl.ds(pl.multiple_of(lo, SEG), rows)],
            xs_hbm.at[pl.ds(pl.multiple_of(go, SEG), rows)], sems.at[buf])

    def wait_tile(tile, buf):
        _wait_rows(functools.partial(seg_copy, buf, 0, 0), nch_ref[tile])

    @pl.when(i == 0)
    def _():
        sorted_ref[1, 0:BM, :] = jnp.zeros((BM, D_MODEL), F32)

        def zcopy(start):
            return pltpu.make_async_copy(
                sorted_ref.at[1, pl.ds(0, BM)],
                xs_hbm.at[pl.ds(pl.multiple_of(start, BM), BM)], sems.at[1])

        for e in range(N_EXPERTS):
            zcopy(jnp.maximum(pend_ref[e] - BM, 0)).start()
        for e in range(N_EXPERTS):
            zcopy(jnp.maximum(pend_ref[e] - BM, 0)).wait()
        for blk in range(min_blocks, nb):
            @pl.when(blk * BM >= pend_ref[N_EXPERTS - 1])
            def _():
                zcopy(blk * BM).start()
                zcopy(blk * BM).wait()

    @pl.when(i >= 2)
    def _():
        wait_tile(i - 2, b)

    lpos = _local_positions(meta_ref[...], lo8v_ref[...])
    lane = lax.broadcasted_iota(jnp.int32, (tm, LANES), 1)
    packed = jnp.zeros((tm, LANES), F32)
    for kk in range(TOP_K):
        packed = jnp.where(lane == kk, lpos[kk], packed)
    lpos_t = packed.T
    hfb = hf_ref[...]
    for j in range(RL // PROWS):
        r = (lax.broadcasted_iota(jnp.int32, (PROWS, tm), 0) + j * PROWS).astype(F32)
        hit = r == lpos_t[0:1, :]
        for kk in range(1, TOP_K):
            hit = hit | (r == lpos_t[kk:kk + 1, :])
        perm = jnp.where(hit, 1.0, 0.0).astype(BF16)
        sorted_ref[b, j * PROWS:(j + 1) * PROWS, :] = jnp.dot(
            perm, hfb, preferred_element_type=F32)

    for e in range(N_EXPERTS):
        q = i * N_EXPERTS + e
        _issue_rows(functools.partial(seg_copy, b), lo8_ref[q], goff_ref[q], nseg_ref[q])

    @pl.when(i == nt - 1)
    def _():
        @pl.when(i >= 1)
        def _():
            wait_tile(i - 1, 1 - b)
        wait_tile(i, b)


def _sort_dispatch(nseg, lo8, goff, nch, pend, hf, meta, lo8v, n_pad_rows):
    t = hf.shape[0]
    tm = TM_PROJ
    row = lambda i, *_: (i, 0)
    return pl.pallas_call(
        functools.partial(_sort_kernel, min_blocks=t * TOP_K // BM),
        grid_spec=pltpu.PrefetchScalarGridSpec(
            num_scalar_prefetch=5,
            grid=(t // tm,),
            in_specs=[
                pl.BlockSpec((tm, D_MODEL), row),
                pl.BlockSpec((tm, LANES), row),
                pl.BlockSpec((None, 1, LANES), lambda i, *_: (i, 0, 0)),
            ],
            out_specs=pl.BlockSpec(memory_space=pl.ANY),
            scratch_shapes=[pltpu.VMEM((2, RL, D_MODEL), F32), pltpu.SemaphoreType.DMA((2,))],
        ),
        out_shape=jax.ShapeDtypeStruct((n_pad_rows, D_MODEL), F32),
        compiler_params=_cparams(("arbitrary",)),
        name="sort_dispatch",
    )(nseg, lo8, goff, nch, pend, hf, meta, lo8v)


def _ffn_kernel(be_ref, nv_ref, x_ref, wg_ref, bg_ref, wu_ref, bu_ref, wd_ref, bd_ref, o_ref,
                wbf_ref):
    i = pl.program_id(0)

    @pl.when(i < nv_ref[0])
    def _():
        @pl.when(jnp.logical_or(i == 0, be_ref[i] != be_ref[jnp.maximum(i - 1, 0)]))
        def _():
            wbf_ref[0] = wg_ref[...].astype(BF16)
            wbf_ref[1] = wu_ref[...].astype(BF16)
            wbf_ref[2] = wd_ref[...].astype(BF16)

        x = x_ref[...].astype(BF16)
        y = jnp.broadcast_to(bd_ref[...], o_ref.shape)
        for c in range(D_MODEL // FFN_COLS):
            cs = slice(c * FFN_COLS, (c + 1) * FFN_COLS)
            g = jnp.dot(x, wbf_ref[0, :, cs], preferred_element_type=F32) + bg_ref[:, cs]
            g = jnp.minimum(g, SWIGLU_LIMIT)
            u = jnp.dot(x, wbf_ref[1, :, cs], preferred_element_type=F32) + bu_ref[:, cs]
            u = jnp.clip(u, -SWIGLU_LIMIT, SWIGLU_LIMIT)
            act = g * jax.nn.sigmoid(SWIGLU_ALPHA * g) * (u + 1.0)
            y = y + jnp.dot(act.astype(BF16), wbf_ref[2, cs, :], preferred_element_type=F32)
        o_ref[...] = y

    @pl.when(pl.program_id(0) >= nv_ref[0])
    def _():
        o_ref[...] = jnp.zeros_like(o_ref)


def _expert_ffn(block_e, nvalid, xs, wg, bg, wu, bu, wd, bd):
    n_rows = xs.shape[0]
    nb = n_rows // BM

    def blk(i, be, nv):
        return jnp.minimum(i, nv[0] - 1)

    xmap = lambda i, be, nv: (blk(i, be, nv), 0)
    wmap = lambda i, be, nv: (be[blk(i, be, nv)], 0, 0)
    return pl.pallas_call(
        _ffn_kernel,
        grid_spec=pltpu.PrefetchScalarGridSpec(
            num_scalar_prefetch=2,
            grid=(nb,),
            in_specs=[
                pl.BlockSpec((BM, D_MODEL), xmap),
                pl.BlockSpec((None, D_MODEL, D_MODEL), wmap),
                pl.BlockSpec((None, 1, D_MODEL), wmap),
                pl.BlockSpec((None, D_MODEL, D_MODEL), wmap),
                pl.BlockSpec((None, 1, D_MODEL), wmap),
                pl.BlockSpec((None, D_MODEL, D_MODEL), wmap),
                pl.BlockSpec((None, 1, D_MODEL), wmap),
            ],
            out_specs=pl.BlockSpec((BM, D_MODEL), lambda i, be, nv: (i, 0)),
            scratch_shapes=[pltpu.VMEM((3, D_MODEL, D_MODEL), BF16)],
        ),
        out_shape=jax.ShapeDtypeStruct((n_rows, D_MODEL), F32),
        compiler_params=pltpu.CompilerParams(dimension_semantics=("arbitrary",),
                                             vmem_limit_bytes=FFN_VMEM_LIMIT),
        name="expert_ffn",
    )(block_e, nvalid, xs, wg, bg, wu, bu, wd, bd)


def _combine_kernel(nseg_ref, lo8_ref, goff_ref, nch_ref,
                    x1_ref, meta_ref, lo8v_ref, g_ref, yb_hbm, o_ref, ybl_ref, sems):
    i = pl.program_id(0)
    nt = pl.num_programs(0)
    b = i % 2
    tm = x1_ref.shape[0]

    def seg_copy(buf, lo, go, rows):
        return pltpu.make_async_copy(
            yb_hbm.at[pl.ds(pl.multiple_of(go, SEG), rows)],
            ybl_ref.at[buf, pl.ds(pl.multiple_of(lo, SEG), rows)], sems.at[buf])

    def issue_tile(tile, buf):
        for e in range(N_EXPERTS):
            q = tile * N_EXPERTS + e
            _issue_rows(functools.partial(seg_copy, buf), lo8_ref[q], goff_ref[q], nseg_ref[q])

    def wait_tile(tile, buf):
        _wait_rows(functools.partial(seg_copy, buf, 0, 0), nch_ref[tile])

    @pl.when(i == 0)
    def _():
        ybl_ref[...] = jnp.zeros_like(ybl_ref)
        issue_tile(0, 0)

    @pl.when(i + 1 < nt)
    def _():
        issue_tile(i + 1, 1 - b)

    wait_tile(i, b)

    meta = meta_ref[...]
    lpos = _local_positions(meta, lo8v_ref[...])
    acc = x1_ref[...]
    for j in range(RL // PCOLS):
        r = (lax.broadcasted_iota(jnp.int32, (tm, PCOLS), 1) + j * PCOLS).astype(F32)
        gmat = jnp.zeros((tm, PCOLS), F32)
        for kk in range(TOP_K):
            gmat = jnp.where(r == lpos[kk], meta[:, 2 * TOP_K + kk:2 * TOP_K + kk + 1], gmat)
        rows = ybl_ref[b, j * PCOLS:(j + 1) * PCOLS, :].astype(BF16)
        acc = acc + jnp.dot(gmat.astype(BF16), rows, preferred_element_type=F32)
    ms = jnp.mean(acc * acc, axis=-1, keepdims=True)
    o_ref[...] = acc * lax.rsqrt(ms + NORM_EPS) * g_ref[...]


def _combine(nseg, lo8, goff, nch, x1, meta, lo8v, final_g, yb):
    t = x1.shape[0]
    tm = TM_PROJ
    row = lambda i, *_: (i, 0)
    return pl.pallas_call(
        _combine_kernel,
        grid_spec=pltpu.PrefetchScalarGridSpec(
            num_scalar_prefetch=4,
            grid=(t // tm,),
            in_specs=[
                pl.BlockSpec((tm, D_MODEL), row),
                pl.BlockSpec((tm, LANES), row),
                pl.BlockSpec((None, 1, LANES), lambda i, *_: (i, 0, 0)),
                pl.BlockSpec((1, D_MODEL), lambda i, *_: (0, 0)),
                pl.BlockSpec(memory_space=pl.ANY),
            ],
            out_specs=pl.BlockSpec((tm, D_MODEL), row),
            scratch_shapes=[pltpu.VMEM((2, RL, D_MODEL), F32), pltpu.SemaphoreType.DMA((2,))],
        ),
        out_shape=jax.ShapeDtypeStruct((t, D_MODEL), F32),
        compiler_params=_cparams(("arbitrary",)),
        name="combine",
    )(nseg, lo8, goff, nch, x1, meta, lo8v, final_g, yb)


def _block_diag(w):
    h, d, _ = w.shape
    eye = jnp.eye(h, dtype=w.dtype)
    return (eye[:, None, :, None] * w[:, :, None, :]).reshape(h * d, h * d)


def kernel(x, positions, attn_norm_g, w_in, conv_w, conv_b, lru_wa, lru_ba, lru_wx, lru_bx, lru_lambda, lru_norm_g, ret_norm_g, ret_norm_b, w_out, ffn_norm_g, router_w, router_b, moe_w_gate, moe_b_gate, moe_w_up, moe_b_up, moe_w_down, moe_b_down, final_norm_g):
    bsz, seq, d = x.shape
    depth = w_in.shape[0]
    t = bsz * seq
    assert depth == 1 and d == D_MODEL and seq % TC_SEQ == 0 and t % TM_PROJ == 0
    nt = t // TM_PROJ
    n_pad = t * TOP_K + nt * N_EXPERTS * SEG + N_EXPERTS * BM
    nb = n_pad // BM

    half = RET_HEAD_DIM // 2
    inv_freq = ROPE_THETA ** (-jnp.arange(half, dtype=F32) / half)
    inv_freq = jnp.concatenate([inv_freq, inv_freq]).reshape(1, RET_HEAD_DIM)
    sign = jnp.concatenate([-jnp.ones((half,), F32), jnp.ones((half,), F32)]).reshape(1, RET_HEAD_DIM)
    pos_col = positions.reshape(t, 1).astype(jnp.int32)
    tri = jnp.tril(jnp.ones((TM_PROJ, TM_PROJ), F32), -1).astype(BF16)

    x2 = x.reshape(t, d)
    for l in range(depth):
        proj = _inproj(x2, attn_norm_g[l].reshape(1, d), w_in[l].astype(BF16))

        wg_bd = jnp.concatenate([_block_diag(lru_wa[l]), _block_diag(lru_wx[l])], axis=1).astype(BF16)
        lam = lru_lambda[l].astype(F32)
        sp = (jnp.maximum(-lam, 0.0) + jnp.log1p(jnp.exp(-jnp.abs(lam)))).reshape(1, D_LRU)
        y_lru = _lru_branch(proj, bsz, seq, conv_w[l], conv_b[l].reshape(1, D_LRU), wg_bd,
                            lru_ba[l].reshape(1, D_LRU), lru_bx[l].reshape(1, D_LRU), sp,
                            lru_norm_g[l].reshape(1, D_LRU))
        y_ret = _ret_branch(proj, pos_col, bsz, seq, inv_freq, sign,
                            ret_norm_g[l].reshape(1, D_RET), ret_norm_b[l].reshape(1, D_RET))

        rw_pad = jnp.zeros((d, LANES), F32).at[:, :N_EXPERTS].set(router_w[l]).astype(BF16)
        rb_pad = jnp.full((1, LANES), -1e30, F32).at[0, :N_EXPERTS].set(router_b[l])
        x1, hf, meta, cnt_tile = _outproj_router(x2, y_lru, y_ret, w_out[l].astype(BF16),
                                            ffn_norm_g[l].reshape(1, d), rw_pad, rb_pad, tri)

        cnt = cnt_tile[:, 0, :N_EXPERTS].astype(jnp.int32)
        c8 = (cnt + SEG - 1) // SEG * SEG
        lo8 = jnp.cumsum(c8, axis=1) - c8
        padded = (jnp.sum(c8, axis=0) + BM - 1) // BM * BM
        pend = jnp.cumsum(padded).astype(jnp.int32)
        pstart = pend - padded
        goff = (pstart[None, :] + jnp.cumsum(c8, axis=0) - c8).astype(jnp.int32)
        nseg = (c8 // SEG).astype(jnp.int32)
        nch = jnp.sum(nseg, axis=1).astype(jnp.int32)
        lo8v = jnp.zeros((nt, 1, LANES), F32).at[:, 0, :N_EXPERTS].set(lo8.astype(F32))
        block_start = jnp.arange(nb, dtype=jnp.int32) * BM
        block_e = jnp.minimum(jnp.sum(block_start[:, None] >= pend[None, :], axis=1),
                              N_EXPERTS - 1).astype(jnp.int32)
        nvalid = (pend[N_EXPERTS - 1:] // BM).astype(jnp.int32)
        nseg, lo8, goff = nseg.reshape(-1), lo8.astype(jnp.int32).reshape(-1), goff.reshape(-1)

        xs = _sort_dispatch(nseg, lo8, goff, nch, pend, hf, meta, lo8v, n_pad)
        yb = _expert_ffn(block_e, nvalid, xs,
                         moe_w_gate[l], moe_b_gate[l].reshape(N_EXPERTS, 1, d),
                         moe_w_up[l], moe_b_up[l].reshape(N_EXPERTS, 1, d),
                         moe_w_down[l], moe_b_down[l].reshape(N_EXPERTS, 1, d))
        x2 = _combine(nseg, lo8, goff, nch, x1, meta, lo8v, final_norm_g.reshape(1, d), yb)
    return x2.reshape(bsz, seq, d)
```

```python
import functools
import math

import numpy as np
import jax
import jax.numpy as jnp
from jax import lax
from jax.experimental import pallas as pl
from jax.experimental.pallas import tpu as pltpu

F32 = jnp.float32
BF16 = jnp.bfloat16

D_MODEL = 1024
D_LRU = 512
D_RET = 512
LRU_HEADS = 8
LRU_HEAD_DIM = D_LRU // LRU_HEADS
CONV_WIDTH = 4
LRU_C = 8.0
RET_HEADS = 4
RET_HEAD_DIM = D_RET // RET_HEADS
ROPE_THETA = 10000.0
D_IN = 2 * D_LRU + 4 * D_RET
N_EXPERTS = 32
TOP_K = 4
SWIGLU_ALPHA = 1.702
SWIGLU_LIMIT = 7.0
NORM_EPS = 1e-5

LANES = 128
SUBLANES = 8
VMEM_LIMIT = 48 * 1024 * 1024

TM_PROJ = 512
TC_SEQ = 256
BM = 512
FFN_COLS = 512
FFN_VMEM_LIMIT = 56 * 1024 * 1024
SEG = SUBLANES
RL = TM_PROJ * TOP_K + N_EXPERTS * SEG
PCOLS = 256
PROWS = 768
BIG_SHIFT = 2
BIG = SEG << BIG_SHIFT
WAIT_SHIFT = 5
WAITBIG = SEG << WAIT_SHIFT

LOG_GAMMA = [math.log1p(-(2.0 ** (-5.0 - h))) for h in range(RET_HEADS)]


def _cparams(sem):
    return pltpu.CompilerParams(dimension_semantics=sem, vmem_limit_bytes=VMEM_LIMIT)


def _inproj_kernel(x_ref, g_ref, w_ref, o_ref):
    x = x_ref[...]
    ms = jnp.mean(x * x, axis=-1, keepdims=True)
    h = x * lax.rsqrt(ms + NORM_EPS) * g_ref[...]
    o_ref[...] = jnp.dot(h.astype(BF16), w_ref[...],
                         preferred_element_type=F32).astype(o_ref.dtype)


def _inproj(x2, g, w_in_bf16):
    t = x2.shape[0]
    return pl.pallas_call(
        _inproj_kernel,
        grid=(t // TM_PROJ,),
        in_specs=[
            pl.BlockSpec((TM_PROJ, D_MODEL), lambda i: (i, 0)),
            pl.BlockSpec((1, D_MODEL), lambda i: (0, 0)),
            pl.BlockSpec((D_MODEL, D_IN), lambda i: (0, 0)),
        ],
        out_specs=pl.BlockSpec((TM_PROJ, D_IN), lambda i: (i, 0)),
        out_shape=jax.ShapeDtypeStruct((t, D_IN), BF16),
        compiler_params=_cparams(("arbitrary",)),
        name="inproj",
    )(x2, g, w_in_bf16)


def _lru_kernel(x_ref, gate_ref, cw_ref, cb_ref, wg_ref, ba_ref, bx_ref, sp_ref, ng_ref,
                o_ref, xext_ref, h_ref):
    s = pl.program_id(1)
    tc = x_ref.shape[0]

    @pl.when(s == 0)
    def _():
        xext_ref[0:SUBLANES, :] = jnp.zeros((SUBLANES, D_LRU), F32)
        h_ref[...] = jnp.zeros_like(h_ref)

    xext_ref[SUBLANES:SUBLANES + tc, :] = x_ref[...].astype(F32)
    xc = cb_ref[...] + cw_ref[CONV_WIDTH - 1:CONV_WIDTH, :] * xext_ref[SUBLANES:SUBLANES + tc, :]
    for j in range(CONV_WIDTH - 1):
        off = SUBLANES - (CONV_WIDTH - 1) + j
        xc = xc + cw_ref[j:j + 1, :] * xext_ref[off:off + tc, :]
    xext_ref[0:SUBLANES, :] = xext_ref[tc:tc + SUBLANES, :]

    gates = jnp.dot(xc.astype(BF16), wg_ref[...], preferred_element_type=F32)
    r = jax.nn.sigmoid(gates[:, :D_LRU] + ba_ref[...])
    ig = jax.nn.sigmoid(gates[:, D_LRU:] + bx_ref[...])
    log_a = (-LRU_C) * r * sp_ref[...]
    a = jnp.exp(log_a)
    b = jnp.sqrt(1.0 - a * a) * (ig * xc)

    ng = tc // SUBLANES
    a = a.reshape(ng, SUBLANES, D_LRU)
    b = b.reshape(ng, SUBLANES, D_LRU)
    in_group = lax.broadcasted_iota(jnp.int32, (1, SUBLANES, 1), 1)
    d = 1
    while d < SUBLANES:
        a_s = pltpu.roll(a, d, 1)
        b_s = pltpu.roll(b, d, 1)
        m = in_group >= d
        b = jnp.where(m, a * b_s + b, b)
        a = jnp.where(m, a * a_s, a)
        d *= 2
    h_prev = h_ref[0:1, :]
    groups = []
    for g in range(ng):
        hg = b[g] + a[g] * h_prev
        groups.append(hg)
        h_prev = hg[SUBLANES - 1:SUBLANES, :]
    h = jnp.concatenate(groups, axis=0)
    h_ref[0:1, :] = h_prev

    y = h * jax.nn.gelu(gate_ref[...].astype(F32))
    ms = jnp.mean(y * y, axis=-1, keepdims=True)
    o_ref[...] = (y * lax.rsqrt(ms + NORM_EPS) * ng_ref[...]).astype(o_ref.dtype)


def _lru_branch(proj, bsz, seq, conv_w, conv_b, wg_bd, ba, bx, softplus_neg_lam, norm_g):
    ns = seq // TC_SEQ
    row = lambda b, s: (b * ns + s, 0)
    const = lambda b, s: (0, 0)
    return pl.pallas_call(
        _lru_kernel,
        grid=(bsz, ns),
        in_specs=[
            pl.BlockSpec((TC_SEQ, D_LRU), lambda b, s: (b * ns + s, 0)),
            pl.BlockSpec((TC_SEQ, D_LRU), lambda b, s: (b * ns + s, 1)),
            pl.BlockSpec((CONV_WIDTH, D_LRU), const),
            pl.BlockSpec((1, D_LRU), const),
            pl.BlockSpec((D_LRU, 2 * D_LRU), const),
            pl.BlockSpec((1, D_LRU), const),
            pl.BlockSpec((1, D_LRU), const),
            pl.BlockSpec((1, D_LRU), const),
            pl.BlockSpec((1, D_LRU), const),
        ],
        out_specs=pl.BlockSpec((TC_SEQ, D_LRU), row),
        out_shape=jax.ShapeDtypeStruct((bsz * seq, D_LRU), BF16),
        scratch_shapes=[
            pltpu.VMEM((TC_SEQ + SUBLANES, D_LRU), F32),
            pltpu.VMEM((SUBLANES, D_LRU), F32),
        ],
        compiler_params=_cparams(("arbitrary", "arbitrary")),
        name="lru_branch",
    )(proj, proj, conv_w, conv_b, wg_bd, ba, bx, softplus_neg_lam, norm_g)


def _ret_kernel(pos_ref, invf_ref, sgn_ref, q_ref, k_ref, v_ref, g_ref, gg_ref, gb_ref,
                o_ref, state_ref, decay_ref):
    s = pl.program_id(1)
    tc = q_ref.shape[0]
    dh = RET_HEAD_DIM

    @pl.when(jnp.logical_and(pl.program_id(0) == 0, s == 0))
    def _():
        ri = lax.broadcasted_iota(jnp.int32, (tc, tc), 0)
        ci = lax.broadcasted_iota(jnp.int32, (tc, tc), 1)
        rel = (ri - ci).astype(F32)
        causal = rel >= 0.0
        relc = jnp.where(causal, rel, 0.0)
        for h in range(RET_HEADS):
            decay_ref[h] = jnp.where(causal, jnp.exp(LOG_GAMMA[h] * relc), 0.0)

    @pl.when(s == 0)
    def _():
        state_ref[...] = jnp.zeros_like(state_ref)

    hr = tc // 2
    pos = pos_ref[...].astype(F32)
    first = lax.broadcasted_iota(jnp.int32, (hr, dh), 1) < dh // 2
    ang = jnp.where(first, pos[0:hr, :], pos[hr:tc, :]) * invf_ref[...]
    c2 = jnp.cos(ang)
    s2 = jnp.sin(ang)
    c2r = pltpu.roll(c2, dh // 2, 1)
    s2r = pltpu.roll(s2, dh // 2, 1)
    cos = jnp.concatenate([jnp.where(first, c2, c2r), jnp.where(first, c2r, c2)], axis=0)
    sin = jnp.concatenate([jnp.where(first, s2, s2r), jnp.where(first, s2r, s2)], axis=0)
    sin_signed = sin * sgn_ref[...]
    rowf = lax.broadcasted_iota(jnp.int32, (tc, 1), 0).astype(F32)
    scale = dh ** -0.5

    for h in range(RET_HEADS):
        sl = slice(h * dh, (h + 1) * dh)
        q = q_ref[:, sl].astype(F32)
        k = k_ref[:, sl].astype(F32)
        v = v_ref[:, sl]
        qr = q * cos + pltpu.roll(q, dh // 2, 1) * sin_signed
        kr = (k * cos + pltpu.roll(k, dh // 2, 1) * sin_signed) * scale
        qb = qr.astype(BF16)
        kb = kr.astype(BF16)
        lg = LOG_GAMMA[h]
        scores = lax.dot_general(qb, kb, (((1,), (1,)), ((), ())),
                                 preferred_element_type=F32) * decay_ref[h]
        intra = jnp.dot(scores.astype(BF16), v, preferred_element_type=F32)
        st = state_ref[h]
        xi = jnp.exp(lg * (rowf + 1.0))
        cross = jnp.dot(qb, st.astype(BF16), preferred_element_type=F32) * xi
        o = intra + cross
        zeta = jnp.exp(lg * (float(tc) - 1.0 - rowf))
        kz = (kr * zeta).astype(BF16)
        kv = lax.dot_general(kz, v, (((0,), (0,)), ((), ())), preferred_element_type=F32)
        state_ref[h] = math.exp(lg * tc) * st + kv

        mu = jnp.mean(o, axis=-1, keepdims=True)
        oc = o - mu
        var = jnp.mean(oc * oc, axis=-1, keepdims=True)
        on = oc * lax.rsqrt(var + NORM_EPS) * gg_ref[:, sl] + gb_ref[:, sl]
        gate = g_ref[:, sl].astype(F32)
        o_ref[:, sl] = (gate * jax.nn.sigmoid(gate) * on).astype(o_ref.dtype)


def _ret_branch(proj, pos_col, bsz, seq, inv_freq, sign, gn_g, gn_b):
    ns = seq // TC_SEQ
    const = lambda b, s: (0, 0)
    col = lambda c: (lambda b, s: (b * ns + s, c))
    return pl.pallas_call(
        _ret_kernel,
        grid=(bsz, ns),
        in_specs=[
            pl.BlockSpec((TC_SEQ, 1), lambda b, s: (b * ns + s, 0)),
            pl.BlockSpec((1, RET_HEAD_DIM), const),
            pl.BlockSpec((1, RET_HEAD_DIM), const),
            pl.BlockSpec((TC_SEQ, D_RET), col(2)),
            pl.BlockSpec((TC_SEQ, D_RET), col(3)),
            pl.BlockSpec((TC_SEQ, D_RET), col(4)),
            pl.BlockSpec((TC_SEQ, D_RET), col(5)),
            pl.BlockSpec((1, D_RET), const),
            pl.BlockSpec((1, D_RET), const),
        ],
        out_specs=pl.BlockSpec((TC_SEQ, D_RET), lambda b, s: (b * ns + s, 0)),
        out_shape=jax.ShapeDtypeStruct((bsz * seq, D_RET), BF16),
        scratch_shapes=[
            pltpu.VMEM((RET_HEADS, RET_HEAD_DIM, RET_HEAD_DIM), F32),
            pltpu.VMEM((RET_HEADS, TC_SEQ, TC_SEQ), F32),
        ],
        compiler_params=_cparams(("arbitrary", "arbitrary")),
        name="ret_branch",
    )(pos_col, inv_freq, sign, proj, proj, proj, proj, gn_g, gn_b)


def _outproj_router_kernel(x_ref, yl_ref, yr_ref, wo_ref, ng_ref, rw_ref, rb_ref, tri_ref,
                           x1_ref, hf_ref, meta_ref, cnt_ref):
    tm = x_ref.shape[0]

    y = jnp.dot(yl_ref[...], wo_ref[0:D_LRU, :], preferred_element_type=F32)
    y = y + jnp.dot(yr_ref[...], wo_ref[D_LRU:, :], preferred_element_type=F32)
    x1 = x_ref[...] + y
    x1_ref[...] = x1
    ms = jnp.mean(x1 * x1, axis=-1, keepdims=True)
    hf = x1 * lax.rsqrt(ms + NORM_EPS) * ng_ref[...]
    hf_ref[...] = hf.astype(hf_ref.dtype)

    logits = jnp.dot(hf.astype(BF16), rw_ref[...], preferred_element_type=F32) + rb_ref[...]
    lane = lax.broadcasted_iota(jnp.int32, (tm, LANES), 1)
    lane_f = lane.astype(F32)
    work = logits
    vals, idxs = [], []
    onehot = jnp.zeros((tm, LANES), F32)
    for _ in range(TOP_K):
        m = jnp.max(work, axis=-1, keepdims=True)
        idx = jnp.min(jnp.where(work == m, lane_f, float(LANES)), axis=-1, keepdims=True)
        sel = lane_f == idx
        work = jnp.where(sel, -jnp.inf, work)
        onehot = jnp.where(sel, 1.0, onehot)
        vals.append(m)
        idxs.append(idx)
    exps = [jnp.exp(v - vals[0]) for v in vals]
    denom = exps[0] + exps[1] + exps[2] + exps[3]
    gates = [e / denom for e in exps]

    rank_mat = jnp.dot(tri_ref[...], onehot.astype(BF16), preferred_element_type=F32)
    meta = jnp.zeros((tm, LANES), F32)
    for kk in range(TOP_K):
        rk = jnp.sum(jnp.where(lane_f == idxs[kk], rank_mat, 0.0), axis=-1, keepdims=True)
        meta = jnp.where(lane == kk, idxs[kk], meta)
        meta = jnp.where(lane == TOP_K + kk, rk, meta)
        meta = jnp.where(lane == 2 * TOP_K + kk, gates[kk], meta)
    meta_ref[...] = meta
    cnt_ref[...] = jnp.broadcast_to(jnp.sum(onehot, axis=0, keepdims=True), cnt_ref.shape)


def _outproj_router(x2, y_lru, y_ret, w_out_bf16, ng, rw_pad, rb_pad, tri):
    t = x2.shape[0]
    tm = TM_PROJ
    const = lambda i: (0, 0)
    row = lambda i: (i, 0)
    return pl.pallas_call(
        _outproj_router_kernel,
        grid=(t // tm,),
        in_specs=[
            pl.BlockSpec((tm, D_MODEL), row),
            pl.BlockSpec((tm, D_LRU), row),
            pl.BlockSpec((tm, D_RET), row),
            pl.BlockSpec((D_MODEL, D_MODEL), const),
            pl.BlockSpec((1, D_MODEL), const),
            pl.BlockSpec((D_MODEL, LANES), const),
            pl.BlockSpec((1, LANES), const),
            pl.BlockSpec((tm, tm), const),
        ],
        out_specs=[
            pl.BlockSpec((tm, D_MODEL), row),
            pl.BlockSpec((tm, D_MODEL), row),
            pl.BlockSpec((tm, LANES), row),
            pl.BlockSpec((None, SUBLANES, LANES), lambda i: (i, 0, 0)),
        ],
        out_shape=[
            jax.ShapeDtypeStruct((t, D_MODEL), F32),
            jax.ShapeDtypeStruct((t, D_MODEL), BF16),
            jax.ShapeDtypeStruct((t, LANES), F32),
            jax.ShapeDtypeStruct((t // tm, SUBLANES, LANES), F32),
        ],
        compiler_params=_cparams(("arbitrary",)),
        name="outproj_router",
    )(x2, y_lru, y_ret, w_out_bf16, ng, rw_pad, rb_pad, tri)


def _local_positions(meta, lo8_row):
    lane_f = lax.broadcasted_iota(jnp.int32, meta.shape, 1).astype(F32)
    out = []
    for kk in range(TOP_K):
        idx = meta[:, kk:kk + 1]
        rank = meta[:, TOP_K + kk:TOP_K + kk + 1]
        lo = jnp.sum(jnp.where(lane_f == idx, lo8_row, 0.0), axis=-1, keepdims=True)
        out.append(lo + rank)
    return out


def _issue_rows(copy, lo, go, nseg):
    nbig = lax.shift_right_logical(nseg, BIG_SHIFT)

    def big(j, c):
        copy(lo + j * BIG, go + j * BIG, BIG).start()
        return c

    lax.fori_loop(0, nbig, big, 0)
    done = nbig * BIG

    def small(j, c):
        copy(lo + done + j * SEG, go + done + j * SEG, SEG).start()
        return c

    lax.fori_loop(0, nseg - lax.shift_left(nbig, BIG_SHIFT), small, 0)


def _wait_rows(copy, nseg):
    nbig = lax.shift_right_logical(nseg, WAIT_SHIFT)

    def big(j, c):
        copy(WAITBIG).wait()
        return c

    lax.fori_loop(0, nbig, big, 0)

    def small(j, c):
        copy(SEG).wait()
        return c

    lax.fori_loop(0, nseg - lax.shift_left(nbig, WAIT_SHIFT), small, 0)


def _sort_kernel(nseg_ref, lo8_ref, goff_ref, nch_ref, pend_ref,
                 hf_ref, meta_ref, lo8v_ref, xs_hbm, sorted_ref, sems, *, min_blocks):
    i = pl.program_id(0)
    nt = pl.num_programs(0)
    b = i % 2
    tm = hf_ref.shape[0]
    nb = xs_hbm.shape[0] // BM

    def seg_copy(buf, lo, go, rows):
        return pltpu.make_async_copy(
            sorted_ref.at[buf, pl.ds(pl.multiple_of(lo, SEG), rows)],
            xs_hbm.at[pl.ds(pl.multiple_of(go, SEG), rows)], sems.at[buf])

    def wait_tile(tile, buf):
        _wait_rows(functools.partial(seg_copy, buf, 0, 0), nch_ref[tile])

    @pl.when(i == 0)
    def _():
        sorted_ref[1, 0:BM, :] = jnp.zeros((BM, D_MODEL), F32)

        def zcopy(start):
            return pltpu.make_async_copy(
                sorted_ref.at[1, pl.ds(0, BM)],
                xs_hbm.at[pl.ds(pl.multiple_of(start, BM), BM)], sems.at[1])

        for e in range(N_EXPERTS):
            zcopy(jnp.maximum(pend_ref[e] - BM, 0)).start()
        for e in range(N_EXPERTS):
            zcopy(jnp.maximum(pend_ref[e] - BM, 0)).wait()
        for blk in range(min_blocks, nb):
            @pl.when(blk * BM >= pend_ref[N_EXPERTS - 1])
            def _():
                zcopy(blk * BM).start()
                zcopy(blk * BM).wait()

    @pl.when(i >= 2)
    def _():
        wait_tile(i - 2, b)

    lpos = _local_positions(meta_ref[...], lo8v_ref[...])
    lane = lax.broadcasted_iota(jnp.int32, (tm, LANES), 1)
    packed = jnp.zeros((tm, LANES), F32)
    for kk in range(TOP_K):
        packed = jnp.where(lane == kk, lpos[kk], packed)
    lpos_t = packed.T
    hfb = hf_ref[...]
    for j in range(RL // PROWS):
        r = (lax.broadcasted_iota(jnp.int32, (PROWS, tm), 0) + j * PROWS).astype(F32)
        perm = jnp.zeros((PROWS, tm), F32)
        for kk in range(TOP_K):
            perm = jnp.where(r == lpos_t[kk:kk + 1, :], 1.0, perm)
        sorted_ref[b, j * PROWS:(j + 1) * PROWS, :] = jnp.dot(
            perm.astype(BF16), hfb, preferred_element_type=F32)

    for e in range(N_EXPERTS):
        q = i * N_EXPERTS + e
        _issue_rows(functools.partial(seg_copy, b), lo8_ref[q], goff_ref[q], nseg_ref[q])

    @pl.when(i == nt - 1)
    def _():
        @pl.when(i >= 1)
        def _():
            wait_tile(i - 1, 1 - b)
        wait_tile(i, b)


def _sort_dispatch(nseg, lo8, goff, nch, pend, hf, meta, lo8v, n_pad_rows):
    t = hf.shape[0]
    tm = TM_PROJ
    row = lambda i, *_: (i, 0)
    return pl.pallas_call(
        functools.partial(_sort_kernel, min_blocks=t * TOP_K // BM),
        grid_spec=pltpu.PrefetchScalarGridSpec(
            num_scalar_prefetch=5,
            grid=(t // tm,),
            in_specs=[
                pl.BlockSpec((tm, D_MODEL), row),
                pl.BlockSpec((tm, LANES), row),
                pl.BlockSpec((None, 1, LANES), lambda i, *_: (i, 0, 0)),
            ],
            out_specs=pl.BlockSpec(memory_space=pl.ANY),
            scratch_shapes=[pltpu.VMEM((2, RL, D_MODEL), F32), pltpu.SemaphoreType.DMA((2,))],
        ),
        out_shape=jax.ShapeDtypeStruct((n_pad_rows, D_MODEL), F32),
        compiler_params=_cparams(("arbitrary",)),
        name="sort_dispatch",
    )(nseg, lo8, goff, nch, pend, hf, meta, lo8v)


def _ffn_kernel(be_ref, nv_ref, x_ref, wg_ref, bg_ref, wu_ref, bu_ref, wd_ref, bd_ref, o_ref,
                wbf_ref):
    i = pl.program_id(0)

    @pl.when(i < nv_ref[0])
    def _():
        @pl.when(jnp.logical_or(i == 0, be_ref[i] != be_ref[jnp.maximum(i - 1, 0)]))
        def _():
            wbf_ref[0] = wg_ref[...].astype(BF16)
            wbf_ref[1] = wu_ref[...].astype(BF16)
            wbf_ref[2] = wd_ref[...].astype(BF16)

        x = x_ref[...].astype(BF16)
        y = jnp.broadcast_to(bd_ref[...], o_ref.shape)
        for c in range(D_MODEL // FFN_COLS):
            cs = slice(c * FFN_COLS, (c + 1) * FFN_COLS)
            g = jnp.dot(x, wbf_ref[0, :, cs], preferred_element_type=F32) + bg_ref[:, cs]
            g = jnp.minimum(g, SWIGLU_LIMIT)
            u = jnp.dot(x, wbf_ref[1, :, cs], preferred_element_type=F32) + bu_ref[:, cs]
            u = jnp.clip(u, -SWIGLU_LIMIT, SWIGLU_LIMIT)
            act = g * jax.nn.sigmoid(SWIGLU_ALPHA * g) * (u + 1.0)
            y = y + jnp.dot(act.astype(BF16), wbf_ref[2, cs, :], preferred_element_type=F32)
        o_ref[...] = y

    @pl.when(pl.program_id(0) >= nv_ref[0])
    def _():
        o_ref[...] = jnp.zeros_like(o_ref)


def _expert_ffn(block_e, nvalid, xs, wg, bg, wu, bu, wd, bd):
    n_rows = xs.shape[0]
    nb = n_rows // BM

    def blk(i, be, nv):
        return jnp.minimum(i, nv[0] - 1)

    xmap = lambda i, be, nv: (blk(i, be, nv), 0)
    wmap = lambda i, be, nv: (be[blk(i, be, nv)], 0, 0)
    return pl.pallas_call(
        _ffn_kernel,
        grid_spec=pltpu.PrefetchScalarGridSpec(
            num_scalar_prefetch=2,
            grid=(nb,),
            in_specs=[
                pl.BlockSpec((BM, D_MODEL), xmap),
                pl.BlockSpec((None, D_MODEL, D_MODEL), wmap),
                pl.BlockSpec((None, 1, D_MODEL), wmap),
                pl.BlockSpec((None, D_MODEL, D_MODEL), wmap),
                pl.BlockSpec((None, 1, D_MODEL), wmap),
                pl.BlockSpec((None, D_MODEL, D_MODEL), wmap),
                pl.BlockSpec((None, 1, D_MODEL), wmap),
            ],
            out_specs=pl.BlockSpec((BM, D_MODEL), lambda i, be, nv: (i, 0)),
            scratch_shapes=[pltpu.VMEM((3, D_MODEL, D_MODEL), BF16)],
        ),
        out_shape=jax.ShapeDtypeStruct((n_rows, D_MODEL), F32),
        compiler_params=pltpu.CompilerParams(dimension_semantics=("arbitrary",),
                                             vmem_limit_bytes=FFN_VMEM_LIMIT),
        name="expert_ffn",
    )(block_e, nvalid, xs, wg, bg, wu, bu, wd, bd)


def _combine_kernel(nseg_ref, lo8_ref, goff_ref, nch_ref,
                    x1_ref, meta_ref, lo8v_ref, g_ref, yb_hbm, o_ref, ybl_ref, sems):
    i = pl.program_id(0)
    nt = pl.num_programs(0)
    b = i % 2
    tm = x1_ref.shape[0]

    def seg_copy(buf, lo, go, rows):
        return pltpu.make_async_copy(
            yb_hbm.at[pl.ds(pl.multiple_of(go, SEG), rows)],
            ybl_ref.at[buf, pl.ds(pl.multiple_of(lo, SEG), rows)], sems.at[buf])

    def issue_tile(tile, buf):
        for e in range(N_EXPERTS):
            q = tile * N_EXPERTS + e
            _issue_rows(functools.partial(seg_copy, buf), lo8_ref[q], goff_ref[q], nseg_ref[q])

    def wait_tile(tile, buf):
        _wait_rows(functools.partial(seg_copy, buf, 0, 0), nch_ref[tile])

    @pl.when(i == 0)
    def _():
        ybl_ref[...] = jnp.zeros_like(ybl_ref)
        issue_tile(0, 0)

    @pl.when(i + 1 < nt)
    def _():
        issue_tile(i + 1, 1 - b)

    wait_tile(i, b)

    meta = meta_ref[...]
    lpos = _local_positions(meta, lo8v_ref[...])
    acc = x1_ref[...]
    for j in range(RL // PCOLS):
        r = (lax.broadcasted_iota(jnp.int32, (tm, PCOLS), 1) + j * PCOLS).astype(F32)
        gmat = jnp.zeros((tm, PCOLS), F32)
        for kk in range(TOP_K):
            gmat = jnp.where(r == lpos[kk], meta[:, 2 * TOP_K + kk:2 * TOP_K + kk + 1], gmat)
        rows = ybl_ref[b, j * PCOLS:(j + 1) * PCOLS, :].astype(BF16)
        acc = acc + jnp.dot(gmat.astype(BF16), rows, preferred_element_type=F32)
    ms = jnp.mean(acc * acc, axis=-1, keepdims=True)
    o_ref[...] = acc * lax.rsqrt(ms + NORM_EPS) * g_ref[...]


def _combine(nseg, lo8, goff, nch, x1, meta, lo8v, final_g, yb):
    t = x1.shape[0]
    tm = TM_PROJ
    row = lambda i, *_: (i, 0)
    return pl.pallas_call(
        _combine_kernel,
        grid_spec=pltpu.PrefetchScalarGridSpec(
            num_scalar_prefetch=4,
            grid=(t // tm,),
            in_specs=[
                pl.BlockSpec((tm, D_MODEL), row),
                pl.BlockSpec((tm, LANES), row),
                pl.BlockSpec((None, 1, LANES), lambda i, *_: (i, 0, 0)),
                pl.BlockSpec((1, D_MODEL), lambda i, *_: (0, 0)),
                pl.BlockSpec(memory_space=pl.ANY),
            ],
            out_specs=pl.BlockSpec((tm, D_MODEL), row),
            scratch_shapes=[pltpu.VMEM((2, RL, D_MODEL), F32), pltpu.SemaphoreType.DMA((2,))],
        ),
        out_shape=jax.ShapeDtypeStruct((t, D_MODEL), F32),
        compiler_params=_cparams(("arbitrary",)),
        name="combine",
    )(nseg, lo8, goff, nch, x1, meta, lo8v, final_g, yb)


def _block_diag(w):
    h, d, _ = w.shape
    eye = jnp.eye(h, dtype=w.dtype)
    return (eye[:, None, :, None] * w[:, :, None, :]).reshape(h * d, h * d)


def kernel(x, positions, attn_norm_g, w_in, conv_w, conv_b, lru_wa, lru_ba, lru_wx, lru_bx, lru_lambda, lru_norm_g, ret_norm_g, ret_norm_b, w_out, ffn_norm_g, router_w, router_b, moe_w_gate, moe_b_gate, moe_w_up, moe_b_up, moe_w_down, moe_b_down, final_norm_g):
    bsz, seq, d = x.shape
    depth = w_in.shape[0]
    t = bsz * seq
    assert depth == 1 and d == D_MODEL and seq % TC_SEQ == 0 and t % TM_PROJ == 0
    nt = t // TM_PROJ
    n_pad = t * TOP_K + nt * N_EXPERTS * SEG + N_EXPERTS * BM
    nb = n_pad // BM

    half = RET_HEAD_DIM // 2
    inv_freq = ROPE_THETA ** (-jnp.arange(half, dtype=F32) / half)
    inv_freq = jnp.concatenate([inv_freq, inv_freq]).reshape(1, RET_HEAD_DIM)
    sign = jnp.concatenate([-jnp.ones((half,), F32), jnp.ones((half,), F32)]).reshape(1, RET_HEAD_DIM)
    pos_col = positions.reshape(t, 1).astype(jnp.int32)
    tri = jnp.tril(jnp.ones((TM_PROJ, TM_PROJ), F32), -1).astype(BF16)

    x2 = x.reshape(t, d)
    for l in range(depth):
        proj = _inproj(x2, attn_norm_g[l].reshape(1, d), w_in[l].astype(BF16))

        wg_bd = jnp.concatenate([_block_diag(lru_wa[l]), _block_diag(lru_wx[l])], axis=1).astype(BF16)
        lam = lru_lambda[l].astype(F32)
        sp = (jnp.maximum(-lam, 0.0) + jnp.log1p(jnp.exp(-jnp.abs(lam)))).reshape(1, D_LRU)
        y_lru = _lru_branch(proj, bsz, seq, conv_w[l], conv_b[l].reshape(1, D_LRU), wg_bd,
                            lru_ba[l].reshape(1, D_LRU), lru_bx[l].reshape(1, D_LRU), sp,
                            lru_norm_g[l].reshape(1, D_LRU))
        y_ret = _ret_branch(proj, pos_col, bsz, seq, inv_freq, sign,
                            ret_norm_g[l].reshape(1, D_RET), ret_norm_b[l].reshape(1, D_RET))

        rw_pad = jnp.zeros((d, LANES), F32).at[:, :N_EXPERTS].set(router_w[l]).astype(BF16)
        rb_pad = jnp.full((1, LANES), -1e30, F32).at[0, :N_EXPERTS].set(router_b[l])
        x1, hf, meta, cnt_tile = _outproj_router(x2, y_lru, y_ret, w_out[l].astype(BF16),
                                            ffn_norm_g[l].reshape(1, d), rw_pad, rb_pad, tri)

        cnt = cnt_tile[:, 0, :N_EXPERTS].astype(jnp.int32)
        c8 = (cnt + SEG - 1) // SEG * SEG
        lo8 = jnp.cumsum(c8, axis=1) - c8
        padded = (jnp.sum(c8, axis=0) + BM - 1) // BM * BM
        pend = jnp.cumsum(padded).astype(jnp.int32)
        pstart = pend - padded
        goff = (pstart[None, :] + jnp.cumsum(c8, axis=0) - c8).astype(jnp.int32)
        nseg = (c8 // SEG).astype(jnp.int32)
        nch = jnp.sum(nseg, axis=1).astype(jnp.int32)
        lo8v = jnp.zeros((nt, 1, LANES), F32).at[:, 0, :N_EXPERTS].set(lo8.astype(F32))
        block_start = jnp.arange(nb, dtype=jnp.int32) * BM
        block_e = jnp.minimum(jnp.sum(block_start[:, None] >= pend[None, :], axis=1),
                              N_EXPERTS - 1).astype(jnp.int32)
        nvalid = (pend[N_EXPERTS - 1:] // BM).astype(jnp.int32)
        nseg, lo8, goff = nseg.reshape(-1), lo8.astype(jnp.int32).reshape(-1), goff.reshape(-1)

        xs = _sort_dispatch(nseg, lo8, goff, nch, pend, hf, meta, lo8v, n_pad)
        yb = _expert_ffn(block_e, nvalid, xs,
                         moe_w_gate[l], moe_b_gate[l].reshape(N_EXPERTS, 1, d),
                         moe_w_up[l], moe_b_up[l].reshape(N_EXPERTS, 1, d),
                         moe_w_down[l], moe_b_down[l].reshape(N_EXPERTS, 1, d))
        x2 = _combine(nseg, lo8, goff, nch, x1, meta, lo8v, final_norm_g.reshape(1, d), yb)
    return x2.reshape(bsz, seq, d)
```

```python
import functools
import math

import numpy as np
import jax
import jax.numpy as jnp
from jax import lax
from jax.experimental import pallas as pl
from jax.experimental.pallas import tpu as pltpu

F32 = jnp.float32
BF16 = jnp.bfloat16

D_MODEL = 1024
D_LRU = 512
D_RET = 512
LRU_HEADS = 8
LRU_HEAD_DIM = D_LRU // LRU_HEADS
CONV_WIDTH = 4
LRU_C = 8.0
RET_HEADS = 4
RET_HEAD_DIM = D_RET // RET_HEADS
ROPE_THETA = 10000.0
D_IN = 2 * D_LRU + 4 * D_RET
N_EXPERTS = 32
TOP_K = 4
SWIGLU_ALPHA = 1.702
SWIGLU_LIMIT = 7.0
NORM_EPS = 1e-5

LANES = 128
SUBLANES = 8
VMEM_LIMIT = 48 * 1024 * 1024

TM_PROJ = 512
TC_SEQ = 256
BM = 512
FFN_COLS = 512
SEG = SUBLANES
RL = TM_PROJ * TOP_K + N_EXPERTS * SEG
PCOLS = 256
PROWS = 768
BIG_SHIFT = 2
BIG = SEG << BIG_SHIFT
MAX_BIG = RL // BIG
MAX_SML = N_EXPERTS * ((1 << BIG_SHIFT) - 1)
WAIT_SHIFT = 5
WAITBIG = SEG << WAIT_SHIFT

LOG_GAMMA = [math.log1p(-(2.0 ** (-5.0 - h))) for h in range(RET_HEADS)]


def _cparams(sem):
    return pltpu.CompilerParams(dimension_semantics=sem, vmem_limit_bytes=VMEM_LIMIT)


def _inproj_kernel(x_ref, g_ref, w_ref, o_ref):
    x = x_ref[...]
    ms = jnp.mean(x * x, axis=-1, keepdims=True)
    h = x * lax.rsqrt(ms + NORM_EPS) * g_ref[...]
    o_ref[...] = jnp.dot(h.astype(BF16), w_ref[...],
                         preferred_element_type=F32).astype(o_ref.dtype)


def _inproj(x2, g, w_in_bf16):
    t = x2.shape[0]
    return pl.pallas_call(
        _inproj_kernel,
        grid=(t // TM_PROJ,),
        in_specs=[
            pl.BlockSpec((TM_PROJ, D_MODEL), lambda i: (i, 0)),
            pl.BlockSpec((1, D_MODEL), lambda i: (0, 0)),
            pl.BlockSpec((D_MODEL, D_IN), lambda i: (0, 0)),
        ],
        out_specs=pl.BlockSpec((TM_PROJ, D_IN), lambda i: (i, 0)),
        out_shape=jax.ShapeDtypeStruct((t, D_IN), BF16),
        compiler_params=_cparams(("arbitrary",)),
        name="inproj",
    )(x2, g, w_in_bf16)


def _lru_kernel(x_ref, gate_ref, cw_ref, cb_ref, wg_ref, ba_ref, bx_ref, sp_ref, ng_ref,
                o_ref, xext_ref, h_ref):
    s = pl.program_id(1)
    tc = x_ref.shape[0]

    @pl.when(s == 0)
    def _():
        xext_ref[0:SUBLANES, :] = jnp.zeros((SUBLANES, D_LRU), F32)
        h_ref[...] = jnp.zeros_like(h_ref)

    xext_ref[SUBLANES:SUBLANES + tc, :] = x_ref[...].astype(F32)
    xc = cb_ref[...] + cw_ref[CONV_WIDTH - 1:CONV_WIDTH, :] * xext_ref[SUBLANES:SUBLANES + tc, :]
    for j in range(CONV_WIDTH - 1):
        off = SUBLANES - (CONV_WIDTH - 1) + j
        xc = xc + cw_ref[j:j + 1, :] * xext_ref[off:off + tc, :]
    xext_ref[0:SUBLANES, :] = xext_ref[tc:tc + SUBLANES, :]

    gates = jnp.dot(xc.astype(BF16), wg_ref[...], preferred_element_type=F32)
    r = jax.nn.sigmoid(gates[:, :D_LRU] + ba_ref[...])
    ig = jax.nn.sigmoid(gates[:, D_LRU:] + bx_ref[...])
    log_a = (-LRU_C) * r * sp_ref[...]
    a = jnp.exp(log_a)
    b = jnp.sqrt(1.0 - a * a) * (ig * xc)

    ng = tc // SUBLANES
    a = a.reshape(ng, SUBLANES, D_LRU)
    b = b.reshape(ng, SUBLANES, D_LRU)
    in_group = lax.broadcasted_iota(jnp.int32, (1, SUBLANES, 1), 1)
    d = 1
    while d < SUBLANES:
        a_s = pltpu.roll(a, d, 1)
        b_s = pltpu.roll(b, d, 1)
        m = in_group >= d
        b = jnp.where(m, a * b_s + b, b)
        a = jnp.where(m, a * a_s, a)
        d *= 2
    h_prev = h_ref[0:1, :]
    groups = []
    for g in range(ng):
        hg = b[g] + a[g] * h_prev
        groups.append(hg)
        h_prev = hg[SUBLANES - 1:SUBLANES, :]
    h = jnp.concatenate(groups, axis=0)
    h_ref[0:1, :] = h_prev

    y = h * jax.nn.gelu(gate_ref[...].astype(F32))
    ms = jnp.mean(y * y, axis=-1, keepdims=True)
    o_ref[...] = (y * lax.rsqrt(ms + NORM_EPS) * ng_ref[...]).astype(o_ref.dtype)


def _lru_branch(proj, bsz, seq, conv_w, conv_b, wg_bd, ba, bx, softplus_neg_lam, norm_g):
    ns = seq // TC_SEQ
    row = lambda b, s: (b * ns + s, 0)
    const = lambda b, s: (0, 0)
    return pl.pallas_call(
        _lru_kernel,
        grid=(bsz, ns),
        in_specs=[
            pl.BlockSpec((TC_SEQ, D_LRU), lambda b, s: (b * ns + s, 0)),
            pl.BlockSpec((TC_SEQ, D_LRU), lambda b, s: (b * ns + s, 1)),
            pl.BlockSpec((CONV_WIDTH, D_LRU), const),
            pl.BlockSpec((1, D_LRU), const),
            pl.BlockSpec((D_LRU, 2 * D_LRU), const),
            pl.BlockSpec((1, D_LRU), const),
            pl.BlockSpec((1, D_LRU), const),
            pl.BlockSpec((1, D_LRU), const),
            pl.BlockSpec((1, D_LRU), const),
        ],
        out_specs=pl.BlockSpec((TC_SEQ, D_LRU), row),
        out_shape=jax.ShapeDtypeStruct((bsz * seq, D_LRU), BF16),
        scratch_shapes=[
            pltpu.VMEM((TC_SEQ + SUBLANES, D_LRU), F32),
            pltpu.VMEM((SUBLANES, D_LRU), F32),
        ],
        compiler_params=_cparams(("arbitrary", "arbitrary")),
        name="lru_branch",
    )(proj, proj, conv_w, conv_b, wg_bd, ba, bx, softplus_neg_lam, norm_g)


def _ret_kernel(pos_ref, invf_ref, sgn_ref, q_ref, k_ref, v_ref, g_ref, gg_ref, gb_ref,
                o_ref, state_ref, decay_ref):
    s = pl.program_id(1)
    tc = q_ref.shape[0]
    dh = RET_HEAD_DIM

    @pl.when(jnp.logical_and(pl.program_id(0) == 0, s == 0))
    def _():
        ri = lax.broadcasted_iota(jnp.int32, (tc, tc), 0)
        ci = lax.broadcasted_iota(jnp.int32, (tc, tc), 1)
        rel = (ri - ci).astype(F32)
        causal = rel >= 0.0
        relc = jnp.where(causal, rel, 0.0)
        for h in range(RET_HEADS):
            decay_ref[h] = jnp.where(causal, jnp.exp(LOG_GAMMA[h] * relc), 0.0)

    @pl.when(s == 0)
    def _():
        state_ref[...] = jnp.zeros_like(state_ref)

    hr = tc // 2
    pos = pos_ref[...].astype(F32)
    first = lax.broadcasted_iota(jnp.int32, (hr, dh), 1) < dh // 2
    ang = jnp.where(first, pos[0:hr, :], pos[hr:tc, :]) * invf_ref[...]
    c2 = jnp.cos(ang)
    s2 = jnp.sin(ang)
    c2r = pltpu.roll(c2, dh // 2, 1)
    s2r = pltpu.roll(s2, dh // 2, 1)
    cos = jnp.concatenate([jnp.where(first, c2, c2r), jnp.where(first, c2r, c2)], axis=0)
    sin = jnp.concatenate([jnp.where(first, s2, s2r), jnp.where(first, s2r, s2)], axis=0)
    sin_signed = sin * sgn_ref[...]
    rowf = lax.broadcasted_iota(jnp.int32, (tc, 1), 0).astype(F32)
    scale = dh ** -0.5

    for h in range(RET_HEADS):
        sl = slice(h * dh, (h + 1) * dh)
        q = q_ref[:, sl].astype(F32)
        k = k_ref[:, sl].astype(F32)
        v = v_ref[:, sl]
        qr = q * cos + pltpu.roll(q, dh // 2, 1) * sin_signed
        kr = (k * cos + pltpu.roll(k, dh // 2, 1) * sin_signed) * scale
        qb = qr.astype(BF16)
        kb = kr.astype(BF16)
        lg = LOG_GAMMA[h]
        scores = lax.dot_general(qb, kb, (((1,), (1,)), ((), ())),
                                 preferred_element_type=F32) * decay_ref[h]
        intra = jnp.dot(scores.astype(BF16), v, preferred_element_type=F32)
        st = state_ref[h]
        xi = jnp.exp(lg * (rowf + 1.0))
        cross = jnp.dot(qb, st.astype(BF16), preferred_element_type=F32) * xi
        o = intra + cross
        zeta = jnp.exp(lg * (float(tc) - 1.0 - rowf))
        kz = (kr * zeta).astype(BF16)
        kv = lax.dot_general(kz, v, (((0,), (0,)), ((), ())), preferred_element_type=F32)
        state_ref[h] = math.exp(lg * tc) * st + kv

        mu = jnp.mean(o, axis=-1, keepdims=True)
        oc = o - mu
        var = jnp.mean(oc * oc, axis=-1, keepdims=True)
        on = oc * lax.rsqrt(var + NORM_EPS) * gg_ref[:, sl] + gb_ref[:, sl]
        gate = g_ref[:, sl].astype(F32)
        o_ref[:, sl] = (gate * jax.nn.sigmoid(gate) * on).astype(o_ref.dtype)


def _ret_branch(proj, pos_col, bsz, seq, inv_freq, sign, gn_g, gn_b):
    ns = seq // TC_SEQ
    const = lambda b, s: (0, 0)
    col = lambda c: (lambda b, s: (b * ns + s, c))
    return pl.pallas_call(
        _ret_kernel,
        grid=(bsz, ns),
        in_specs=[
            pl.BlockSpec((TC_SEQ, 1), lambda b, s: (b * ns + s, 0)),
            pl.BlockSpec((1, RET_HEAD_DIM), const),
            pl.BlockSpec((1, RET_HEAD_DIM), const),
            pl.BlockSpec((TC_SEQ, D_RET), col(2)),
            pl.BlockSpec((TC_SEQ, D_RET), col(3)),
            pl.BlockSpec((TC_SEQ, D_RET), col(4)),
            pl.BlockSpec((TC_SEQ, D_RET), col(5)),
            pl.BlockSpec((1, D_RET), const),
            pl.BlockSpec((1, D_RET), const),
        ],
        out_specs=pl.BlockSpec((TC_SEQ, D_RET), lambda b, s: (b * ns + s, 0)),
        out_shape=jax.ShapeDtypeStruct((bsz * seq, D_RET), BF16),
        scratch_shapes=[
            pltpu.VMEM((RET_HEADS, RET_HEAD_DIM, RET_HEAD_DIM), F32),
            pltpu.VMEM((RET_HEADS, TC_SEQ, TC_SEQ), F32),
        ],
        compiler_params=_cparams(("arbitrary", "arbitrary")),
        name="ret_branch",
    )(pos_col, inv_freq, sign, proj, proj, proj, proj, gn_g, gn_b)


def _outproj_router_kernel(x_ref, yl_ref, yr_ref, wo_ref, ng_ref, rw_ref, rb_ref, tri_ref,
                           x1_ref, hf_ref, meta_ref, cnt_ref):
    tm = x_ref.shape[0]

    y = jnp.dot(yl_ref[...], wo_ref[0:D_LRU, :], preferred_element_type=F32)
    y = y + jnp.dot(yr_ref[...], wo_ref[D_LRU:, :], preferred_element_type=F32)
    x1 = x_ref[...] + y
    x1_ref[...] = x1
    ms = jnp.mean(x1 * x1, axis=-1, keepdims=True)
    hf = x1 * lax.rsqrt(ms + NORM_EPS) * ng_ref[...]
    hf_ref[...] = hf.astype(hf_ref.dtype)

    logits = jnp.dot(hf.astype(BF16), rw_ref[...], preferred_element_type=F32) + rb_ref[...]
    lane = lax.broadcasted_iota(jnp.int32, (tm, LANES), 1)
    lane_f = lane.astype(F32)
    work = logits
    vals, idxs = [], []
    onehot = jnp.zeros((tm, LANES), F32)
    for _ in range(TOP_K):
        m = jnp.max(work, axis=-1, keepdims=True)
        idx = jnp.min(jnp.where(work == m, lane_f, float(LANES)), axis=-1, keepdims=True)
        sel = lane_f == idx
        work = jnp.where(sel, -jnp.inf, work)
        onehot = jnp.where(sel, 1.0, onehot)
        vals.append(m)
        idxs.append(idx)
    exps = [jnp.exp(v - vals[0]) for v in vals]
    denom = exps[0] + exps[1] + exps[2] + exps[3]
    gates = [e / denom for e in exps]

    rank_mat = jnp.dot(tri_ref[...], onehot.astype(BF16), preferred_element_type=F32)
    meta = jnp.zeros((tm, LANES), F32)
    for kk in range(TOP_K):
        rk = jnp.sum(jnp.where(lane_f == idxs[kk], rank_mat, 0.0), axis=-1, keepdims=True)
        meta = jnp.where(lane == kk, idxs[kk], meta)
        meta = jnp.where(lane == TOP_K + kk, rk, meta)
        meta = jnp.where(lane == 2 * TOP_K + kk, gates[kk], meta)
    meta_ref[...] = meta
    cnt_ref[...] = jnp.broadcast_to(jnp.sum(onehot, axis=0, keepdims=True), cnt_ref.shape)


def _outproj_router(x2, y_lru, y_ret, w_out_bf16, ng, rw_pad, rb_pad, tri):
    t = x2.shape[0]
    tm = TM_PROJ
    const = lambda i: (0, 0)
    row = lambda i: (i, 0)
    return pl.pallas_call(
        _outproj_router_kernel,
        grid=(t // tm,),
        in_specs=[
            pl.BlockSpec((tm, D_MODEL), row),
            pl.BlockSpec((tm, D_LRU), row),
            pl.BlockSpec((tm, D_RET), row),
            pl.BlockSpec((D_MODEL, D_MODEL), const),
            pl.BlockSpec((1, D_MODEL), const),
            pl.BlockSpec((D_MODEL, LANES), const),
            pl.BlockSpec((1, LANES), const),
            pl.BlockSpec((tm, tm), const),
        ],
        out_specs=[
            pl.BlockSpec((tm, D_MODEL), row),
            pl.BlockSpec((tm, D_MODEL), row),
            pl.BlockSpec((tm, LANES), row),
            pl.BlockSpec((None, SUBLANES, LANES), lambda i: (i, 0, 0)),
        ],
        out_shape=[
            jax.ShapeDtypeStruct((t, D_MODEL), F32),
            jax.ShapeDtypeStruct((t, D_MODEL), BF16),
            jax.ShapeDtypeStruct((t, LANES), F32),
            jax.ShapeDtypeStruct((t // tm, SUBLANES, LANES), F32),
        ],
        compiler_params=_cparams(("arbitrary",)),
        name="outproj_router",
    )(x2, y_lru, y_ret, w_out_bf16, ng, rw_pad, rb_pad, tri)


def _local_positions(meta, lo8_row):
    lane_f = lax.broadcasted_iota(jnp.int32, meta.shape, 1).astype(F32)
    out = []
    for kk in range(TOP_K):
        idx = meta[:, kk:kk + 1]
        rank = meta[:, TOP_K + kk:TOP_K + kk + 1]
        lo = jnp.sum(jnp.where(lane_f == idx, lo8_row, 0.0), axis=-1, keepdims=True)
        out.append(lo + rank)
    return out


def _issue_pieces(copy, pieces, tile):
    blo_ref, bgo_ref, nbig_ref, slo_ref, sgo_ref, nsml_ref = pieces

    def big(p, c):
        q = tile * MAX_BIG + p
        copy(blo_ref[q], bgo_ref[q], BIG).start()
        return c

    lax.fori_loop(0, nbig_ref[tile], big, 0)

    def small(p, c):
        q = tile * MAX_SML + p
        copy(slo_ref[q], sgo_ref[q], SEG).start()
        return c

    lax.fori_loop(0, nsml_ref[tile], small, 0)


def _wait_rows(copy, nseg):
    nbig = lax.shift_right_logical(nseg, WAIT_SHIFT)

    def big(j, c):
        copy(WAITBIG).wait()
        return c

    lax.fori_loop(0, nbig, big, 0)

    def small(j, c):
        copy(SEG).wait()
        return c

    lax.fori_loop(0, nseg - lax.shift_left(nbig, WAIT_SHIFT), small, 0)


def _sort_kernel(blo_ref, bgo_ref, nbig_ref, slo_ref, sgo_ref, nsml_ref, nch_ref, pend_ref,
                 hf_ref, meta_ref, lo8v_ref, xs_hbm, sorted_ref, sems, *, min_blocks):
    pieces = (blo_ref, bgo_ref, nbig_ref, slo_ref, sgo_ref, nsml_ref)
    i = pl.program_id(0)
    nt = pl.num_programs(0)
    b = i % 2
    tm = hf_ref.shape[0]
    nb = xs_hbm.shape[0] // BM

    def seg_copy(buf, lo, go, rows):
        return pltpu.make_async_copy(
            sorted_ref.at[buf, pl.ds(pl.multiple_of(lo, SEG), rows)],
            xs_hbm.at[pl.ds(pl.multiple_of(go, SEG), rows)], sems.at[buf])

    def wait_tile(tile, buf):
        _wait_rows(functools.partial(seg_copy, buf, 0, 0), nch_ref[tile])

    @pl.when(i == 0)
    def _():
        sorted_ref[1, 0:BM, :] = jnp.zeros((BM, D_MODEL), F32)

        def zcopy(start):
            return pltpu.make_async_copy(
                sorted_ref.at[1, pl.ds(0, BM)],
                xs_hbm.at[pl.ds(pl.multiple_of(start, BM), BM)], sems.at[1])

        for e in range(N_EXPERTS):
            zcopy(jnp.maximum(pend_ref[e] - BM, 0)).start()
        for e in range(N_EXPERTS):
            zcopy(jnp.maximum(pend_ref[e] - BM, 0)).wait()
        for blk in range(min_blocks, nb):
            @pl.when(blk * BM >= pend_ref[N_EXPERTS - 1])
            def _():
                zcopy(blk * BM).start()
                zcopy(blk * BM).wait()

    @pl.when(i >= 2)
    def _():
        wait_tile(i - 2, b)

    lpos = _local_positions(meta_ref[...], lo8v_ref[...])
    lane = lax.broadcasted_iota(jnp.int32, (tm, LANES), 1)
    packed = jnp.zeros((tm, LANES), F32)
    for kk in range(TOP_K):
        packed = jnp.where(lane == kk, lpos[kk], packed)
    lpos_t = packed.T
    hfb = hf_ref[...]
    for j in range(RL // PROWS):
        r = (lax.broadcasted_iota(jnp.int32, (PROWS, tm), 0) + j * PROWS).astype(F32)
        perm = jnp.zeros((PROWS, tm), F32)
        for kk in range(TOP_K):
            perm = jnp.where(r == lpos_t[kk:kk + 1, :], 1.0, perm)
        sorted_ref[b, j * PROWS:(j + 1) * PROWS, :] = jnp.dot(
            perm.astype(BF16), hfb, preferred_element_type=F32)

    _issue_pieces(functools.partial(seg_copy, b), pieces, i)

    @pl.when(i == nt - 1)
    def _():
        @pl.when(i >= 1)
        def _():
            wait_tile(i - 1, 1 - b)
        wait_tile(i, b)


def _sort_dispatch(pieces, nch, pend, hf, meta, lo8v, n_pad_rows):
    t = hf.shape[0]
    tm = TM_PROJ
    row = lambda i, *_: (i, 0)
    return pl.pallas_call(
        functools.partial(_sort_kernel, min_blocks=t * TOP_K // BM),
        grid_spec=pltpu.PrefetchScalarGridSpec(
            num_scalar_prefetch=len(pieces) + 2,
            grid=(t // tm,),
            in_specs=[
                pl.BlockSpec((tm, D_MODEL), row),
                pl.BlockSpec((tm, LANES), row),
                pl.BlockSpec((None, 1, LANES), lambda i, *_: (i, 0, 0)),
            ],
            out_specs=pl.BlockSpec(memory_space=pl.ANY),
            scratch_shapes=[pltpu.VMEM((2, RL, D_MODEL), F32), pltpu.SemaphoreType.DMA((2,))],
        ),
        out_shape=jax.ShapeDtypeStruct((n_pad_rows, D_MODEL), F32),
        compiler_params=_cparams(("arbitrary",)),
        name="sort_dispatch",
    )(*pieces, nch, pend, hf, meta, lo8v)


def _ffn_kernel(be_ref, nv_ref, nxt_ref, x_ref, wg_hbm, bg_ref, wu_hbm, bu_ref, wd_hbm, bd_ref,
                o_ref, wstage_ref, wbf_ref, sems):
    i = pl.program_id(0)
    w_hbm = (wg_hbm, wu_hbm, wd_hbm)

    def fetch(e, m):
        return pltpu.make_async_copy(w_hbm[m].at[e], wstage_ref.at[m], sems.at[m])

    @pl.when(i < nv_ref[0])
    def _():
        @pl.when(i == 0)
        def _():
            for m in range(3):
                fetch(be_ref[0], m).start()

        @pl.when(jnp.logical_or(i == 0, be_ref[i] != be_ref[jnp.maximum(i - 1, 0)]))
        def _():
            for m in range(3):
                fetch(be_ref[i], m).wait()
                wbf_ref[m] = wstage_ref[m].astype(BF16)

            @pl.when(nxt_ref[i] >= 0)
            def _():
                for m in range(3):
                    fetch(nxt_ref[i], m).start()

        x = x_ref[...].astype(BF16)
        y = jnp.broadcast_to(bd_ref[...], o_ref.shape)
        for c in range(D_MODEL // FFN_COLS):
            cs = slice(c * FFN_COLS, (c + 1) * FFN_COLS)
            g = jnp.dot(x, wbf_ref[0, :, cs], preferred_element_type=F32) + bg_ref[:, cs]
            g = jnp.minimum(g, SWIGLU_LIMIT)
            u = jnp.dot(x, wbf_ref[1, :, cs], preferred_element_type=F32) + bu_ref[:, cs]
            u = jnp.clip(u, -SWIGLU_LIMIT, SWIGLU_LIMIT)
            act = g * jax.nn.sigmoid(SWIGLU_ALPHA * g) * (u + 1.0)
            y = y + jnp.dot(act.astype(BF16), wbf_ref[2, cs, :], preferred_element_type=F32)
        o_ref[...] = y

    @pl.when(pl.program_id(0) >= nv_ref[0])
    def _():
        o_ref[...] = jnp.zeros_like(o_ref)


def _expert_ffn(block_e, nvalid, next_e, xs, wg, bg, wu, bu, wd, bd):
    n_rows = xs.shape[0]
    nb = n_rows // BM

    def blk(i, be, nv, nx):
        return jnp.minimum(i, nv[0] - 1)

    xmap = lambda i, be, nv, nx: (blk(i, be, nv, nx), 0)
    bmap = lambda i, be, nv, nx: (be[blk(i, be, nv, nx)], 0, 0)
    hbm = pl.BlockSpec(memory_space=pl.ANY)
    return pl.pallas_call(
        _ffn_kernel,
        grid_spec=pltpu.PrefetchScalarGridSpec(
            num_scalar_prefetch=3,
            grid=(nb,),
            in_specs=[
                pl.BlockSpec((BM, D_MODEL), xmap),
                hbm,
                pl.BlockSpec((None, 1, D_MODEL), bmap),
                hbm,
                pl.BlockSpec((None, 1, D_MODEL), bmap),
                hbm,
                pl.BlockSpec((None, 1, D_MODEL), bmap),
            ],
            out_specs=pl.BlockSpec((BM, D_MODEL), lambda i, be, nv, nx: (i, 0)),
            scratch_shapes=[pltpu.VMEM((3, D_MODEL, D_MODEL), F32),
                            pltpu.VMEM((3, D_MODEL, D_MODEL), BF16),
                            pltpu.SemaphoreType.DMA((3,))],
        ),
        out_shape=jax.ShapeDtypeStruct((n_rows, D_MODEL), F32),
        compiler_params=_cparams(("arbitrary",)),
        name="expert_ffn",
    )(block_e, nvalid, next_e, xs, wg, bg, wu, bu, wd, bd)


def _combine_kernel(blo_ref, bgo_ref, nbig_ref, slo_ref, sgo_ref, nsml_ref, nch_ref,
                    x1_ref, meta_ref, lo8v_ref, g_ref, yb_hbm, o_ref, ybl_ref, sems):
    pieces = (blo_ref, bgo_ref, nbig_ref, slo_ref, sgo_ref, nsml_ref)
    i = pl.program_id(0)
    nt = pl.num_programs(0)
    b = i % 2
    tm = x1_ref.shape[0]

    def seg_copy(buf, lo, go, rows):
        return pltpu.make_async_copy(
            yb_hbm.at[pl.ds(pl.multiple_of(go, SEG), rows)],
            ybl_ref.at[buf, pl.ds(pl.multiple_of(lo, SEG), rows)], sems.at[buf])

    def issue_tile(tile, buf):
        _issue_pieces(functools.partial(seg_copy, buf), pieces, tile)

    def wait_tile(tile, buf):
        _wait_rows(functools.partial(seg_copy, buf, 0, 0), nch_ref[tile])

    @pl.when(i == 0)
    def _():
        ybl_ref[...] = jnp.zeros_like(ybl_ref)
        issue_tile(0, 0)

    @pl.when(i + 1 < nt)
    def _():
        issue_tile(i + 1, 1 - b)

    wait_tile(i, b)

    meta = meta_ref[...]
    lpos = _local_positions(meta, lo8v_ref[...])
    cols = []
    for j in range(RL // PCOLS):
        r = (lax.broadcasted_iota(jnp.int32, (tm, PCOLS), 1) + j * PCOLS).astype(F32)
        gmat = jnp.zeros((tm, PCOLS), F32)
        for kk in range(TOP_K):
            gmat = jnp.where(r == lpos[kk], meta[:, 2 * TOP_K + kk:2 * TOP_K + kk + 1], gmat)
        cols.append(gmat.astype(BF16))
    gates = jnp.concatenate(cols, axis=1)
    acc = x1_ref[...] + jnp.dot(gates, ybl_ref[b].astype(BF16), preferred_element_type=F32)
    ms = jnp.mean(acc * acc, axis=-1, keepdims=True)
    o_ref[...] = acc * lax.rsqrt(ms + NORM_EPS) * g_ref[...]


def _combine(pieces, nch, x1, meta, lo8v, final_g, yb):
    t = x1.shape[0]
    tm = TM_PROJ
    row = lambda i, *_: (i, 0)
    return pl.pallas_call(
        _combine_kernel,
        grid_spec=pltpu.PrefetchScalarGridSpec(
            num_scalar_prefetch=len(pieces) + 1,
            grid=(t // tm,),
            in_specs=[
                pl.BlockSpec((tm, D_MODEL), row),
                pl.BlockSpec((tm, LANES), row),
                pl.BlockSpec((None, 1, LANES), lambda i, *_: (i, 0, 0)),
                pl.BlockSpec((1, D_MODEL), lambda i, *_: (0, 0)),
                pl.BlockSpec(memory_space=pl.ANY),
            ],
            out_specs=pl.BlockSpec((tm, D_MODEL), row),
            scratch_shapes=[pltpu.VMEM((2, RL, D_MODEL), F32), pltpu.SemaphoreType.DMA((2,))],
        ),
        out_shape=jax.ShapeDtypeStruct((t, D_MODEL), F32),
        compiler_params=_cparams(("arbitrary",)),
        name="combine",
    )(*pieces, nch, x1, meta, lo8v, final_g, yb)


def _expand_pieces(cnt, lo, go, step, cap):
    n_e = cnt.shape[1]
    cum = jnp.cumsum(cnt, axis=1)
    p = jnp.arange(cap, dtype=jnp.int32)
    seg = jnp.minimum(jnp.sum(p[None, :, None] >= cum[:, None, :], axis=2), n_e - 1)
    j = p[None, :] - jnp.take_along_axis(cum - cnt, seg, axis=1)
    lo_p = jnp.take_along_axis(lo, seg, axis=1) + j * step
    go_p = jnp.take_along_axis(go, seg, axis=1) + j * step
    return (lo_p.astype(jnp.int32).reshape(-1), go_p.astype(jnp.int32).reshape(-1),
            cum[:, -1].astype(jnp.int32))


def _piece_lists(nseg, lo8, goff):
    nbig = nseg >> BIG_SHIFT
    nsml = nseg - (nbig << BIG_SHIFT)
    done = nbig * BIG
    return (_expand_pieces(nbig, lo8, goff, BIG, MAX_BIG)
            + _expand_pieces(nsml, lo8 + done, goff + done, SEG, MAX_SML))


def _block_diag(w):
    h, d, _ = w.shape
    eye = jnp.eye(h, dtype=w.dtype)
    return (eye[:, None, :, None] * w[:, :, None, :]).reshape(h * d, h * d)


def kernel(x, positions, attn_norm_g, w_in, conv_w, conv_b, lru_wa, lru_ba, lru_wx, lru_bx, lru_lambda, lru_norm_g, ret_norm_g, ret_norm_b, w_out, ffn_norm_g, router_w, router_b, moe_w_gate, moe_b_gate, moe_w_up, moe_b_up, moe_w_down, moe_b_down, final_norm_g):
    bsz, seq, d = x.shape
    depth = w_in.shape[0]
    t = bsz * seq
    assert depth == 1 and d == D_MODEL and seq % TC_SEQ == 0 and t % TM_PROJ == 0
    nt = t // TM_PROJ
    n_pad = t * TOP_K + nt * N_EXPERTS * SEG + N_EXPERTS * BM
    nb = n_pad // BM

    half = RET_HEAD_DIM // 2
    inv_freq = ROPE_THETA ** (-jnp.arange(half, dtype=F32) / half)
    inv_freq = jnp.concatenate([inv_freq, inv_freq]).reshape(1, RET_HEAD_DIM)
    sign = jnp.concatenate([-jnp.ones((half,), F32), jnp.ones((half,), F32)]).reshape(1, RET_HEAD_DIM)
    pos_col = positions.reshape(t, 1).astype(jnp.int32)
    tri = jnp.tril(jnp.ones((TM_PROJ, TM_PROJ), F32), -1).astype(BF16)

    x2 = x.reshape(t, d)
    for l in range(depth):
        proj = _inproj(x2, attn_norm_g[l].reshape(1, d), w_in[l].astype(BF16))

        wg_bd = jnp.concatenate([_block_diag(lru_wa[l]), _block_diag(lru_wx[l])], axis=1).astype(BF16)
        lam = lru_lambda[l].astype(F32)
        sp = (jnp.maximum(-lam, 0.0) + jnp.log1p(jnp.exp(-jnp.abs(lam)))).reshape(1, D_LRU)
        y_lru = _lru_branch(proj, bsz, seq, conv_w[l], conv_b[l].reshape(1, D_LRU), wg_bd,
                            lru_ba[l].reshape(1, D_LRU), lru_bx[l].reshape(1, D_LRU), sp,
                            lru_norm_g[l].reshape(1, D_LRU))
        y_ret = _ret_branch(proj, pos_col, bsz, seq, inv_freq, sign,
                            ret_norm_g[l].reshape(1, D_RET), ret_norm_b[l].reshape(1, D_RET))

        rw_pad = jnp.zeros((d, LANES), F32).at[:, :N_EXPERTS].set(router_w[l]).astype(BF16)
        rb_pad = jnp.full((1, LANES), -1e30, F32).at[0, :N_EXPERTS].set(router_b[l])
        x1, hf, meta, cnt_tile = _outproj_router(x2, y_lru, y_ret, w_out[l].astype(BF16),
                                            ffn_norm_g[l].reshape(1, d), rw_pad, rb_pad, tri)

        cnt = cnt_tile[:, 0, :N_EXPERTS].astype(jnp.int32)
        c8 = (cnt + SEG - 1) // SEG * SEG
        lo8 = jnp.cumsum(c8, axis=1) - c8
        padded = (jnp.sum(c8, axis=0) + BM - 1) // BM * BM
        pend = jnp.cumsum(padded).astype(jnp.int32)
        pstart = pend - padded
        goff = (pstart[None, :] + jnp.cumsum(c8, axis=0) - c8).astype(jnp.int32)
        nseg = (c8 // SEG).astype(jnp.int32)
        nch = jnp.sum(nseg, axis=1).astype(jnp.int32)
        lo8v = jnp.zeros((nt, 1, LANES), F32).at[:, 0, :N_EXPERTS].set(lo8.astype(F32))
        block_start = jnp.arange(nb, dtype=jnp.int32) * BM
        block_e = jnp.minimum(jnp.sum(block_start[:, None] >= pend[None, :], axis=1),
                              N_EXPERTS - 1).astype(jnp.int32)
        nvalid = (pend[N_EXPERTS - 1:] // BM).astype(jnp.int32)
        pieces = _piece_lists(nseg, lo8.astype(jnp.int32), goff)

        xs = _sort_dispatch(pieces, nch, pend, hf, meta, lo8v, n_pad)
        next_blk = pend[block_e] // BM
        next_e = jnp.where(next_blk < nvalid[0], block_e[jnp.minimum(next_blk, nb - 1)],
                           -1).astype(jnp.int32)
        yb = _expert_ffn(block_e, nvalid, next_e, xs,
                         moe_w_gate[l], moe_b_gate[l].reshape(N_EXPERTS, 1, d),
                         moe_w_up[l], moe_b_up[l].reshape(N_EXPERTS, 1, d),
                         moe_w_down[l], moe_b_down[l].reshape(N_EXPERTS, 1, d))
        x2 = _combine(pieces, nch, x1, meta, lo8v, final_norm_g.reshape(1, d), yb)
    return x2.reshape(bsz, seq, d)
```

```python
import functools
import math

import numpy as np
import jax
import jax.numpy as jnp
from jax import lax
from jax.experimental import pallas as pl
from jax.experimental.pallas import tpu as pltpu

F32 = jnp.float32
BF16 = jnp.bfloat16

D_MODEL = 1024
D_LRU = 512
D_RET = 512
LRU_HEADS = 8
LRU_HEAD_DIM = D_LRU // LRU_HEADS
CONV_WIDTH = 4
LRU_C = 8.0
RET_HEADS = 4
RET_HEAD_DIM = D_RET // RET_HEADS
ROPE_THETA = 10000.0
D_IN = 2 * D_LRU + 4 * D_RET
N_EXPERTS = 32
TOP_K = 4
SWIGLU_ALPHA = 1.702
SWIGLU_LIMIT = 7.0
NORM_EPS = 1e-5

LANES = 128
SUBLANES = 8
VMEM_LIMIT = 48 * 1024 * 1024

TM_PROJ = 512
TC_SEQ = 256
TC_MIX = 512
BM = 512
FFN_COLS = 512
SEG = SUBLANES
RL = TM_PROJ * TOP_K + N_EXPERTS * SEG
PCOLS = 256
PROWS = 768
BIG_SHIFT = 2
BIG = SEG << BIG_SHIFT
MAX_BIG = RL // BIG
MAX_SML = N_EXPERTS * ((1 << BIG_SHIFT) - 1)
WAIT_SHIFT = 5
WAITBIG = SEG << WAIT_SHIFT

LOG_GAMMA = [math.log1p(-(2.0 ** (-5.0 - h))) for h in range(RET_HEADS)]


def _cparams(sem):
    return pltpu.CompilerParams(dimension_semantics=sem, vmem_limit_bytes=VMEM_LIMIT)


def _inproj_kernel(x_ref, g_ref, w_ref, o_ref):
    x = x_ref[...]
    ms = jnp.mean(x * x, axis=-1, keepdims=True)
    h = x * lax.rsqrt(ms + NORM_EPS) * g_ref[...]
    o_ref[...] = jnp.dot(h.astype(BF16), w_ref[...],
                         preferred_element_type=F32).astype(o_ref.dtype)


def _inproj(x2, g, w_in_bf16):
    t = x2.shape[0]
    return pl.pallas_call(
        _inproj_kernel,
        grid=(t // TM_PROJ,),
        in_specs=[
            pl.BlockSpec((TM_PROJ, D_MODEL), lambda i: (i, 0)),
            pl.BlockSpec((1, D_MODEL), lambda i: (0, 0)),
            pl.BlockSpec((D_MODEL, D_IN), lambda i: (0, 0)),
        ],
        out_specs=pl.BlockSpec((TM_PROJ, D_IN), lambda i: (i, 0)),
        out_shape=jax.ShapeDtypeStruct((t, D_IN), BF16),
        compiler_params=_cparams(("arbitrary",)),
        name="inproj",
    )(x2, g, w_in_bf16)


def _lru_kernel(x_ref, gate_ref, cw_ref, cb_ref, wg_ref, ba_ref, bx_ref, sp_ref, ng_ref,
                o_ref, xext_ref, h_ref):
    s = pl.program_id(1)
    tc = x_ref.shape[0]

    @pl.when(s == 0)
    def _():
        xext_ref[0:SUBLANES, :] = jnp.zeros((SUBLANES, D_LRU), F32)
        h_ref[...] = jnp.zeros_like(h_ref)

    xext_ref[SUBLANES:SUBLANES + tc, :] = x_ref[...].astype(F32)
    xc = cb_ref[...] + cw_ref[CONV_WIDTH - 1:CONV_WIDTH, :] * xext_ref[SUBLANES:SUBLANES + tc, :]
    for j in range(CONV_WIDTH - 1):
        off = SUBLANES - (CONV_WIDTH - 1) + j
        xc = xc + cw_ref[j:j + 1, :] * xext_ref[off:off + tc, :]
    xext_ref[0:SUBLANES, :] = xext_ref[tc:tc + SUBLANES, :]

    gates = jnp.dot(xc.astype(BF16), wg_ref[...], preferred_element_type=F32)
    r = jax.nn.sigmoid(gates[:, :D_LRU] + ba_ref[...])
    ig = jax.nn.sigmoid(gates[:, D_LRU:] + bx_ref[...])
    log_a = (-LRU_C) * r * sp_ref[...]
    a = jnp.exp(log_a)
    b = jnp.sqrt(1.0 - a * a) * (ig * xc)

    ng = tc // SUBLANES
    a = a.reshape(ng, SUBLANES, D_LRU)
    b = b.reshape(ng, SUBLANES, D_LRU)
    in_group = lax.broadcasted_iota(jnp.int32, (1, SUBLANES, 1), 1)
    d = 1
    while d < SUBLANES:
        a_s = pltpu.roll(a, d, 1)
        b_s = pltpu.roll(b, d, 1)
        m = in_group >= d
        b = jnp.where(m, a * b_s + b, b)
        a = jnp.where(m, a * a_s, a)
        d *= 2
    h_prev = h_ref[0:1, :]
    groups = []
    for g in range(ng):
        hg = b[g] + a[g] * h_prev
        groups.append(hg)
        h_prev = hg[SUBLANES - 1:SUBLANES, :]
    h = jnp.concatenate(groups, axis=0)
    h_ref[0:1, :] = h_prev

    y = h * jax.nn.gelu(gate_ref[...].astype(F32))
    ms = jnp.mean(y * y, axis=-1, keepdims=True)
    o_ref[...] = (y * lax.rsqrt(ms + NORM_EPS) * ng_ref[...]).astype(o_ref.dtype)


def _lru_branch(proj, bsz, seq, conv_w, conv_b, wg_bd, ba, bx, softplus_neg_lam, norm_g):
    ns = seq // TC_SEQ
    row = lambda b, s: (b * ns + s, 0)
    const = lambda b, s: (0, 0)
    return pl.pallas_call(
        _lru_kernel,
        grid=(bsz, ns),
        in_specs=[
            pl.BlockSpec((TC_SEQ, D_LRU), lambda b, s: (b * ns + s, 0)),
            pl.BlockSpec((TC_SEQ, D_LRU), lambda b, s: (b * ns + s, 1)),
            pl.BlockSpec((CONV_WIDTH, D_LRU), const),
            pl.BlockSpec((1, D_LRU), const),
            pl.BlockSpec((D_LRU, 2 * D_LRU), const),
            pl.BlockSpec((1, D_LRU), const),
            pl.BlockSpec((1, D_LRU), const),
            pl.BlockSpec((1, D_LRU), const),
            pl.BlockSpec((1, D_LRU), const),
        ],
        out_specs=pl.BlockSpec((TC_SEQ, D_LRU), row),
        out_shape=jax.ShapeDtypeStruct((bsz * seq, D_LRU), BF16),
        scratch_shapes=[
            pltpu.VMEM((TC_SEQ + SUBLANES, D_LRU), F32),
            pltpu.VMEM((SUBLANES, D_LRU), F32),
        ],
        compiler_params=_cparams(("arbitrary", "arbitrary")),
        name="lru_branch",
    )(proj, proj, conv_w, conv_b, wg_bd, ba, bx, softplus_neg_lam, norm_g)


def _ret_kernel(pos_ref, invf_ref, sgn_ref, q_ref, k_ref, v_ref, g_ref, gg_ref, gb_ref,
                o_ref, state_ref, decay_ref):
    s = pl.program_id(1)
    tc = q_ref.shape[0]
    dh = RET_HEAD_DIM

    @pl.when(jnp.logical_and(pl.program_id(0) == 0, s == 0))
    def _():
        ri = lax.broadcasted_iota(jnp.int32, (tc, tc), 0)
        ci = lax.broadcasted_iota(jnp.int32, (tc, tc), 1)
        rel = (ri - ci).astype(F32)
        causal = rel >= 0.0
        relc = jnp.where(causal, rel, 0.0)
        for h in range(RET_HEADS):
            decay_ref[h] = jnp.where(causal, jnp.exp(LOG_GAMMA[h] * relc), 0.0)

    @pl.when(s == 0)
    def _():
        state_ref[...] = jnp.zeros_like(state_ref)

    hr = tc // 2
    pos = pos_ref[...].astype(F32)
    first = lax.broadcasted_iota(jnp.int32, (hr, dh), 1) < dh // 2
    ang = jnp.where(first, pos[0:hr, :], pos[hr:tc, :]) * invf_ref[...]
    c2 = jnp.cos(ang)
    s2 = jnp.sin(ang)
    c2r = pltpu.roll(c2, dh // 2, 1)
    s2r = pltpu.roll(s2, dh // 2, 1)
    cos = jnp.concatenate([jnp.where(first, c2, c2r), jnp.where(first, c2r, c2)], axis=0)
    sin = jnp.concatenate([jnp.where(first, s2, s2r), jnp.where(first, s2r, s2)], axis=0)
    sin_signed = sin * sgn_ref[...]
    rowf = lax.broadcasted_iota(jnp.int32, (tc, 1), 0).astype(F32)
    scale = dh ** -0.5

    for h in range(RET_HEADS):
        sl = slice(h * dh, (h + 1) * dh)
        q = q_ref[:, sl].astype(F32)
        k = k_ref[:, sl].astype(F32)
        v = v_ref[:, sl]
        qr = q * cos + pltpu.roll(q, dh // 2, 1) * sin_signed
        kr = (k * cos + pltpu.roll(k, dh // 2, 1) * sin_signed) * scale
        qb = qr.astype(BF16)
        kb = kr.astype(BF16)
        lg = LOG_GAMMA[h]
        scores = lax.dot_general(qb, kb, (((1,), (1,)), ((), ())),
                                 preferred_element_type=F32) * decay_ref[h]
        intra = jnp.dot(scores.astype(BF16), v, preferred_element_type=F32)
        st = state_ref[h]
        xi = jnp.exp(lg * (rowf + 1.0))
        cross = jnp.dot(qb, st.astype(BF16), preferred_element_type=F32) * xi
        o = intra + cross
        zeta = jnp.exp(lg * (float(tc) - 1.0 - rowf))
        kz = (kr * zeta).astype(BF16)
        kv = lax.dot_general(kz, v, (((0,), (0,)), ((), ())), preferred_element_type=F32)
        state_ref[h] = math.exp(lg * tc) * st + kv

        mu = jnp.mean(o, axis=-1, keepdims=True)
        oc = o - mu
        var = jnp.mean(oc * oc, axis=-1, keepdims=True)
        on = oc * lax.rsqrt(var + NORM_EPS) * gg_ref[:, sl] + gb_ref[:, sl]
        gate = g_ref[:, sl].astype(F32)
        o_ref[:, sl] = (gate * jax.nn.sigmoid(gate) * on).astype(o_ref.dtype)


def _ret_branch(proj, pos_col, bsz, seq, inv_freq, sign, gn_g, gn_b):
    ns = seq // TC_SEQ
    const = lambda b, s: (0, 0)
    col = lambda c: (lambda b, s: (b * ns + s, c))
    return pl.pallas_call(
        _ret_kernel,
        grid=(bsz, ns),
        in_specs=[
            pl.BlockSpec((TC_SEQ, 1), lambda b, s: (b * ns + s, 0)),
            pl.BlockSpec((1, RET_HEAD_DIM), const),
            pl.BlockSpec((1, RET_HEAD_DIM), const),
            pl.BlockSpec((TC_SEQ, D_RET), col(2)),
            pl.BlockSpec((TC_SEQ, D_RET), col(3)),
            pl.BlockSpec((TC_SEQ, D_RET), col(4)),
            pl.BlockSpec((TC_SEQ, D_RET), col(5)),
            pl.BlockSpec((1, D_RET), const),
            pl.BlockSpec((1, D_RET), const),
        ],
        out_specs=pl.BlockSpec((TC_SEQ, D_RET), lambda b, s: (b * ns + s, 0)),
        out_shape=jax.ShapeDtypeStruct((bsz * seq, D_RET), BF16),
        scratch_shapes=[
            pltpu.VMEM((RET_HEADS, RET_HEAD_DIM, RET_HEAD_DIM), F32),
            pltpu.VMEM((RET_HEADS, TC_SEQ, TC_SEQ), F32),
        ],
        compiler_params=_cparams(("arbitrary", "arbitrary")),
        name="ret_branch",
    )(pos_col, inv_freq, sign, proj, proj, proj, proj, gn_g, gn_b)


def _mixer_kernel(x_ref, pos_ref, ng_ref, win_ref, cw_ref, cb_ref, wg_ref, ba_ref, bx_ref,
                  sp_ref, lng_ref, invf_ref, sgn_ref, gg_ref, gb_ref,
                  ylru_ref, yret_ref,
                  proj_ref, xext_ref, h_ref, state_ref, decay_ref, *, ns):
    i = pl.program_id(0)
    tc = x_ref.shape[0]
    dh = RET_HEAD_DIM

    @pl.when(i == 0)
    def _():
        proj_ref[1] = jnp.zeros((tc, D_IN), BF16)
        ri = lax.broadcasted_iota(jnp.int32, (tc, tc), 0)
        ci = lax.broadcasted_iota(jnp.int32, (tc, tc), 1)
        rel = (ri - ci).astype(F32)
        causal = rel >= 0.0
        relc = jnp.where(causal, rel, 0.0)
        for h in range(RET_HEADS):
            decay_ref[h] = jnp.where(causal, jnp.exp(LOG_GAMMA[h] * relc), 0.0)

    @pl.when(lax.rem(jnp.maximum(i - 1, 0), ns) == 0)
    def _():
        xext_ref[0:SUBLANES, :] = jnp.zeros((SUBLANES, D_LRU), F32)
        h_ref[...] = jnp.zeros_like(h_ref)
        state_ref[...] = jnp.zeros_like(state_ref)

    x = x_ref[...]
    ms = jnp.mean(x * x, axis=-1, keepdims=True)
    hn = x * lax.rsqrt(ms + NORM_EPS) * ng_ref[...]
    proj_ref[i % 2] = jnp.dot(hn.astype(BF16), win_ref[...],
                              preferred_element_type=F32).astype(BF16)

    slot = (i + 1) % 2

    def pcol(lo, width):
        return proj_ref[slot, :, lo:lo + width]

    xext_ref[SUBLANES:SUBLANES + tc, :] = pcol(0, D_LRU).astype(F32)
    xc = cb_ref[...] + cw_ref[CONV_WIDTH - 1:CONV_WIDTH, :] * xext_ref[SUBLANES:SUBLANES + tc, :]
    for j in range(CONV_WIDTH - 1):
        off = SUBLANES - (CONV_WIDTH - 1) + j
        xc = xc + cw_ref[j:j + 1, :] * xext_ref[off:off + tc, :]
    xext_ref[0:SUBLANES, :] = xext_ref[tc:tc + SUBLANES, :]

    gates = jnp.dot(xc.astype(BF16), wg_ref[...], preferred_element_type=F32)
    r = jax.nn.sigmoid(gates[:, :D_LRU] + ba_ref[...])
    ig = jax.nn.sigmoid(gates[:, D_LRU:] + bx_ref[...])
    a = jnp.exp((-LRU_C) * r * sp_ref[...])
    b = jnp.sqrt(1.0 - a * a) * (ig * xc)

    ng = tc // SUBLANES
    a = a.reshape(ng, SUBLANES, D_LRU)
    b = b.reshape(ng, SUBLANES, D_LRU)
    in_group = lax.broadcasted_iota(jnp.int32, (1, SUBLANES, 1), 1)
    d = 1
    while d < SUBLANES:
        a_s = pltpu.roll(a, d, 1)
        b_s = pltpu.roll(b, d, 1)
        m = in_group >= d
        b = jnp.where(m, a * b_s + b, b)
        a = jnp.where(m, a * a_s, a)
        d *= 2
    h_prev = h_ref[0:1, :]
    groups = []
    for g in range(ng):
        hg = b[g] + a[g] * h_prev
        groups.append(hg)
        h_prev = hg[SUBLANES - 1:SUBLANES, :]
    hseq = jnp.concatenate(groups, axis=0)
    h_ref[0:1, :] = h_prev

    y = hseq * jax.nn.gelu(pcol(D_LRU, D_LRU).astype(F32))
    ms = jnp.mean(y * y, axis=-1, keepdims=True)
    ylru_ref[...] = (y * lax.rsqrt(ms + NORM_EPS) * lng_ref[...]).astype(ylru_ref.dtype)

    hr = tc // 2
    pos = pos_ref[...].astype(F32)
    first = lax.broadcasted_iota(jnp.int32, (hr, dh), 1) < dh // 2
    ang = jnp.where(first, pos[0:hr, :], pos[hr:tc, :]) * invf_ref[...]
    c2 = jnp.cos(ang)
    s2 = jnp.sin(ang)
    c2r = pltpu.roll(c2, dh // 2, 1)
    s2r = pltpu.roll(s2, dh // 2, 1)
    cos = jnp.concatenate([jnp.where(first, c2, c2r), jnp.where(first, c2r, c2)], axis=0)
    sin = jnp.concatenate([jnp.where(first, s2, s2r), jnp.where(first, s2r, s2)], axis=0)
    sin_signed = sin * sgn_ref[...]
    rowf = lax.broadcasted_iota(jnp.int32, (tc, 1), 0).astype(F32)
    scale = dh ** -0.5
    q0, k0, v0, g0 = 2 * D_LRU, 2 * D_LRU + D_RET, 2 * D_LRU + 2 * D_RET, 2 * D_LRU + 3 * D_RET

    for h in range(RET_HEADS):
        sl = slice(h * dh, (h + 1) * dh)
        q = pcol(q0 + h * dh, dh).astype(F32)
        k = pcol(k0 + h * dh, dh).astype(F32)
        v = pcol(v0 + h * dh, dh)
        qr = q * cos + pltpu.roll(q, dh // 2, 1) * sin_signed
        kr = (k * cos + pltpu.roll(k, dh // 2, 1) * sin_signed) * scale
        qb = qr.astype(BF16)
        kb = kr.astype(BF16)
        lg = LOG_GAMMA[h]
        scores = lax.dot_general(qb, kb, (((1,), (1,)), ((), ())),
                                 preferred_element_type=F32) * decay_ref[h]
        intra = jnp.dot(scores.astype(BF16), v, preferred_element_type=F32)
        st = state_ref[h]
        xi = jnp.exp(lg * (rowf + 1.0))
        cross = jnp.dot(qb, st.astype(BF16), preferred_element_type=F32) * xi
        o = intra + cross
        zeta = jnp.exp(lg * (float(tc) - 1.0 - rowf))
        kz = (kr * zeta).astype(BF16)
        kv = lax.dot_general(kz, v, (((0,), (0,)), ((), ())), preferred_element_type=F32)
        state_ref[h] = math.exp(lg * tc) * st + kv

        mu = jnp.mean(o, axis=-1, keepdims=True)
        oc = o - mu
        var = jnp.mean(oc * oc, axis=-1, keepdims=True)
        on = oc * lax.rsqrt(var + NORM_EPS) * gg_ref[:, sl] + gb_ref[:, sl]
        gate = pcol(g0 + h * dh, dh).astype(F32)
        yret_ref[:, sl] = (gate * jax.nn.sigmoid(gate) * on).astype(yret_ref.dtype)


def _mixer(x2, pos_col, seq, norm_g, w_in_bf16, conv_w, conv_b, wg_bd, ba, bx, sp, lru_norm_g,
           inv_freq, sign, gn_g, gn_b):
    t = x2.shape[0]
    tc = TC_MIX
    n_tiles = t // tc
    const = lambda i: (0, 0)
    cur = lambda i: (jnp.minimum(i, n_tiles - 1), 0)
    prev = lambda i: (jnp.maximum(i - 1, 0), 0)
    vec = lambda n: pl.BlockSpec((1, n), const)
    return pl.pallas_call(
        functools.partial(_mixer_kernel, ns=seq // tc),
        grid=(n_tiles + 1,),
        in_specs=[
            pl.BlockSpec((tc, D_MODEL), cur),
            pl.BlockSpec((tc, 1), prev),
            vec(D_MODEL),
            pl.BlockSpec((D_MODEL, D_IN), const),
            pl.BlockSpec((CONV_WIDTH, D_LRU), const),
            vec(D_LRU),
            pl.BlockSpec((D_LRU, 2 * D_LRU), const),
            vec(D_LRU), vec(D_LRU), vec(D_LRU), vec(D_LRU),
            vec(RET_HEAD_DIM), vec(RET_HEAD_DIM),
            vec(D_RET), vec(D_RET),
        ],
        out_specs=[pl.BlockSpec((tc, D_LRU), prev), pl.BlockSpec((tc, D_RET), prev)],
        out_shape=[jax.ShapeDtypeStruct((t, D_LRU), BF16), jax.ShapeDtypeStruct((t, D_RET), BF16)],
        scratch_shapes=[
            pltpu.VMEM((2, tc, D_IN), BF16),
            pltpu.VMEM((tc + SUBLANES, D_LRU), F32),
            pltpu.VMEM((SUBLANES, D_LRU), F32),
            pltpu.VMEM((RET_HEADS, RET_HEAD_DIM, RET_HEAD_DIM), F32),
            pltpu.VMEM((RET_HEADS, tc, tc), F32),
        ],
        compiler_params=_cparams(("arbitrary",)),
        name="mixer",
    )(x2, pos_col, norm_g, w_in_bf16, conv_w, conv_b, wg_bd, ba, bx, sp, lru_norm_g,
      inv_freq, sign, gn_g, gn_b)


def _outproj_router_kernel(x_ref, yl_ref, yr_ref, wo_ref, ng_ref, rw_ref, rb_ref, tri_ref,
                           x1_ref, hf_ref, meta_ref, cnt_ref):
    tm = x_ref.shape[0]

    y = jnp.dot(yl_ref[...], wo_ref[0:D_LRU, :], preferred_element_type=F32)
    y = y + jnp.dot(yr_ref[...], wo_ref[D_LRU:, :], preferred_element_type=F32)
    x1 = x_ref[...] + y
    x1_ref[...] = x1
    ms = jnp.mean(x1 * x1, axis=-1, keepdims=True)
    hf = x1 * lax.rsqrt(ms + NORM_EPS) * ng_ref[...]
    hf_ref[...] = hf.astype(hf_ref.dtype)

    logits = jnp.dot(hf.astype(BF16), rw_ref[...], preferred_element_type=F32) + rb_ref[...]
    lane = lax.broadcasted_iota(jnp.int32, (tm, LANES), 1)
    lane_f = lane.astype(F32)
    work = logits
    vals, idxs = [], []
    onehot = jnp.zeros((tm, LANES), F32)
    for _ in range(TOP_K):
        m = jnp.max(work, axis=-1, keepdims=True)
        idx = jnp.min(jnp.where(work == m, lane_f, float(LANES)), axis=-1, keepdims=True)
        sel = lane_f == idx
        work = jnp.where(sel, -jnp.inf, work)
        onehot = jnp.where(sel, 1.0, onehot)
        vals.append(m)
        idxs.append(idx)
    exps = [jnp.exp(v - vals[0]) for v in vals]
    denom = exps[0] + exps[1] + exps[2] + exps[3]
    gates = [e / denom for e in exps]

    rank_mat = jnp.dot(tri_ref[...], onehot.astype(BF16), preferred_element_type=F32)
    meta = jnp.zeros((tm, LANES), F32)
    for kk in range(TOP_K):
        rk = jnp.sum(jnp.where(lane_f == idxs[kk], rank_mat, 0.0), axis=-1, keepdims=True)
        meta = jnp.where(lane == kk, idxs[kk], meta)
        meta = jnp.where(lane == TOP_K + kk, rk, meta)
        meta = jnp.where(lane == 2 * TOP_K + kk, gates[kk], meta)
    meta_ref[...] = meta
    cnt_ref[...] = jnp.broadcast_to(jnp.sum(onehot, axis=0, keepdims=True), cnt_ref.shape)


def _outproj_router(x2, y_lru, y_ret, w_out_bf16, ng, rw_pad, rb_pad, tri):
    t = x2.shape[0]
    tm = TM_PROJ
    const = lambda i: (0, 0)
    row = lambda i: (i, 0)
    return pl.pallas_call(
        _outproj_router_kernel,
        grid=(t // tm,),
        in_specs=[
            pl.BlockSpec((tm, D_MODEL), row),
            pl.BlockSpec((tm, D_LRU), row),
            pl.BlockSpec((tm, D_RET), row),
            pl.BlockSpec((D_MODEL, D_MODEL), const),
            pl.BlockSpec((1, D_MODEL), const),
            pl.BlockSpec((D_MODEL, LANES), const),
            pl.BlockSpec((1, LANES), const),
            pl.BlockSpec((tm, tm), const),
        ],
        out_specs=[
            pl.BlockSpec((tm, D_MODEL), row),
            pl.BlockSpec((tm, D_MODEL), row),
            pl.BlockSpec((tm, LANES), row),
            pl.BlockSpec((None, SUBLANES, LANES), lambda i: (i, 0, 0)),
        ],
        out_shape=[
            jax.ShapeDtypeStruct((t, D_MODEL), F32),
            jax.ShapeDtypeStruct((t, D_MODEL), BF16),
            jax.ShapeDtypeStruct((t, LANES), F32),
            jax.ShapeDtypeStruct((t // tm, SUBLANES, LANES), F32),
        ],
        compiler_params=_cparams(("arbitrary",)),
        name="outproj_router",
    )(x2, y_lru, y_ret, w_out_bf16, ng, rw_pad, rb_pad, tri)


def _local_positions(meta, lo8_row):
    lane_f = lax.broadcasted_iota(jnp.int32, meta.shape, 1).astype(F32)
    out = []
    for kk in range(TOP_K):
        idx = meta[:, kk:kk + 1]
        rank = meta[:, TOP_K + kk:TOP_K + kk + 1]
        lo = jnp.sum(jnp.where(lane_f == idx, lo8_row, 0.0), axis=-1, keepdims=True)
        out.append(lo + rank)
    return out


def _issue_pieces(copy, pieces, tile):
    blo_ref, bgo_ref, nbig_ref, slo_ref, sgo_ref, nsml_ref = pieces

    def big(p, c):
        q = tile * MAX_BIG + p
        copy(blo_ref[q], bgo_ref[q], BIG).start()
        return c

    lax.fori_loop(0, nbig_ref[tile], big, 0)

    def small(p, c):
        q = tile * MAX_SML + p
        copy(slo_ref[q], sgo_ref[q], SEG).start()
        return c

    lax.fori_loop(0, nsml_ref[tile], small, 0)


def _wait_rows(copy, nseg):
    nbig = lax.shift_right_logical(nseg, WAIT_SHIFT)

    def big(j, c):
        copy(WAITBIG).wait()
        return c

    lax.fori_loop(0, nbig, big, 0)

    def small(j, c):
        copy(SEG).wait()
        return c

    lax.fori_loop(0, nseg - lax.shift_left(nbig, WAIT_SHIFT), small, 0)


def _sort_kernel(blo_ref, bgo_ref, nbig_ref, slo_ref, sgo_ref, nsml_ref, nch_ref, pend_ref,
                 hf_ref, meta_ref, lo8v_ref, xs_hbm, sorted_ref, sems, *, min_blocks):
    pieces = (blo_ref, bgo_ref, nbig_ref, slo_ref, sgo_ref, nsml_ref)
    i = pl.program_id(0)
    nt = pl.num_programs(0)
    b = i % 2
    tm = hf_ref.shape[0]
    nb = xs_hbm.shape[0] // BM

    def seg_copy(buf, lo, go, rows):
        return pltpu.make_async_copy(
            sorted_ref.at[buf, pl.ds(pl.multiple_of(lo, SEG), rows)],
            xs_hbm.at[pl.ds(pl.multiple_of(go, SEG), rows)], sems.at[buf])

    def wait_tile(tile, buf):
        _wait_rows(functools.partial(seg_copy, buf, 0, 0), nch_ref[tile])

    @pl.when(i == 0)
    def _():
        sorted_ref[1, 0:BM, :] = jnp.zeros((BM, D_MODEL), F32)

        def zcopy(start):
            return pltpu.make_async_copy(
                sorted_ref.at[1, pl.ds(0, BM)],
                xs_hbm.at[pl.ds(pl.multiple_of(start, BM), BM)], sems.at[1])

        for e in range(N_EXPERTS):
            zcopy(jnp.maximum(pend_ref[e] - BM, 0)).start()
        for e in range(N_EXPERTS):
            zcopy(jnp.maximum(pend_ref[e] - BM, 0)).wait()
        for blk in range(min_blocks, nb):
            @pl.when(blk * BM >= pend_ref[N_EXPERTS - 1])
            def _():
                zcopy(blk * BM).start()
                zcopy(blk * BM).wait()

    @pl.when(i >= 2)
    def _():
        wait_tile(i - 2, b)

    lpos = _local_positions(meta_ref[...], lo8v_ref[...])
    lane = lax.broadcasted_iota(jnp.int32, (tm, LANES), 1)
    packed = jnp.zeros((tm, LANES), F32)
    for kk in range(TOP_K):
        packed = jnp.where(lane == kk, lpos[kk], packed)
    lpos_t = packed.T
    hfb = hf_ref[...]
    for j in range(RL // PROWS):
        r = (lax.broadcasted_iota(jnp.int32, (PROWS, tm), 0) + j * PROWS).astype(F32)
        perm = jnp.zeros((PROWS, tm), F32)
        for kk in range(TOP_K):
            perm = jnp.where(r == lpos_t[kk:kk + 1, :], 1.0, perm)
        sorted_ref[b, j * PROWS:(j + 1) * PROWS, :] = jnp.dot(
            perm.astype(BF16), hfb, preferred_element_type=F32)

    _issue_pieces(functools.partial(seg_copy, b), pieces, i)

    @pl.when(i == nt - 1)
    def _():
        @pl.when(i >= 1)
        def _():
            wait_tile(i - 1, 1 - b)
        wait_tile(i, b)


def _sort_dispatch(pieces, nch, pend, hf, meta, lo8v, n_pad_rows):
    t = hf.shape[0]
    tm = TM_PROJ
    row = lambda i, *_: (i, 0)
    return pl.pallas_call(
        functools.partial(_sort_kernel, min_blocks=t * TOP_K // BM),
        grid_spec=pltpu.PrefetchScalarGridSpec(
            num_scalar_prefetch=len(pieces) + 2,
            grid=(t // tm,),
            in_specs=[
                pl.BlockSpec((tm, D_MODEL), row),
                pl.BlockSpec((tm, LANES), row),
                pl.BlockSpec((None, 1, LANES), lambda i, *_: (i, 0, 0)),
            ],
            out_specs=pl.BlockSpec(memory_space=pl.ANY),
            scratch_shapes=[pltpu.VMEM((2, RL, D_MODEL), F32), pltpu.SemaphoreType.DMA((2,))],
        ),
        out_shape=jax.ShapeDtypeStruct((n_pad_rows, D_MODEL), F32),
        compiler_params=_cparams(("arbitrary",)),
        name="sort_dispatch",
    )(*pieces, nch, pend, hf, meta, lo8v)


def _ffn_kernel(be_ref, nv_ref, nxt_ref, x_ref, wg_hbm, bg_ref, wu_hbm, bu_ref, wd_hbm, bd_ref,
                o_ref, wstage_ref, wbf_ref, sems):
    i = pl.program_id(0)
    w_hbm = (wg_hbm, wu_hbm, wd_hbm)

    def fetch(e, m):
        return pltpu.make_async_copy(w_hbm[m].at[e], wstage_ref.at[m], sems.at[m])

    @pl.when(i < nv_ref[0])
    def _():
        @pl.when(i == 0)
        def _():
            for m in range(3):
                fetch(be_ref[0], m).start()

        @pl.when(jnp.logical_or(i == 0, be_ref[i] != be_ref[jnp.maximum(i - 1, 0)]))
        def _():
            for m in range(3):
                fetch(be_ref[i], m).wait()
                wbf_ref[m] = wstage_ref[m].astype(BF16)

            @pl.when(nxt_ref[i] >= 0)
            def _():
                for m in range(3):
                    fetch(nxt_ref[i], m).start()

        x = x_ref[...].astype(BF16)
        y = jnp.broadcast_to(bd_ref[...], o_ref.shape)
        for c in range(D_MODEL // FFN_COLS):
            cs = slice(c * FFN_COLS, (c + 1) * FFN_COLS)
            g = jnp.dot(x, wbf_ref[0, :, cs], preferred_element_type=F32) + bg_ref[:, cs]
            g = jnp.minimum(g, SWIGLU_LIMIT)
            u = jnp.dot(x, wbf_ref[1, :, cs], preferred_element_type=F32) + bu_ref[:, cs]
            u = jnp.clip(u, -SWIGLU_LIMIT, SWIGLU_LIMIT)
            act = g * jax.nn.sigmoid(SWIGLU_ALPHA * g) * (u + 1.0)
            y = y + jnp.dot(act.astype(BF16), wbf_ref[2, cs, :], preferred_element_type=F32)
        o_ref[...] = y

    @pl.when(pl.program_id(0) >= nv_ref[0])
    def _():
        o_ref[...] = jnp.zeros_like(o_ref)


def _expert_ffn(block_e, nvalid, next_e, xs, wg, bg, wu, bu, wd, bd):
    n_rows = xs.shape[0]
    nb = n_rows // BM

    def blk(i, be, nv, nx):
        return jnp.minimum(i, nv[0] - 1)

    xmap = lambda i, be, nv, nx: (blk(i, be, nv, nx), 0)
    bmap = lambda i, be, nv, nx: (be[blk(i, be, nv, nx)], 0, 0)
    hbm = pl.BlockSpec(memory_space=pl.ANY)
    return pl.pallas_call(
        _ffn_kernel,
        grid_spec=pltpu.PrefetchScalarGridSpec(
            num_scalar_prefetch=3,
            grid=(nb,),
            in_specs=[
                pl.BlockSpec((BM, D_MODEL), xmap),
                hbm,
                pl.BlockSpec((None, 1, D_MODEL), bmap),
                hbm,
                pl.BlockSpec((None, 1, D_MODEL), bmap),
                hbm,
                pl.BlockSpec((None, 1, D_MODEL), bmap),
            ],
            out_specs=pl.BlockSpec((BM, D_MODEL), lambda i, be, nv, nx: (i, 0)),
            scratch_shapes=[pltpu.VMEM((3, D_MODEL, D_MODEL), F32),
                            pltpu.VMEM((3, D_MODEL, D_MODEL), BF16),
                            pltpu.SemaphoreType.DMA((3,))],
        ),
        out_shape=jax.ShapeDtypeStruct((n_rows, D_MODEL), F32),
        compiler_params=_cparams(("arbitrary",)),
        name="expert_ffn",
    )(block_e, nvalid, next_e, xs, wg, bg, wu, bu, wd, bd)


def _combine_kernel(blo_ref, bgo_ref, nbig_ref, slo_ref, sgo_ref, nsml_ref, nch_ref,
                    x1_ref, meta_ref, lo8v_ref, g_ref, yb_hbm, o_ref, ybl_ref, sems):
    pieces = (blo_ref, bgo_ref, nbig_ref, slo_ref, sgo_ref, nsml_ref)
    i = pl.program_id(0)
    nt = pl.num_programs(0)
    b = i % 2
    tm = x1_ref.shape[0]

    def seg_copy(buf, lo, go, rows):
        return pltpu.make_async_copy(
            yb_hbm.at[pl.ds(pl.multiple_of(go, SEG), rows)],
            ybl_ref.at[buf, pl.ds(pl.multiple_of(lo, SEG), rows)], sems.at[buf])

    def issue_tile(tile, buf):
        _issue_pieces(functools.partial(seg_copy, buf), pieces, tile)

    def wait_tile(tile, buf):
        _wait_rows(functools.partial(seg_copy, buf, 0, 0), nch_ref[tile])

    @pl.when(i == 0)
    def _():
        ybl_ref[...] = jnp.zeros_like(ybl_ref)
        issue_tile(0, 0)

    @pl.when(i + 1 < nt)
    def _():
        issue_tile(i + 1, 1 - b)

    wait_tile(i, b)

    meta = meta_ref[...]
    lpos = _local_positions(meta, lo8v_ref[...])
    cols = []
    for j in range(RL // PCOLS):
        r = (lax.broadcasted_iota(jnp.int32, (tm, PCOLS), 1) + j * PCOLS).astype(F32)
        gmat = jnp.zeros((tm, PCOLS), F32)
        for kk in range(TOP_K):
            gmat = jnp.where(r == lpos[kk], meta[:, 2 * TOP_K + kk:2 * TOP_K + kk + 1], gmat)
        cols.append(gmat.astype(BF16))
    gates = jnp.concatenate(cols, axis=1)
    acc = x1_ref[...] + jnp.dot(gates, ybl_ref[b].astype(BF16), preferred_element_type=F32)
    ms = jnp.mean(acc * acc, axis=-1, keepdims=True)
    o_ref[...] = acc * lax.rsqrt(ms + NORM_EPS) * g_ref[...]


def _combine(pieces, nch, x1, meta, lo8v, final_g, yb):
    t = x1.shape[0]
    tm = TM_PROJ
    row = lambda i, *_: (i, 0)
    return pl.pallas_call(
        _combine_kernel,
        grid_spec=pltpu.PrefetchScalarGridSpec(
            num_scalar_prefetch=len(pieces) + 1,
            grid=(t // tm,),
            in_specs=[
                pl.BlockSpec((tm, D_MODEL), row),
                pl.BlockSpec((tm, LANES), row),
                pl.BlockSpec((None, 1, LANES), lambda i, *_: (i, 0, 0)),
                pl.BlockSpec((1, D_MODEL), lambda i, *_: (0, 0)),
                pl.BlockSpec(memory_space=pl.ANY),
            ],
            out_specs=pl.BlockSpec((tm, D_MODEL), row),
            scratch_shapes=[pltpu.VMEM((2, RL, D_MODEL), F32), pltpu.SemaphoreType.DMA((2,))],
        ),
        out_shape=jax.ShapeDtypeStruct((t, D_MODEL), F32),
        compiler_params=_cparams(("arbitrary",)),
        name="combine",
    )(*pieces, nch, x1, meta, lo8v, final_g, yb)


def _expand_pieces(cnt, lo, go, step, cap):
    cum = jnp.cumsum(cnt, axis=1)[:, None, :]
    start = cum - cnt[:, None, :]
    p = jnp.arange(cap, dtype=jnp.int32)[None, :, None]
    mine = jnp.logical_and(p >= start, p < cum)
    off = (p - start) * step
    lo_p = jnp.sum(jnp.where(mine, lo[:, None, :] + off, 0), axis=2)
    go_p = jnp.sum(jnp.where(mine, go[:, None, :] + off, 0), axis=2)
    return (lo_p.astype(jnp.int32).reshape(-1), go_p.astype(jnp.int32).reshape(-1),
            cum[:, 0, -1].astype(jnp.int32))


def _piece_lists(nseg, lo8, goff):
    nbig = nseg >> BIG_SHIFT
    nsml = nseg - (nbig << BIG_SHIFT)
    done = nbig * BIG
    return (_expand_pieces(nbig, lo8, goff, BIG, MAX_BIG)
            + _expand_pieces(nsml, lo8 + done, goff + done, SEG, MAX_SML))


def _block_diag(w):
    h, d, _ = w.shape
    eye = jnp.eye(h, dtype=w.dtype)
    return (eye[:, None, :, None] * w[:, :, None, :]).reshape(h * d, h * d)


def kernel(x, positions, attn_norm_g, w_in, conv_w, conv_b, lru_wa, lru_ba, lru_wx, lru_bx, lru_lambda, lru_norm_g, ret_norm_g, ret_norm_b, w_out, ffn_norm_g, router_w, router_b, moe_w_gate, moe_b_gate, moe_w_up, moe_b_up, moe_w_down, moe_b_down, final_norm_g):
    bsz, seq, d = x.shape
    depth = w_in.shape[0]
    t = bsz * seq
    assert depth == 1 and d == D_MODEL and seq % TC_SEQ == 0 and t % TM_PROJ == 0
    nt = t // TM_PROJ
    n_pad = t * TOP_K + nt * N_EXPERTS * SEG + N_EXPERTS * BM
    nb = n_pad // BM

    half = RET_HEAD_DIM // 2
    inv_freq = ROPE_THETA ** (-jnp.arange(half, dtype=F32) / half)
    inv_freq = jnp.concatenate([inv_freq, inv_freq]).reshape(1, RET_HEAD_DIM)
    sign = jnp.concatenate([-jnp.ones((half,), F32), jnp.ones((half,), F32)]).reshape(1, RET_HEAD_DIM)
    pos_col = positions.reshape(t, 1).astype(jnp.int32)
    tri = jnp.tril(jnp.ones((TM_PROJ, TM_PROJ), F32), -1).astype(BF16)

    x2 = x.reshape(t, d)
    for l in range(depth):
        wg_bd = jnp.concatenate([_block_diag(lru_wa[l]), _block_diag(lru_wx[l])], axis=1).astype(BF16)
        lam = lru_lambda[l].astype(F32)
        sp = (jnp.maximum(-lam, 0.0) + jnp.log1p(jnp.exp(-jnp.abs(lam)))).reshape(1, D_LRU)
        y_lru, y_ret = _mixer(
            x2, pos_col, seq, attn_norm_g[l].reshape(1, d), w_in[l].astype(BF16),
            conv_w[l], conv_b[l].reshape(1, D_LRU), wg_bd,
            lru_ba[l].reshape(1, D_LRU), lru_bx[l].reshape(1, D_LRU), sp,
            lru_norm_g[l].reshape(1, D_LRU), inv_freq, sign,
            ret_norm_g[l].reshape(1, D_RET), ret_norm_b[l].reshape(1, D_RET))

        rw_pad = jnp.zeros((d, LANES), F32).at[:, :N_EXPERTS].set(router_w[l]).astype(BF16)
        rb_pad = jnp.full((1, LANES), -1e30, F32).at[0, :N_EXPERTS].set(router_b[l])
        x1, hf, meta, cnt_tile = _outproj_router(x2, y_lru, y_ret, w_out[l].astype(BF16),
                                            ffn_norm_g[l].reshape(1, d), rw_pad, rb_pad, tri)

        cnt = cnt_tile[:, 0, :N_EXPERTS].astype(jnp.int32)
        c8 = (cnt + SEG - 1) // SEG * SEG
        lo8 = jnp.cumsum(c8, axis=1) - c8
        padded = (jnp.sum(c8, axis=0) + BM - 1) // BM * BM
        pend = jnp.cumsum(padded).astype(jnp.int32)
        pstart = pend - padded
        goff = (pstart[None, :] + jnp.cumsum(c8, axis=0) - c8).astype(jnp.int32)
        nseg = (c8 // SEG).astype(jnp.int32)
        nch = jnp.sum(nseg, axis=1).astype(jnp.int32)
        lo8v = jnp.zeros((nt, 1, LANES), F32).at[:, 0, :N_EXPERTS].set(lo8.astype(F32))
        block_start = jnp.arange(nb, dtype=jnp.int32) * BM
        block_e = jnp.minimum(jnp.sum(block_start[:, None] >= pend[None, :], axis=1),
                              N_EXPERTS - 1).astype(jnp.int32)
        nvalid = (pend[N_EXPERTS - 1:] // BM).astype(jnp.int32)
        pieces = _piece_lists(nseg, lo8.astype(jnp.int32), goff)

        xs = _sort_dispatch(pieces, nch, pend, hf, meta, lo8v, n_pad)
        next_blk = pend[block_e] // BM
        next_e = jnp.where(next_blk < nvalid[0], block_e[jnp.minimum(next_blk, nb - 1)],
                           -1).astype(jnp.int32)
        yb = _expert_ffn(block_e, nvalid, next_e, xs,
                         moe_w_gate[l], moe_b_gate[l].reshape(N_EXPERTS, 1, d),
                         moe_w_up[l], moe_b_up[l].reshape(N_EXPERTS, 1, d),
                         moe_w_down[l], moe_b_down[l].reshape(N_EXPERTS, 1, d))
        x2 = _combine(pieces, nch, x1, meta, lo8v, final_norm_g.reshape(1, d), yb)
    return x2.reshape(bsz, seq, d)
```

```python
import functools
import math

import numpy as np
import jax
import jax.numpy as jnp
from jax import lax
from jax.experimental import pallas as pl
from jax.experimental.pallas import tpu as pltpu

F32 = jnp.float32
BF16 = jnp.bfloat16

D_MODEL = 1024
D_LRU = 512
D_RET = 512
LRU_HEADS = 8
LRU_HEAD_DIM = D_LRU // LRU_HEADS
CONV_WIDTH = 4
LRU_C = 8.0
RET_HEADS = 4
RET_HEAD_DIM = D_RET // RET_HEADS
ROPE_THETA = 10000.0
D_IN = 2 * D_LRU + 4 * D_RET
N_EXPERTS = 32
TOP_K = 4
SWIGLU_ALPHA = 1.702
SWIGLU_LIMIT = 7.0
NORM_EPS = 1e-5

LANES = 128
SUBLANES = 8
VMEM_LIMIT = 48 * 1024 * 1024

TM_PROJ = 512
TC_SEQ = 256
TC_MIX = 512
BM = 512
FFN_COLS = 1024
SEG = SUBLANES
RL = TM_PROJ * TOP_K + N_EXPERTS * SEG
PCOLS = 256
PROWS = 768
BIG_SHIFT = 2
BIG = SEG << BIG_SHIFT
ISSUE_SHIFT = 2
ISSUE = 1 << ISSUE_SHIFT
MAX_BIG = RL // BIG
MAX_SML = N_EXPERTS * ((1 << BIG_SHIFT) - 1)
WAIT_SHIFT = 5
WAITBIG = SEG << WAIT_SHIFT

LOG_GAMMA = [math.log1p(-(2.0 ** (-5.0 - h))) for h in range(RET_HEADS)]


def _split_quarter_pi():
    rest = np.float64(np.pi) / 4.0
    parts = []
    for _ in range(3):
        m, e = np.frexp(rest)
        piece = np.ldexp(np.round(m * 1024.0) / 1024.0, e)
        parts.append(float(piece))
        rest = rest - piece
    parts.append(float(np.float32(rest)))
    return parts


_QPI = _split_quarter_pi()
_SIN_C = (-1.9515295891e-4, 8.3321608736e-3, -1.6666654611e-1)
_COS_C = (2.443315711809948e-5, -1.388731625493765e-3, 4.166664568298827e-2)


def _sincos(x):
    ax = jnp.abs(x)
    j = (ax * (4.0 / math.pi)).astype(jnp.int32)
    j = j + jnp.bitwise_and(j, 1)
    y = j.astype(F32)
    r = (((ax - y * _QPI[0]) - y * _QPI[1]) - y * _QPI[2]) - y * _QPI[3]
    z = r * r
    sin_p = r + r * z * ((_SIN_C[0] * z + _SIN_C[1]) * z + _SIN_C[2])
    cos_p = 1.0 - 0.5 * z + z * z * ((_COS_C[0] * z + _COS_C[1]) * z + _COS_C[2])
    quad = jnp.right_shift(j, 1)
    swap = jnp.bitwise_and(quad, 1) == 1
    sin_v = jnp.where(swap, cos_p, sin_p)
    cos_v = jnp.where(swap, sin_p, cos_p)
    sin_neg = jnp.logical_xor(jnp.bitwise_and(quad, 2) == 2, x < 0.0)
    cos_neg = jnp.bitwise_and(quad + 1, 2) == 2
    return jnp.where(sin_neg, -sin_v, sin_v), jnp.where(cos_neg, -cos_v, cos_v)


def _cparams(sem):
    return pltpu.CompilerParams(dimension_semantics=sem, vmem_limit_bytes=VMEM_LIMIT)


def _mixer_kernel(x_ref, pos_ref, ng_ref, win_ref, cw_ref, cb_ref, wg_ref, ba_ref, bx_ref,
                  sp_ref, lng_ref, invf_ref, sgn_ref, gg_ref, gb_ref,
                  ylru_ref, yret_ref,
                  proj_ref, xext_ref, h_ref, state_ref, decay_ref, *, ns):
    i = pl.program_id(0)
    tc = x_ref.shape[0]
    dh = RET_HEAD_DIM

    @pl.when(i == 0)
    def _():
        proj_ref[1] = jnp.zeros((tc, D_IN), BF16)
        ri = lax.broadcasted_iota(jnp.int32, (tc, tc), 0)
        ci = lax.broadcasted_iota(jnp.int32, (tc, tc), 1)
        rel = (ri - ci).astype(F32)
        causal = rel >= 0.0
        relc = jnp.where(causal, rel, 0.0)
        for h in range(RET_HEADS):
            decay_ref[h] = jnp.where(causal, jnp.exp(LOG_GAMMA[h] * relc), 0.0)

    @pl.when(lax.rem(jnp.maximum(i - 1, 0), ns) == 0)
    def _():
        xext_ref[0:SUBLANES, :] = jnp.zeros((SUBLANES, D_LRU), F32)
        h_ref[...] = jnp.zeros_like(h_ref)
        state_ref[...] = jnp.zeros_like(state_ref)

    x = x_ref[...]
    ms = jnp.mean(x * x, axis=-1, keepdims=True)
    hn = x * lax.rsqrt(ms + NORM_EPS) * ng_ref[...]
    proj_ref[i % 2] = jnp.dot(hn.astype(BF16), win_ref[...],
                              preferred_element_type=F32).astype(BF16)

    slot = (i + 1) % 2

    def pcol(lo, width):
        return proj_ref[slot, :, lo:lo + width]

    xext_ref[SUBLANES:SUBLANES + tc, :] = pcol(0, D_LRU).astype(F32)
    xc = cb_ref[...] + cw_ref[CONV_WIDTH - 1:CONV_WIDTH, :] * xext_ref[SUBLANES:SUBLANES + tc, :]
    for j in range(CONV_WIDTH - 1):
        off = SUBLANES - (CONV_WIDTH - 1) + j
        xc = xc + cw_ref[j:j + 1, :] * xext_ref[off:off + tc, :]
    xext_ref[0:SUBLANES, :] = xext_ref[tc:tc + SUBLANES, :]

    gates = jnp.dot(xc.astype(BF16), wg_ref[...], preferred_element_type=F32)
    r = jax.nn.sigmoid(gates[:, :D_LRU] + ba_ref[...])
    ig = jax.nn.sigmoid(gates[:, D_LRU:] + bx_ref[...])
    a = jnp.exp((-LRU_C) * r * sp_ref[...])
    z = 1.0 - a * a
    b = jnp.where(z > 0.0, z * lax.rsqrt(z), 0.0) * (ig * xc)

    ng = tc // SUBLANES
    a = a.reshape(ng, SUBLANES, D_LRU)
    b = b.reshape(ng, SUBLANES, D_LRU)
    in_group = lax.broadcasted_iota(jnp.int32, (1, SUBLANES, 1), 1)
    d = 1
    while d < SUBLANES:
        a_s = pltpu.roll(a, d, 1)
        b_s = pltpu.roll(b, d, 1)
        m = in_group >= d
        b = jnp.where(m, a * b_s + b, b)
        a = jnp.where(m, a * a_s, a)
        d *= 2
    h_prev = h_ref[0:1, :]
    groups = []
    for g in range(ng):
        hg = b[g] + a[g] * h_prev
        groups.append(hg)
        h_prev = hg[SUBLANES - 1:SUBLANES, :]
    hseq = jnp.concatenate(groups, axis=0)
    h_ref[0:1, :] = h_prev

    y = hseq * jax.nn.gelu(pcol(D_LRU, D_LRU).astype(F32))
    ms = jnp.mean(y * y, axis=-1, keepdims=True)
    ylru_ref[...] = (y * lax.rsqrt(ms + NORM_EPS) * lng_ref[...]).astype(ylru_ref.dtype)

    hr = tc // 2
    pos = pos_ref[...].astype(F32)
    first = lax.broadcasted_iota(jnp.int32, (hr, dh), 1) < dh // 2
    ang = jnp.where(first, pos[0:hr, :], pos[hr:tc, :]) * invf_ref[...]
    s2, c2 = _sincos(ang)
    c2r = pltpu.roll(c2, dh // 2, 1)
    s2r = pltpu.roll(s2, dh // 2, 1)
    cos = jnp.concatenate([jnp.where(first, c2, c2r), jnp.where(first, c2r, c2)], axis=0)
    sin = jnp.concatenate([jnp.where(first, s2, s2r), jnp.where(first, s2r, s2)], axis=0)
    sin_signed = sin * sgn_ref[...]
    rowf = lax.broadcasted_iota(jnp.int32, (tc, 1), 0).astype(F32)
    scale = dh ** -0.5
    q0, k0, v0, g0 = 2 * D_LRU, 2 * D_LRU + D_RET, 2 * D_LRU + 2 * D_RET, 2 * D_LRU + 3 * D_RET

    for h in range(RET_HEADS):
        sl = slice(h * dh, (h + 1) * dh)
        q = pcol(q0 + h * dh, dh).astype(F32)
        k = pcol(k0 + h * dh, dh).astype(F32)
        v = pcol(v0 + h * dh, dh)
        qr = q * cos + pltpu.roll(q, dh // 2, 1) * sin_signed
        kr = (k * cos + pltpu.roll(k, dh // 2, 1) * sin_signed) * scale
        qb = qr.astype(BF16)
        kb = kr.astype(BF16)
        lg = LOG_GAMMA[h]
        scores = lax.dot_general(qb, kb, (((1,), (1,)), ((), ())),
                                 preferred_element_type=F32) * decay_ref[h]
        intra = jnp.dot(scores.astype(BF16), v, preferred_element_type=F32)
        st = state_ref[h]
        xi = jnp.exp(lg * (rowf + 1.0))
        cross = jnp.dot(qb, st.astype(BF16), preferred_element_type=F32) * xi
        o = intra + cross
        zeta = jnp.exp(lg * (float(tc) - 1.0 - rowf))
        kz = (kr * zeta).astype(BF16)
        kv = lax.dot_general(kz, v, (((0,), (0,)), ((), ())), preferred_element_type=F32)
        state_ref[h] = math.exp(lg * tc) * st + kv

        mu = jnp.mean(o, axis=-1, keepdims=True)
        oc = o - mu
        var = jnp.mean(oc * oc, axis=-1, keepdims=True)
        on = oc * lax.rsqrt(var + NORM_EPS) * gg_ref[:, sl] + gb_ref[:, sl]
        gate = pcol(g0 + h * dh, dh).astype(F32)
        yret_ref[:, sl] = (gate * jax.nn.sigmoid(gate) * on).astype(yret_ref.dtype)


def _mixer(x2, pos_col, seq, norm_g, w_in_bf16, conv_w, conv_b, wg_bd, ba, bx, sp, lru_norm_g,
           inv_freq, sign, gn_g, gn_b):
    t = x2.shape[0]
    tc = TC_MIX
    n_tiles = t // tc
    const = lambda i: (0, 0)
    cur = lambda i: (jnp.minimum(i, n_tiles - 1), 0)
    prev = lambda i: (jnp.maximum(i - 1, 0), 0)
    vec = lambda n: pl.BlockSpec((1, n), const)
    return pl.pallas_call(
        functools.partial(_mixer_kernel, ns=seq // tc),
        grid=(n_tiles + 1,),
        in_specs=[
            pl.BlockSpec((tc, D_MODEL), cur),
            pl.BlockSpec((tc, 1), prev),
            vec(D_MODEL),
            pl.BlockSpec((D_MODEL, D_IN), const),
            pl.BlockSpec((CONV_WIDTH, D_LRU), const),
            vec(D_LRU),
            pl.BlockSpec((D_LRU, 2 * D_LRU), const),
            vec(D_LRU), vec(D_LRU), vec(D_LRU), vec(D_LRU),
            vec(RET_HEAD_DIM), vec(RET_HEAD_DIM),
            vec(D_RET), vec(D_RET),
        ],
        out_specs=[pl.BlockSpec((tc, D_LRU), prev), pl.BlockSpec((tc, D_RET), prev)],
        out_shape=[jax.ShapeDtypeStruct((t, D_LRU), BF16), jax.ShapeDtypeStruct((t, D_RET), BF16)],
        scratch_shapes=[
            pltpu.VMEM((2, tc, D_IN), BF16),
            pltpu.VMEM((tc + SUBLANES, D_LRU), F32),
            pltpu.VMEM((SUBLANES, D_LRU), F32),
            pltpu.VMEM((RET_HEADS, RET_HEAD_DIM, RET_HEAD_DIM), F32),
            pltpu.VMEM((RET_HEADS, tc, tc), F32),
        ],
        compiler_params=_cparams(("arbitrary",)),
        name="mixer",
    )(x2, pos_col, norm_g, w_in_bf16, conv_w, conv_b, wg_bd, ba, bx, sp, lru_norm_g,
      inv_freq, sign, gn_g, gn_b)


def _outproj_router_kernel(x_ref, yl_ref, yr_ref, wo_ref, ng_ref, rw_ref, rb_ref, tri_ref,
                           x1_ref, hf_ref, meta_ref, cnt_ref):
    tm = x_ref.shape[0]

    y = jnp.dot(yl_ref[...], wo_ref[0:D_LRU, :], preferred_element_type=F32)
    y = y + jnp.dot(yr_ref[...], wo_ref[D_LRU:, :], preferred_element_type=F32)
    x1 = x_ref[...] + y
    x1_ref[...] = x1
    ms = jnp.mean(x1 * x1, axis=-1, keepdims=True)
    hf = x1 * lax.rsqrt(ms + NORM_EPS) * ng_ref[...]
    hf_ref[...] = hf.astype(hf_ref.dtype)

    logits = jnp.dot(hf.astype(BF16), rw_ref[...], preferred_element_type=F32) + rb_ref[...]
    lane = lax.broadcasted_iota(jnp.int32, (tm, LANES), 1)
    lane_f = lane.astype(F32)
    work = logits
    vals, idxs = [], []
    onehot = jnp.zeros((tm, LANES), F32)
    for _ in range(TOP_K):
        m = jnp.max(work, axis=-1, keepdims=True)
        idx = jnp.min(jnp.where(work == m, lane_f, float(LANES)), axis=-1, keepdims=True)
        sel = lane_f == idx
        work = jnp.where(sel, -jnp.inf, work)
        onehot = jnp.where(sel, 1.0, onehot)
        vals.append(m)
        idxs.append(idx)
    exps = [jnp.exp(v - vals[0]) for v in vals]
    denom = exps[0] + exps[1] + exps[2] + exps[3]
    gates = [e / denom for e in exps]

    rank_mat = jnp.dot(tri_ref[...], onehot.astype(BF16), preferred_element_type=F32)
    meta = jnp.zeros((tm, LANES), F32)
    for kk in range(TOP_K):
        rk = jnp.sum(jnp.where(lane_f == idxs[kk], rank_mat, 0.0), axis=-1, keepdims=True)
        meta = jnp.where(lane == kk, idxs[kk], meta)
        meta = jnp.where(lane == TOP_K + kk, rk, meta)
        meta = jnp.where(lane == 2 * TOP_K + kk, gates[kk], meta)
    meta_ref[...] = meta
    cnt_ref[...] = jnp.broadcast_to(jnp.sum(onehot, axis=0, keepdims=True), cnt_ref.shape)


def _outproj_router(x2, y_lru, y_ret, w_out_bf16, ng, rw_pad, rb_pad, tri):
    t = x2.shape[0]
    tm = TM_PROJ
    const = lambda i: (0, 0)
    row = lambda i: (i, 0)
    return pl.pallas_call(
        _outproj_router_kernel,
        grid=(t // tm,),
        in_specs=[
            pl.BlockSpec((tm, D_MODEL), row),
            pl.BlockSpec((tm, D_LRU), row),
            pl.BlockSpec((tm, D_RET), row),
            pl.BlockSpec((D_MODEL, D_MODEL), const),
            pl.BlockSpec((1, D_MODEL), const),
            pl.BlockSpec((D_MODEL, LANES), const),
            pl.BlockSpec((1, LANES), const),
            pl.BlockSpec((tm, tm), const),
        ],
        out_specs=[
            pl.BlockSpec((tm, D_MODEL), row),
            pl.BlockSpec((tm, D_MODEL), row),
            pl.BlockSpec((tm, LANES), row),
            pl.BlockSpec((None, SUBLANES, LANES), lambda i: (i, 0, 0)),
        ],
        out_shape=[
            jax.ShapeDtypeStruct((t, D_MODEL), F32),
            jax.ShapeDtypeStruct((t, D_MODEL), BF16),
            jax.ShapeDtypeStruct((t, LANES), F32),
            jax.ShapeDtypeStruct((t // tm, SUBLANES, LANES), F32),
        ],
        compiler_params=_cparams(("arbitrary",)),
        name="outproj_router",
    )(x2, y_lru, y_ret, w_out_bf16, ng, rw_pad, rb_pad, tri)


def _local_positions(meta, lo8_row):
    lane_f = lax.broadcasted_iota(jnp.int32, meta.shape, 1).astype(F32)
    out = []
    for kk in range(TOP_K):
        idx = meta[:, kk:kk + 1]
        rank = meta[:, TOP_K + kk:TOP_K + kk + 1]
        lo = jnp.sum(jnp.where(lane_f == idx, lo8_row, 0.0), axis=-1, keepdims=True)
        out.append(lo + rank)
    return out


def _issue_pieces(copy, pieces, tile):
    blo_ref, bgo_ref, nbig_ref, slo_ref, sgo_ref, nsml_ref = pieces

    def start_list(lo_ref, go_ref, base, n, rows):
        groups = lax.shift_right_logical(n, ISSUE_SHIFT)

        def group(p, c):
            for u in range(ISSUE):
                q = base + p * ISSUE + u
                copy(lo_ref[q], go_ref[q], rows).start()
            return c

        lax.fori_loop(0, groups, group, 0)

        def single(p, c):
            copy(lo_ref[base + p], go_ref[base + p], rows).start()
            return c

        lax.fori_loop(lax.shift_left(groups, ISSUE_SHIFT), n, single, 0)

    start_list(blo_ref, bgo_ref, tile * MAX_BIG, nbig_ref[tile], BIG)
    start_list(slo_ref, sgo_ref, tile * MAX_SML, nsml_ref[tile], SEG)


def _wait_rows(copy, nseg):
    nbig = lax.shift_right_logical(nseg, WAIT_SHIFT)

    def big(j, c):
        copy(WAITBIG).wait()
        return c

    lax.fori_loop(0, nbig, big, 0)

    def small(j, c):
        copy(SEG).wait()
        return c

    lax.fori_loop(0, nseg - lax.shift_left(nbig, WAIT_SHIFT), small, 0)


def _sort_kernel(blo_ref, bgo_ref, nbig_ref, slo_ref, sgo_ref, nsml_ref, nch_ref, pend_ref,
                 hf_ref, meta_ref, lo8v_ref, xs_hbm, sorted_ref, sems, *, min_blocks):
    pieces = (blo_ref, bgo_ref, nbig_ref, slo_ref, sgo_ref, nsml_ref)
    i = pl.program_id(0)
    nt = pl.num_programs(0)
    b = i % 2
    tm = hf_ref.shape[0]
    nb = xs_hbm.shape[0] // BM

    def seg_copy(buf, lo, go, rows):
        return pltpu.make_async_copy(
            sorted_ref.at[buf, pl.ds(pl.multiple_of(lo, SEG), rows)],
            xs_hbm.at[pl.ds(pl.multiple_of(go, SEG), rows)], sems.at[buf])

    def wait_tile(tile, buf):
        _wait_rows(functools.partial(seg_copy, buf, 0, 0), nch_ref[tile])

    @pl.when(i == 0)
    def _():
        sorted_ref[1, 0:BM, :] = jnp.zeros((BM, D_MODEL), F32)

        def zcopy(start):
            return pltpu.make_async_copy(
                sorted_ref.at[1, pl.ds(0, BM)],
                xs_hbm.at[pl.ds(pl.multiple_of(start, BM), BM)], sems.at[1])

        for e in range(N_EXPERTS):
            zcopy(jnp.maximum(pend_ref[e] - BM, 0)).start()
        for e in range(N_EXPERTS):
            zcopy(jnp.maximum(pend_ref[e] - BM, 0)).wait()
        for blk in range(min_blocks, nb):
            @pl.when(blk * BM >= pend_ref[N_EXPERTS - 1])
            def _():
                zcopy(blk * BM).start()
                zcopy(blk * BM).wait()

    @pl.when(i >= 2)
    def _():
        wait_tile(i - 2, b)

    lpos = _local_positions(meta_ref[...], lo8v_ref[...])
    lane = lax.broadcasted_iota(jnp.int32, (tm, LANES), 1)
    packed = jnp.zeros((tm, LANES), F32)
    for kk in range(TOP_K):
        packed = jnp.where(lane == kk, lpos[kk], packed)
    lpos_t = packed.T
    hfb = hf_ref[...]
    for j in range(RL // PROWS):
        r = (lax.broadcasted_iota(jnp.int32, (PROWS, tm), 0) + j * PROWS).astype(F32)
        perm = jnp.zeros((PROWS, tm), F32)
        for kk in range(TOP_K):
            perm = jnp.where(r == lpos_t[kk:kk + 1, :], 1.0, perm)
        sorted_ref[b, j * PROWS:(j + 1) * PROWS, :] = jnp.dot(
            perm.astype(BF16), hfb, preferred_element_type=F32)

    _issue_pieces(functools.partial(seg_copy, b), pieces, i)

    @pl.when(i == nt - 1)
    def _():
        @pl.when(i >= 1)
        def _():
            wait_tile(i - 1, 1 - b)
        wait_tile(i, b)


def _sort_dispatch(pieces, nch, pend, hf, meta, lo8v, n_pad_rows):
    t = hf.shape[0]
    tm = TM_PROJ
    row = lambda i, *_: (i, 0)
    return pl.pallas_call(
        functools.partial(_sort_kernel, min_blocks=t * TOP_K // BM),
        grid_spec=pltpu.PrefetchScalarGridSpec(
            num_scalar_prefetch=len(pieces) + 2,
            grid=(t // tm,),
            in_specs=[
                pl.BlockSpec((tm, D_MODEL), row),
                pl.BlockSpec((tm, LANES), row),
                pl.BlockSpec((None, 1, LANES), lambda i, *_: (i, 0, 0)),
            ],
            out_specs=pl.BlockSpec(memory_space=pl.ANY),
            scratch_shapes=[pltpu.VMEM((2, RL, D_MODEL), F32), pltpu.SemaphoreType.DMA((2,))],
        ),
        out_shape=jax.ShapeDtypeStruct((n_pad_rows, D_MODEL), F32),
        compiler_params=_cparams(("arbitrary",)),
        name="sort_dispatch",
    )(*pieces, nch, pend, hf, meta, lo8v)


def _ffn_kernel(be_ref, nv_ref, nxt_ref, x_ref, wg_hbm, bg_ref, wu_hbm, bu_ref, wd_hbm, bd_ref,
                o_ref, wstage_ref, wbf_ref, sems):
    i = pl.program_id(0)
    w_hbm = (wg_hbm, wu_hbm, wd_hbm)

    def fetch(e, m):
        return pltpu.make_async_copy(w_hbm[m].at[e], wstage_ref.at[m], sems.at[m])

    @pl.when(i < nv_ref[0])
    def _():
        @pl.when(i == 0)
        def _():
            for m in range(3):
                fetch(be_ref[0], m).start()

        @pl.when(jnp.logical_or(i == 0, be_ref[i] != be_ref[jnp.maximum(i - 1, 0)]))
        def _():
            for m in range(3):
                fetch(be_ref[i], m).wait()
                wbf_ref[m] = wstage_ref[m].astype(BF16)

            @pl.when(nxt_ref[i] >= 0)
            def _():
                for m in range(3):
                    fetch(nxt_ref[i], m).start()

        x = x_ref[...].astype(BF16)
        y = jnp.broadcast_to(bd_ref[...], o_ref.shape)
        for c in range(D_MODEL // FFN_COLS):
            cs = slice(c * FFN_COLS, (c + 1) * FFN_COLS)
            g = jnp.dot(x, wbf_ref[0, :, cs], preferred_element_type=F32) + bg_ref[:, cs]
            g = jnp.minimum(g, SWIGLU_LIMIT)
            u = jnp.dot(x, wbf_ref[1, :, cs], preferred_element_type=F32) + bu_ref[:, cs]
            u = jnp.clip(u, -SWIGLU_LIMIT, SWIGLU_LIMIT)
            act = g * jax.nn.sigmoid(SWIGLU_ALPHA * g) * (u + 1.0)
            y = y + jnp.dot(act.astype(BF16), wbf_ref[2, cs, :], preferred_element_type=F32)
        o_ref[...] = y

    @pl.when(pl.program_id(0) >= nv_ref[0])
    def _():
        o_ref[...] = jnp.zeros_like(o_ref)


def _expert_ffn(block_e, nvalid, next_e, xs, wg, bg, wu, bu, wd, bd):
    n_rows = xs.shape[0]
    nb = n_rows // BM

    def blk(i, be, nv, nx):
        return jnp.minimum(i, nv[0] - 1)

    xmap = lambda i, be, nv, nx: (blk(i, be, nv, nx), 0)
    bmap = lambda i, be, nv, nx: (be[blk(i, be, nv, nx)], 0, 0)
    hbm = pl.BlockSpec(memory_space=pl.ANY)
    return pl.pallas_call(
        _ffn_kernel,
        grid_spec=pltpu.PrefetchScalarGridSpec(
            num_scalar_prefetch=3,
            grid=(nb,),
            in_specs=[
                pl.BlockSpec((BM, D_MODEL), xmap),
                hbm,
                pl.BlockSpec((None, 1, D_MODEL), bmap),
                hbm,
                pl.BlockSpec((None, 1, D_MODEL), bmap),
                hbm,
                pl.BlockSpec((None, 1, D_MODEL), bmap),
            ],
            out_specs=pl.BlockSpec((BM, D_MODEL), lambda i, be, nv, nx: (i, 0)),
            scratch_shapes=[pltpu.VMEM((3, D_MODEL, D_MODEL), F32),
                            pltpu.VMEM((3, D_MODEL, D_MODEL), BF16),
                            pltpu.SemaphoreType.DMA((3,))],
        ),
        out_shape=jax.ShapeDtypeStruct((n_rows, D_MODEL), F32),
        compiler_params=_cparams(("arbitrary",)),
        name="expert_ffn",
    )(block_e, nvalid, next_e, xs, wg, bg, wu, bu, wd, bd)


def _combine_kernel(blo_ref, bgo_ref, nbig_ref, slo_ref, sgo_ref, nsml_ref, nch_ref,
                    x1_ref, meta_ref, lo8v_ref, g_ref, yb_hbm, o_ref, ybl_ref, sems):
    pieces = (blo_ref, bgo_ref, nbig_ref, slo_ref, sgo_ref, nsml_ref)
    i = pl.program_id(0)
    nt = pl.num_programs(0)
    b = i % 2
    tm = x1_ref.shape[0]

    def seg_copy(buf, lo, go, rows):
        return pltpu.make_async_copy(
            yb_hbm.at[pl.ds(pl.multiple_of(go, SEG), rows)],
            ybl_ref.at[buf, pl.ds(pl.multiple_of(lo, SEG), rows)], sems.at[buf])

    def issue_tile(tile, buf):
        _issue_pieces(functools.partial(seg_copy, buf), pieces, tile)

    def wait_tile(tile, buf):
        _wait_rows(functools.partial(seg_copy, buf, 0, 0), nch_ref[tile])

    @pl.when(i == 0)
    def _():
        ybl_ref[...] = jnp.zeros_like(ybl_ref)
        issue_tile(0, 0)

    @pl.when(i + 1 < nt)
    def _():
        issue_tile(i + 1, 1 - b)

    wait_tile(i, b)

    meta = meta_ref[...]
    lpos = _local_positions(meta, lo8v_ref[...])
    cols = []
    for j in range(RL // PCOLS):
        r = (lax.broadcasted_iota(jnp.int32, (tm, PCOLS), 1) + j * PCOLS).astype(F32)
        gmat = jnp.zeros((tm, PCOLS), F32)
        for kk in range(TOP_K):
            gmat = jnp.where(r == lpos[kk], meta[:, 2 * TOP_K + kk:2 * TOP_K + kk + 1], gmat)
        cols.append(gmat.astype(BF16))
    gates = jnp.concatenate(cols, axis=1)
    acc = x1_ref[...] + jnp.dot(gates, ybl_ref[b].astype(BF16), preferred_element_type=F32)
    ms = jnp.mean(acc * acc, axis=-1, keepdims=True)
    o_ref[...] = acc * lax.rsqrt(ms + NORM_EPS) * g_ref[...]


def _combine(pieces, nch, x1, meta, lo8v, final_g, yb):
    t = x1.shape[0]
    tm = TM_PROJ
    row = lambda i, *_: (i, 0)
    return pl.pallas_call(
        _combine_kernel,
        grid_spec=pltpu.PrefetchScalarGridSpec(
            num_scalar_prefetch=len(pieces) + 1,
            grid=(t // tm,),
            in_specs=[
                pl.BlockSpec((tm, D_MODEL), row),
                pl.BlockSpec((tm, LANES), row),
                pl.BlockSpec((None, 1, LANES), lambda i, *_: (i, 0, 0)),
                pl.BlockSpec((1, D_MODEL), lambda i, *_: (0, 0)),
                pl.BlockSpec(memory_space=pl.ANY),
            ],
            out_specs=pl.BlockSpec((tm, D_MODEL), row),
            scratch_shapes=[pltpu.VMEM((2, RL, D_MODEL), F32), pltpu.SemaphoreType.DMA((2,))],
        ),
        out_shape=jax.ShapeDtypeStruct((t, D_MODEL), F32),
        compiler_params=_cparams(("arbitrary",)),
        name="combine",
    )(*pieces, nch, x1, meta, lo8v, final_g, yb)


def _expand_pieces(cnt, lo, go, step, cap):
    cum = jnp.cumsum(cnt, axis=1)[:, None, :]
    start = cum - cnt[:, None, :]
    p = jnp.arange(cap, dtype=jnp.int32)[None, :, None]
    mine = jnp.logical_and(p >= start, p < cum)
    off = (p - start) * step
    lo_p = jnp.sum(jnp.where(mine, lo[:, None, :] + off, 0), axis=2)
    go_p = jnp.sum(jnp.where(mine, go[:, None, :] + off, 0), axis=2)
    return (lo_p.astype(jnp.int32).reshape(-1), go_p.astype(jnp.int32).reshape(-1),
            cum[:, 0, -1].astype(jnp.int32))


def _piece_lists(nseg, lo8, goff):
    nbig = nseg >> BIG_SHIFT
    nsml = nseg - (nbig << BIG_SHIFT)
    done = nbig * BIG
    return (_expand_pieces(nbig, lo8, goff, BIG, MAX_BIG)
            + _expand_pieces(nsml, lo8 + done, goff + done, SEG, MAX_SML))


def _block_diag(w):
    h, d, _ = w.shape
    eye = jnp.eye(h, dtype=w.dtype)
    return (eye[:, None, :, None] * w[:, :, None, :]).reshape(h * d, h * d)


def kernel(x, positions, attn_norm_g, w_in, conv_w, conv_b, lru_wa, lru_ba, lru_wx, lru_bx, lru_lambda, lru_norm_g, ret_norm_g, ret_norm_b, w_out, ffn_norm_g, router_w, router_b, moe_w_gate, moe_b_gate, moe_w_up, moe_b_up, moe_w_down, moe_b_down, final_norm_g):
    bsz, seq, d = x.shape
    depth = w_in.shape[0]
    t = bsz * seq
    assert depth == 1 and d == D_MODEL and seq % TC_SEQ == 0 and t % TM_PROJ == 0
    nt = t // TM_PROJ
    n_pad = t * TOP_K + nt * N_EXPERTS * SEG + N_EXPERTS * BM
    nb = n_pad // BM

    half = RET_HEAD_DIM // 2
    inv_freq = ROPE_THETA ** (-jnp.arange(half, dtype=F32) / half)
    inv_freq = jnp.concatenate([inv_freq, inv_freq]).reshape(1, RET_HEAD_DIM)
    sign = jnp.concatenate([-jnp.ones((half,), F32), jnp.ones((half,), F32)]).reshape(1, RET_HEAD_DIM)
    pos_col = positions.reshape(t, 1).astype(jnp.int32)
    tri = jnp.tril(jnp.ones((TM_PROJ, TM_PROJ), F32), -1).astype(BF16)

    x2 = x.reshape(t, d)
    for l in range(depth):
        wg_bd = jnp.concatenate([_block_diag(lru_wa[l]), _block_diag(lru_wx[l])], axis=1).astype(BF16)
        lam = lru_lambda[l].astype(F32)
        sp = (jnp.maximum(-lam, 0.0) + jnp.log1p(jnp.exp(-jnp.abs(lam)))).reshape(1, D_LRU)
        y_lru, y_ret = _mixer(
            x2, pos_col, seq, attn_norm_g[l].reshape(1, d), w_in[l].astype(BF16),
            conv_w[l], conv_b[l].reshape(1, D_LRU), wg_bd,
            lru_ba[l].reshape(1, D_LRU), lru_bx[l].reshape(1, D_LRU), sp,
            lru_norm_g[l].reshape(1, D_LRU), inv_freq, sign,
            ret_norm_g[l].reshape(1, D_RET), ret_norm_b[l].reshape(1, D_RET))

        rw_pad = jnp.zeros((d, LANES), F32).at[:, :N_EXPERTS].set(router_w[l]).astype(BF16)
        rb_pad = jnp.full((1, LANES), -1e30, F32).at[0, :N_EXPERTS].set(router_b[l])
        x1, hf, meta, cnt_tile = _outproj_router(x2, y_lru, y_ret, w_out[l].astype(BF16),
                                            ffn_norm_g[l].reshape(1, d), rw_pad, rb_pad, tri)

        cnt = cnt_tile[:, 0, :N_EXPERTS].astype(jnp.int32)
        c8 = (cnt + SEG - 1) // SEG * SEG
        lo8 = jnp.cumsum(c8, axis=1) - c8
        padded = (jnp.sum(c8, axis=0) + BM - 1) // BM * BM
        pend = jnp.cumsum(padded).astype(jnp.int32)
        pstart = pend - padded
        goff = (pstart[None, :] + jnp.cumsum(c8, axis=0) - c8).astype(jnp.int32)
        nseg = (c8 // SEG).astype(jnp.int32)
        lo8v = jnp.zeros((nt, 1, LANES), F32).at[:, 0, :N_EXPERTS].set(lo8.astype(F32))
        block_start = jnp.arange(nb, dtype=jnp.int32) * BM
        block_e = jnp.minimum(jnp.sum(block_start[:, None] >= pend[None, :], axis=1),
                              N_EXPERTS - 1).astype(jnp.int32)
        nvalid = (pend[N_EXPERTS - 1:] // BM).astype(jnp.int32)
        pieces = _piece_lists(nseg, lo8.astype(jnp.int32), goff)
        nch = jnp.sum(nseg, axis=1).astype(jnp.int32)

        xs = _sort_dispatch(pieces, nch, pend, hf, meta, lo8v, n_pad)
        next_blk = pend[block_e] // BM
        next_e = jnp.where(next_blk < nvalid[0], block_e[jnp.minimum(next_blk, nb - 1)],
                           -1).astype(jnp.int32)
        yb = _expert_ffn(block_e, nvalid, next_e, xs,
                         moe_w_gate[l], moe_b_gate[l].reshape(N_EXPERTS, 1, d),
                         moe_w_up[l], moe_b_up[l].reshape(N_EXPERTS, 1, d),
                         moe_w_down[l], moe_b_down[l].reshape(N_EXPERTS, 1, d))
        x2 = _combine(pieces, nch, x1, meta, lo8v, final_norm_g.reshape(1, d), yb)
    return x2.reshape(bsz, seq, d)
```

```python
import functools
import math

import numpy as np
import jax
import jax.numpy as jnp
from jax import lax
from jax.experimental import pallas as pl
from jax.experimental.pallas import tpu as pltpu

F32 = jnp.float32
BF16 = jnp.bfloat16

D_MODEL = 1024
D_LRU = 512
D_RET = 512
LRU_HEADS = 8
LRU_HEAD_DIM = D_LRU // LRU_HEADS
CONV_WIDTH = 4
LRU_C = 8.0
RET_HEADS = 4
RET_HEAD_DIM = D_RET // RET_HEADS
ROPE_THETA = 10000.0
D_IN = 2 * D_LRU + 4 * D_RET
N_EXPERTS = 32
TOP_K = 4
SWIGLU_ALPHA = 1.702
SWIGLU_LIMIT = 7.0
NORM_EPS = 1e-5

LANES = 128
SUBLANES = 8
VMEM_LIMIT = 48 * 1024 * 1024

TM_PROJ = 512
TC_SEQ = 256
TC_MIX = 512
BM = 512
FFN_COLS = 1024
SEG = SUBLANES
RL = TM_PROJ * TOP_K + N_EXPERTS * SEG
PCOLS = 256
PROWS = 768
BIG_SHIFT = 2
BIG = SEG << BIG_SHIFT
ISSUE_SHIFT = 2
ISSUE = 1 << ISSUE_SHIFT
MAX_BIG = RL // BIG
MAX_SML = N_EXPERTS * ((1 << BIG_SHIFT) - 1)
WAIT_SHIFT = 5
WAITBIG = SEG << WAIT_SHIFT

LOG_GAMMA = [math.log1p(-(2.0 ** (-5.0 - h))) for h in range(RET_HEADS)]


def _split_quarter_pi():
    rest = np.float64(np.pi) / 4.0
    parts = []
    for _ in range(3):
        m, e = np.frexp(rest)
        piece = np.ldexp(np.round(m * 1024.0) / 1024.0, e)
        parts.append(float(piece))
        rest = rest - piece
    parts.append(float(np.float32(rest)))
    return parts


_QPI = _split_quarter_pi()
_SIN_C = (-1.9515295891e-4, 8.3321608736e-3, -1.6666654611e-1)
_COS_C = (2.443315711809948e-5, -1.388731625493765e-3, 4.166664568298827e-2)


def _sincos(x):
    ax = jnp.abs(x)
    j = (ax * (4.0 / math.pi)).astype(jnp.int32)
    j = j + jnp.bitwise_and(j, 1)
    y = j.astype(F32)
    r = (((ax - y * _QPI[0]) - y * _QPI[1]) - y * _QPI[2]) - y * _QPI[3]
    z = r * r
    sin_p = r + r * z * ((_SIN_C[0] * z + _SIN_C[1]) * z + _SIN_C[2])
    cos_p = 1.0 - 0.5 * z + z * z * ((_COS_C[0] * z + _COS_C[1]) * z + _COS_C[2])
    quad = jnp.right_shift(j, 1)
    swap = jnp.bitwise_and(quad, 1) == 1
    sin_v = jnp.where(swap, cos_p, sin_p)
    cos_v = jnp.where(swap, sin_p, cos_p)
    sin_neg = jnp.logical_xor(jnp.bitwise_and(quad, 2) == 2, x < 0.0)
    cos_neg = jnp.bitwise_and(quad + 1, 2) == 2
    return jnp.where(sin_neg, -sin_v, sin_v), jnp.where(cos_neg, -cos_v, cos_v)


def _cparams(sem):
    return pltpu.CompilerParams(dimension_semantics=sem, vmem_limit_bytes=VMEM_LIMIT)


def _mixer_kernel(x_ref, pos_ref, ng_ref, win_ref, cw_ref, cb_ref, wg_ref, ba_ref, bx_ref,
                  sp_ref, lng_ref, invf_ref, sgn_ref, gg_ref, gb_ref,
                  ylru_ref, yret_ref,
                  proj_ref, xext_ref, h_ref, state_ref, decay_ref, *, ns):
    i = pl.program_id(0)
    tc = x_ref.shape[0]
    dh = RET_HEAD_DIM

    @pl.when(i == 0)
    def _():
        proj_ref[1] = jnp.zeros((tc, D_IN), BF16)
        ri = lax.broadcasted_iota(jnp.int32, (tc, tc), 0)
        ci = lax.broadcasted_iota(jnp.int32, (tc, tc), 1)
        rel = (ri - ci).astype(F32)
        causal = rel >= 0.0
        relc = jnp.where(causal, rel, 0.0)
        for h in range(RET_HEADS):
            decay_ref[h] = jnp.where(causal, jnp.exp(LOG_GAMMA[h] * relc), 0.0)

    @pl.when(lax.rem(jnp.maximum(i - 1, 0), ns) == 0)
    def _():
        xext_ref[0:SUBLANES, :] = jnp.zeros((SUBLANES, D_LRU), F32)
        h_ref[...] = jnp.zeros_like(h_ref)
        state_ref[...] = jnp.zeros_like(state_ref)

    x = x_ref[...]
    ms = jnp.mean(x * x, axis=-1, keepdims=True)
    hn = x * lax.rsqrt(ms + NORM_EPS) * ng_ref[...]
    proj_ref[i % 2] = jnp.dot(hn.astype(BF16), win_ref[...],
                              preferred_element_type=F32).astype(BF16)

    slot = (i + 1) % 2

    def pcol(lo, width):
        return proj_ref[slot, :, lo:lo + width]

    xext_ref[SUBLANES:SUBLANES + tc, :] = pcol(0, D_LRU).astype(F32)
    xc = cb_ref[...] + cw_ref[CONV_WIDTH - 1:CONV_WIDTH, :] * xext_ref[SUBLANES:SUBLANES + tc, :]
    for j in range(CONV_WIDTH - 1):
        off = SUBLANES - (CONV_WIDTH - 1) + j
        xc = xc + cw_ref[j:j + 1, :] * xext_ref[off:off + tc, :]
    xext_ref[0:SUBLANES, :] = xext_ref[tc:tc + SUBLANES, :]

    gates = jnp.dot(xc.astype(BF16), wg_ref[...], preferred_element_type=F32)
    r = jax.nn.sigmoid(gates[:, :D_LRU] + ba_ref[...])
    ig = jax.nn.sigmoid(gates[:, D_LRU:] + bx_ref[...])
    a = jnp.exp((-LRU_C) * r * sp_ref[...])
    z = 1.0 - a * a
    b = jnp.where(z > 0.0, z * lax.rsqrt(z), 0.0) * (ig * xc)

    ng = tc // SUBLANES
    a = a.reshape(ng, SUBLANES, D_LRU)
    b = b.reshape(ng, SUBLANES, D_LRU)
    in_group = lax.broadcasted_iota(jnp.int32, (1, SUBLANES, 1), 1)
    d = 1
    while d < SUBLANES:
        a_s = pltpu.roll(a, d, 1)
        b_s = pltpu.roll(b, d, 1)
        m = in_group >= d
        b = jnp.where(m, a * b_s + b, b)
        a = jnp.where(m, a * a_s, a)
        d *= 2
    h_prev = h_ref[0:1, :]
    groups = []
    for g in range(ng):
        hg = b[g] + a[g] * h_prev
        groups.append(hg)
        h_prev = hg[SUBLANES - 1:SUBLANES, :]
    hseq = jnp.concatenate(groups, axis=0)
    h_ref[0:1, :] = h_prev

    y = hseq * jax.nn.gelu(pcol(D_LRU, D_LRU).astype(F32))
    ms = jnp.mean(y * y, axis=-1, keepdims=True)
    ylru_ref[...] = (y * lax.rsqrt(ms + NORM_EPS) * lng_ref[...]).astype(ylru_ref.dtype)

    hr = tc // 2
    pos = pos_ref[...].astype(F32)
    first = lax.broadcasted_iota(jnp.int32, (hr, dh), 1) < dh // 2
    ang = jnp.where(first, pos[0:hr, :], pos[hr:tc, :]) * invf_ref[...]
    s2, c2 = _sincos(ang)
    c2r = pltpu.roll(c2, dh // 2, 1)
    s2r = pltpu.roll(s2, dh // 2, 1)
    cos = jnp.concatenate([jnp.where(first, c2, c2r), jnp.where(first, c2r, c2)], axis=0)
    sin = jnp.concatenate([jnp.where(first, s2, s2r), jnp.where(first, s2r, s2)], axis=0)
    sin_signed = sin * sgn_ref[...]
    rowf = lax.broadcasted_iota(jnp.int32, (tc, 1), 0).astype(F32)
    scale = dh ** -0.5
    q0, k0, v0, g0 = 2 * D_LRU, 2 * D_LRU + D_RET, 2 * D_LRU + 2 * D_RET, 2 * D_LRU + 3 * D_RET

    for h in range(RET_HEADS):
        sl = slice(h * dh, (h + 1) * dh)
        q = pcol(q0 + h * dh, dh).astype(F32)
        k = pcol(k0 + h * dh, dh).astype(F32)
        v = pcol(v0 + h * dh, dh)
        qr = q * cos + pltpu.roll(q, dh // 2, 1) * sin_signed
        kr = (k * cos + pltpu.roll(k, dh // 2, 1) * sin_signed) * scale
        qb = qr.astype(BF16)
        kb = kr.astype(BF16)
        lg = LOG_GAMMA[h]
        scores = lax.dot_general(qb, kb, (((1,), (1,)), ((), ())),
                                 preferred_element_type=F32) * decay_ref[h]
        intra = jnp.dot(scores.astype(BF16), v, preferred_element_type=F32)
        st = state_ref[h]
        xi = jnp.exp(lg * (rowf + 1.0))
        cross = jnp.dot(qb, st.astype(BF16), preferred_element_type=F32) * xi
        o = intra + cross
        zeta = jnp.exp(lg * (float(tc) - 1.0 - rowf))
        kz = (kr * zeta).astype(BF16)
        kv = lax.dot_general(kz, v, (((0,), (0,)), ((), ())), preferred_element_type=F32)
        state_ref[h] = math.exp(lg * tc) * st + kv

        mu = jnp.mean(o, axis=-1, keepdims=True)
        oc = o - mu
        var = jnp.mean(oc * oc, axis=-1, keepdims=True)
        on = oc * lax.rsqrt(var + NORM_EPS) * gg_ref[:, sl] + gb_ref[:, sl]
        gate = pcol(g0 + h * dh, dh).astype(F32)
        yret_ref[:, sl] = (gate * jax.nn.sigmoid(gate) * on).astype(yret_ref.dtype)


def _mixer(x2, pos_col, seq, norm_g, w_in_bf16, conv_w, conv_b, wg_bd, ba, bx, sp, lru_norm_g,
           inv_freq, sign, gn_g, gn_b):
    t = x2.shape[0]
    tc = TC_MIX
    n_tiles = t // tc
    const = lambda i: (0, 0)
    cur = lambda i: (jnp.minimum(i, n_tiles - 1), 0)
    prev = lambda i: (jnp.maximum(i - 1, 0), 0)
    vec = lambda n: pl.BlockSpec((1, n), const)
    return pl.pallas_call(
        functools.partial(_mixer_kernel, ns=seq // tc),
        grid=(n_tiles + 1,),
        in_specs=[
            pl.BlockSpec((tc, D_MODEL), cur),
            pl.BlockSpec((tc, 1), prev),
            vec(D_MODEL),
            pl.BlockSpec((D_MODEL, D_IN), const),
            pl.BlockSpec((CONV_WIDTH, D_LRU), const),
            vec(D_LRU),
            pl.BlockSpec((D_LRU, 2 * D_LRU), const),
            vec(D_LRU), vec(D_LRU), vec(D_LRU), vec(D_LRU),
            vec(RET_HEAD_DIM), vec(RET_HEAD_DIM),
            vec(D_RET), vec(D_RET),
        ],
        out_specs=[pl.BlockSpec((tc, D_LRU), prev), pl.BlockSpec((tc, D_RET), prev)],
        out_shape=[jax.ShapeDtypeStruct((t, D_LRU), BF16), jax.ShapeDtypeStruct((t, D_RET), BF16)],
        scratch_shapes=[
            pltpu.VMEM((2, tc, D_IN), BF16),
            pltpu.VMEM((tc + SUBLANES, D_LRU), F32),
            pltpu.VMEM((SUBLANES, D_LRU), F32),
            pltpu.VMEM((RET_HEADS, RET_HEAD_DIM, RET_HEAD_DIM), F32),
            pltpu.VMEM((RET_HEADS, tc, tc), F32),
        ],
        compiler_params=_cparams(("arbitrary",)),
        name="mixer",
    )(x2, pos_col, norm_g, w_in_bf16, conv_w, conv_b, wg_bd, ba, bx, sp, lru_norm_g,
      inv_freq, sign, gn_g, gn_b)


def _outproj_router_kernel(x_ref, yl_ref, yr_ref, wo_ref, ng_ref, rw_ref, rb_ref, tri_ref,
                           upper_ref, x1_ref, hf_ref, meta_ref, cnt_ref):
    tm = x_ref.shape[0]

    y = jnp.dot(yl_ref[...], wo_ref[0:D_LRU, :], preferred_element_type=F32)
    y = y + jnp.dot(yr_ref[...], wo_ref[D_LRU:, :], preferred_element_type=F32)
    x1 = x_ref[...] + y
    x1_ref[...] = x1
    ms = jnp.mean(x1 * x1, axis=-1, keepdims=True)
    hf = x1 * lax.rsqrt(ms + NORM_EPS) * ng_ref[...]
    hf_ref[...] = hf.astype(hf_ref.dtype)

    logits = jnp.dot(hf.astype(BF16), rw_ref[...], preferred_element_type=F32) + rb_ref[...]
    lane = lax.broadcasted_iota(jnp.int32, (tm, LANES), 1)
    lane_f = lane.astype(F32)
    work = logits
    vals, idxs = [], []
    onehot = jnp.zeros((tm, LANES), F32)
    for _ in range(TOP_K):
        m = jnp.max(work, axis=-1, keepdims=True)
        idx = jnp.min(jnp.where(work == m, lane_f, float(LANES)), axis=-1, keepdims=True)
        sel = lane_f == idx
        work = jnp.where(sel, -jnp.inf, work)
        onehot = jnp.where(sel, 1.0, onehot)
        vals.append(m)
        idxs.append(idx)
    exps = [jnp.exp(v - vals[0]) for v in vals]
    denom = exps[0] + exps[1] + exps[2] + exps[3]
    gates = [e / denom for e in exps]

    counts = jnp.sum(onehot, axis=0, keepdims=True)
    seg_rows = jnp.floor((counts + (SEG - 1.0)) * (1.0 / SEG)) * SEG
    seg_start = jnp.dot(jnp.broadcast_to(seg_rows, (SUBLANES, LANES)).astype(BF16), upper_ref[...],
                        preferred_element_type=F32)[0:1, :]
    pos_mat = jnp.dot(tri_ref[...], onehot.astype(BF16), preferred_element_type=F32) + seg_start
    meta = jnp.zeros((tm, LANES), F32)
    for kk in range(TOP_K):
        pos = jnp.sum(jnp.where(lane_f == idxs[kk], pos_mat, 0.0), axis=-1, keepdims=True)
        meta = jnp.where(lane == kk, idxs[kk], meta)
        meta = jnp.where(lane == TOP_K + kk, pos, meta)
        meta = jnp.where(lane == 2 * TOP_K + kk, gates[kk], meta)
    meta_ref[...] = meta
    cnt_ref[...] = jnp.broadcast_to(counts, cnt_ref.shape)


def _outproj_router(x2, y_lru, y_ret, w_out_bf16, ng, rw_pad, rb_pad, tri, upper):
    t = x2.shape[0]
    tm = TM_PROJ
    const = lambda i: (0, 0)
    row = lambda i: (i, 0)
    return pl.pallas_call(
        _outproj_router_kernel,
        grid=(t // tm,),
        in_specs=[
            pl.BlockSpec((tm, D_MODEL), row),
            pl.BlockSpec((tm, D_LRU), row),
            pl.BlockSpec((tm, D_RET), row),
            pl.BlockSpec((D_MODEL, D_MODEL), const),
            pl.BlockSpec((1, D_MODEL), const),
            pl.BlockSpec((D_MODEL, LANES), const),
            pl.BlockSpec((1, LANES), const),
            pl.BlockSpec((tm, tm), const),
            pl.BlockSpec((LANES, LANES), const),
        ],
        out_specs=[
            pl.BlockSpec((tm, D_MODEL), row),
            pl.BlockSpec((tm, D_MODEL), row),
            pl.BlockSpec((tm, LANES), row),
            pl.BlockSpec((None, SUBLANES, LANES), lambda i: (i, 0, 0)),
        ],
        out_shape=[
            jax.ShapeDtypeStruct((t, D_MODEL), F32),
            jax.ShapeDtypeStruct((t, D_MODEL), BF16),
            jax.ShapeDtypeStruct((t, LANES), F32),
            jax.ShapeDtypeStruct((t // tm, SUBLANES, LANES), F32),
        ],
        compiler_params=_cparams(("arbitrary",)),
        name="outproj_router",
    )(x2, y_lru, y_ret, w_out_bf16, ng, rw_pad, rb_pad, tri, upper)


def _issue_pieces(copy, pieces, tile):
    blo_ref, bgo_ref, nbig_ref, slo_ref, sgo_ref, nsml_ref = pieces

    def start_list(lo_ref, go_ref, base, n, rows):
        groups = lax.shift_right_logical(n, ISSUE_SHIFT)

        def group(p, c):
            for u in range(ISSUE):
                q = base + p * ISSUE + u
                copy(lo_ref[q], go_ref[q], rows).start()
            return c

        lax.fori_loop(0, groups, group, 0)

        def single(p, c):
            copy(lo_ref[base + p], go_ref[base + p], rows).start()
            return c

        lax.fori_loop(lax.shift_left(groups, ISSUE_SHIFT), n, single, 0)

    start_list(blo_ref, bgo_ref, tile * MAX_BIG, nbig_ref[tile], BIG)
    start_list(slo_ref, sgo_ref, tile * MAX_SML, nsml_ref[tile], SEG)


def _wait_rows(copy, nseg):
    nbig = lax.shift_right_logical(nseg, WAIT_SHIFT)

    def big(j, c):
        copy(WAITBIG).wait()
        return c

    lax.fori_loop(0, nbig, big, 0)

    def small(j, c):
        copy(SEG).wait()
        return c

    lax.fori_loop(0, nseg - lax.shift_left(nbig, WAIT_SHIFT), small, 0)


def _sort_kernel(blo_ref, bgo_ref, nbig_ref, slo_ref, sgo_ref, nsml_ref, nch_ref, pend_ref,
                 hf_ref, meta_ref, xs_hbm, sorted_ref, sems, *, min_blocks):
    pieces = (blo_ref, bgo_ref, nbig_ref, slo_ref, sgo_ref, nsml_ref)
    i = pl.program_id(0)
    nt = pl.num_programs(0)
    b = i % 2
    tm = hf_ref.shape[0]
    nb = xs_hbm.shape[0] // BM

    def seg_copy(buf, lo, go, rows):
        return pltpu.make_async_copy(
            sorted_ref.at[buf, pl.ds(pl.multiple_of(lo, SEG), rows)],
            xs_hbm.at[pl.ds(pl.multiple_of(go, SEG), rows)], sems.at[buf])

    def wait_tile(tile, buf):
        _wait_rows(functools.partial(seg_copy, buf, 0, 0), nch_ref[tile])

    @pl.when(i == 0)
    def _():
        sorted_ref[1, 0:BM, :] = jnp.zeros((BM, D_MODEL), F32)

        def zcopy(start):
            return pltpu.make_async_copy(
                sorted_ref.at[1, pl.ds(0, BM)],
                xs_hbm.at[pl.ds(pl.multiple_of(start, BM), BM)], sems.at[1])

        for e in range(N_EXPERTS):
            zcopy(jnp.maximum(pend_ref[e] - BM, 0)).start()
        for e in range(N_EXPERTS):
            zcopy(jnp.maximum(pend_ref[e] - BM, 0)).wait()
        for blk in range(min_blocks, nb):
            @pl.when(blk * BM >= pend_ref[N_EXPERTS - 1])
            def _():
                zcopy(blk * BM).start()
                zcopy(blk * BM).wait()

    @pl.when(i >= 2)
    def _():
        wait_tile(i - 2, b)

    meta_t = meta_ref[...].T
    hfb = hf_ref[...]
    for j in range(RL // PROWS):
        r = (lax.broadcasted_iota(jnp.int32, (PROWS, tm), 0) + j * PROWS).astype(F32)
        perm = jnp.zeros((PROWS, tm), F32)
        for kk in range(TOP_K):
            perm = jnp.where(r == meta_t[TOP_K + kk:TOP_K + kk + 1, :], 1.0, perm)
        sorted_ref[b, j * PROWS:(j + 1) * PROWS, :] = jnp.dot(
            perm.astype(BF16), hfb, preferred_element_type=F32)

    _issue_pieces(functools.partial(seg_copy, b), pieces, i)

    @pl.when(i == nt - 1)
    def _():
        @pl.when(i >= 1)
        def _():
            wait_tile(i - 1, 1 - b)
        wait_tile(i, b)


def _sort_dispatch(pieces, nch, pend, hf, meta, n_pad_rows):
    t = hf.shape[0]
    tm = TM_PROJ
    row = lambda i, *_: (i, 0)
    return pl.pallas_call(
        functools.partial(_sort_kernel, min_blocks=t * TOP_K // BM),
        grid_spec=pltpu.PrefetchScalarGridSpec(
            num_scalar_prefetch=len(pieces) + 2,
            grid=(t // tm,),
            in_specs=[
                pl.BlockSpec((tm, D_MODEL), row),
                pl.BlockSpec((tm, LANES), row),
            ],
            out_specs=pl.BlockSpec(memory_space=pl.ANY),
            scratch_shapes=[pltpu.VMEM((2, RL, D_MODEL), F32), pltpu.SemaphoreType.DMA((2,))],
        ),
        out_shape=jax.ShapeDtypeStruct((n_pad_rows, D_MODEL), F32),
        compiler_params=_cparams(("arbitrary",)),
        name="sort_dispatch",
    )(*pieces, nch, pend, hf, meta)


def _ffn_kernel(be_ref, nv_ref, nxt_ref, x_ref, wg_hbm, bg_ref, wu_hbm, bu_ref, wd_hbm, bd_ref,
                o_ref, wstage_ref, wbf_ref, sems):
    i = pl.program_id(0)
    w_hbm = (wg_hbm, wu_hbm, wd_hbm)

    def fetch(e, m):
        return pltpu.make_async_copy(w_hbm[m].at[e], wstage_ref.at[m], sems.at[m])

    @pl.when(i < nv_ref[0])
    def _():
        @pl.when(i == 0)
        def _():
            for m in range(3):
                fetch(be_ref[0], m).start()

        @pl.when(jnp.logical_or(i == 0, be_ref[i] != be_ref[jnp.maximum(i - 1, 0)]))
        def _():
            for m in range(3):
                fetch(be_ref[i], m).wait()
                wbf_ref[m] = wstage_ref[m].astype(BF16)

            @pl.when(nxt_ref[i] >= 0)
            def _():
                for m in range(3):
                    fetch(nxt_ref[i], m).start()

        x = x_ref[...].astype(BF16)
        y = jnp.broadcast_to(bd_ref[...], o_ref.shape)
        for c in range(D_MODEL // FFN_COLS):
            cs = slice(c * FFN_COLS, (c + 1) * FFN_COLS)
            g = jnp.dot(x, wbf_ref[0, :, cs], preferred_element_type=F32) + bg_ref[:, cs]
            g = jnp.minimum(g, SWIGLU_LIMIT)
            u = jnp.dot(x, wbf_ref[1, :, cs], preferred_element_type=F32) + bu_ref[:, cs]
            u = jnp.clip(u, -SWIGLU_LIMIT, SWIGLU_LIMIT)
            act = g * jax.nn.sigmoid(SWIGLU_ALPHA * g) * (u + 1.0)
            y = y + jnp.dot(act.astype(BF16), wbf_ref[2, cs, :], preferred_element_type=F32)
        o_ref[...] = y

    @pl.when(pl.program_id(0) >= nv_ref[0])
    def _():
        o_ref[...] = jnp.zeros_like(o_ref)


def _expert_ffn(block_e, nvalid, next_e, xs, wg, bg, wu, bu, wd, bd):
    n_rows = xs.shape[0]
    nb = n_rows // BM

    def blk(i, be, nv, nx):
        return jnp.minimum(i, nv[0] - 1)

    xmap = lambda i, be, nv, nx: (blk(i, be, nv, nx), 0)
    bmap = lambda i, be, nv, nx: (be[blk(i, be, nv, nx)], 0, 0)
    hbm = pl.BlockSpec(memory_space=pl.ANY)
    return pl.pallas_call(
        _ffn_kernel,
        grid_spec=pltpu.PrefetchScalarGridSpec(
            num_scalar_prefetch=3,
            grid=(nb,),
            in_specs=[
                pl.BlockSpec((BM, D_MODEL), xmap),
                hbm,
                pl.BlockSpec((None, 1, D_MODEL), bmap),
                hbm,
                pl.BlockSpec((None, 1, D_MODEL), bmap),
                hbm,
                pl.BlockSpec((None, 1, D_MODEL), bmap),
            ],
            out_specs=pl.BlockSpec((BM, D_MODEL), lambda i, be, nv, nx: (i, 0)),
            scratch_shapes=[pltpu.VMEM((3, D_MODEL, D_MODEL), F32),
                            pltpu.VMEM((3, D_MODEL, D_MODEL), BF16),
                            pltpu.SemaphoreType.DMA((3,))],
        ),
        out_shape=jax.ShapeDtypeStruct((n_rows, D_MODEL), F32),
        compiler_params=_cparams(("arbitrary",)),
        name="expert_ffn",
    )(block_e, nvalid, next_e, xs, wg, bg, wu, bu, wd, bd)


def _combine_kernel(blo_ref, bgo_ref, nbig_ref, slo_ref, sgo_ref, nsml_ref, nch_ref,
                    x1_ref, meta_ref, g_ref, yb_hbm, o_ref, ybl_ref, sems):
    pieces = (blo_ref, bgo_ref, nbig_ref, slo_ref, sgo_ref, nsml_ref)
    i = pl.program_id(0)
    nt = pl.num_programs(0)
    b = i % 2
    tm = x1_ref.shape[0]

    def seg_copy(buf, lo, go, rows):
        return pltpu.make_async_copy(
            yb_hbm.at[pl.ds(pl.multiple_of(go, SEG), rows)],
            ybl_ref.at[buf, pl.ds(pl.multiple_of(lo, SEG), rows)], sems.at[buf])

    def issue_tile(tile, buf):
        _issue_pieces(functools.partial(seg_copy, buf), pieces, tile)

    def wait_tile(tile, buf):
        _wait_rows(functools.partial(seg_copy, buf, 0, 0), nch_ref[tile])

    @pl.when(i == 0)
    def _():
        ybl_ref[...] = jnp.zeros_like(ybl_ref)
        issue_tile(0, 0)

    @pl.when(i + 1 < nt)
    def _():
        issue_tile(i + 1, 1 - b)

    wait_tile(i, b)

    meta = meta_ref[...]
    lpos_b = [jnp.broadcast_to(meta[:, TOP_K + kk:TOP_K + kk + 1], (tm, LANES))
              for kk in range(TOP_K)]
    gate_b = [jnp.broadcast_to(meta[:, 2 * TOP_K + kk:2 * TOP_K + kk + 1], (tm, LANES))
              for kk in range(TOP_K)]
    lane_f = lax.broadcasted_iota(jnp.int32, (tm, LANES), 1).astype(F32)
    cols = []
    for j in range(RL // LANES):
        r = lane_f + float(j * LANES)
        gmat = jnp.zeros((tm, LANES), F32)
        for kk in range(TOP_K):
            gmat = jnp.where(r == lpos_b[kk], gate_b[kk], gmat)
        cols.append(gmat.astype(BF16))
    gates = jnp.concatenate(cols, axis=1)
    acc = x1_ref[...] + jnp.dot(gates, ybl_ref[b].astype(BF16), preferred_element_type=F32)
    ms = jnp.mean(acc * acc, axis=-1, keepdims=True)
    o_ref[...] = acc * lax.rsqrt(ms + NORM_EPS) * g_ref[...]


def _combine(pieces, nch, x1, meta, final_g, yb):
    t = x1.shape[0]
    tm = TM_PROJ
    row = lambda i, *_: (i, 0)
    return pl.pallas_call(
        _combine_kernel,
        grid_spec=pltpu.PrefetchScalarGridSpec(
            num_scalar_prefetch=len(pieces) + 1,
            grid=(t // tm,),
            in_specs=[
                pl.BlockSpec((tm, D_MODEL), row),
                pl.BlockSpec((tm, LANES), row),
                pl.BlockSpec((1, D_MODEL), lambda i, *_: (0, 0)),
                pl.BlockSpec(memory_space=pl.ANY),
            ],
            out_specs=pl.BlockSpec((tm, D_MODEL), row),
            scratch_shapes=[pltpu.VMEM((2, RL, D_MODEL), F32), pltpu.SemaphoreType.DMA((2,))],
        ),
        out_shape=jax.ShapeDtypeStruct((t, D_MODEL), F32),
        compiler_params=_cparams(("arbitrary",)),
        name="combine",
    )(*pieces, nch, x1, meta, final_g, yb)


def _expand_pieces(cnt, lo, go, step, cap):
    cum = jnp.cumsum(cnt, axis=1)[:, None, :]
    start = cum - cnt[:, None, :]
    p = jnp.arange(cap, dtype=jnp.int32)[None, :, None]
    mine = jnp.logical_and(p >= start, p < cum)
    off = (p - start) * step
    lo_p = jnp.sum(jnp.where(mine, lo[:, None, :] + off, 0), axis=2)
    go_p = jnp.sum(jnp.where(mine, go[:, None, :] + off, 0), axis=2)
    return (lo_p.astype(jnp.int32).reshape(-1), go_p.astype(jnp.int32).reshape(-1),
            cum[:, 0, -1].astype(jnp.int32))


def _piece_lists(nseg, lo8, goff):
    nbig = nseg >> BIG_SHIFT
    nsml = nseg - (nbig << BIG_SHIFT)
    done = nbig * BIG
    return (_expand_pieces(nbig, lo8, goff, BIG, MAX_BIG)
            + _expand_pieces(nsml, lo8 + done, goff + done, SEG, MAX_SML))


def _block_diag(w):
    h, d, _ = w.shape
    eye = jnp.eye(h, dtype=w.dtype)
    return (eye[:, None, :, None] * w[:, :, None, :]).reshape(h * d, h * d)


def kernel(x, positions, attn_norm_g, w_in, conv_w, conv_b, lru_wa, lru_ba, lru_wx, lru_bx, lru_lambda, lru_norm_g, ret_norm_g, ret_norm_b, w_out, ffn_norm_g, router_w, router_b, moe_w_gate, moe_b_gate, moe_w_up, moe_b_up, moe_w_down, moe_b_down, final_norm_g):
    bsz, seq, d = x.shape
    depth = w_in.shape[0]
    t = bsz * seq
    assert depth == 1 and d == D_MODEL and seq % TC_SEQ == 0 and t % TM_PROJ == 0
    nt = t // TM_PROJ
    n_pad = t * TOP_K + nt * N_EXPERTS * SEG + N_EXPERTS * BM
    nb = n_pad // BM

    half = RET_HEAD_DIM // 2
    inv_freq = ROPE_THETA ** (-jnp.arange(half, dtype=F32) / half)
    inv_freq = jnp.concatenate([inv_freq, inv_freq]).reshape(1, RET_HEAD_DIM)
    sign = jnp.concatenate([-jnp.ones((half,), F32), jnp.ones((half,), F32)]).reshape(1, RET_HEAD_DIM)
    pos_col = positions.reshape(t, 1).astype(jnp.int32)
    tri = jnp.tril(jnp.ones((TM_PROJ, TM_PROJ), F32), -1).astype(BF16)
    upper = jnp.triu(jnp.ones((LANES, LANES), F32), 1).astype(BF16)

    x2 = x.reshape(t, d)
    for l in range(depth):
        wg_bd = jnp.concatenate([_block_diag(lru_wa[l]), _block_diag(lru_wx[l])], axis=1).astype(BF16)
        lam = lru_lambda[l].astype(F32)
        sp = (jnp.maximum(-lam, 0.0) + jnp.log1p(jnp.exp(-jnp.abs(lam)))).reshape(1, D_LRU)
        y_lru, y_ret = _mixer(
            x2, pos_col, seq, attn_norm_g[l].reshape(1, d), w_in[l].astype(BF16),
            conv_w[l], conv_b[l].reshape(1, D_LRU), wg_bd,
            lru_ba[l].reshape(1, D_LRU), lru_bx[l].reshape(1, D_LRU), sp,
            lru_norm_g[l].reshape(1, D_LRU), inv_freq, sign,
            ret_norm_g[l].reshape(1, D_RET), ret_norm_b[l].reshape(1, D_RET))

        rw_pad = jnp.zeros((d, LANES), F32).at[:, :N_EXPERTS].set(router_w[l]).astype(BF16)
        rb_pad = jnp.full((1, LANES), -1e30, F32).at[0, :N_EXPERTS].set(router_b[l])
        x1, hf, meta, cnt_tile = _outproj_router(x2, y_lru, y_ret, w_out[l].astype(BF16),
                                            ffn_norm_g[l].reshape(1, d), rw_pad, rb_pad, tri,
                                            upper)

        cnt = cnt_tile[:, 0, :N_EXPERTS].astype(jnp.int32)
        c8 = (cnt + SEG - 1) // SEG * SEG
        lo8 = jnp.cumsum(c8, axis=1) - c8
        padded = (jnp.sum(c8, axis=0) + BM - 1) // BM * BM
        pend = jnp.cumsum(padded).astype(jnp.int32)
        pstart = pend - padded
        goff = (pstart[None, :] + jnp.cumsum(c8, axis=0) - c8).astype(jnp.int32)
        nseg = (c8 // SEG).astype(jnp.int32)
        block_start = jnp.arange(nb, dtype=jnp.int32) * BM
        block_e = jnp.minimum(jnp.sum(block_start[:, None] >= pend[None, :], axis=1),
                              N_EXPERTS - 1).astype(jnp.int32)
        nvalid = (pend[N_EXPERTS - 1:] // BM).astype(jnp.int32)
        pieces = _piece_lists(nseg, lo8.astype(jnp.int32), goff)
        nch = jnp.sum(nseg, axis=1).astype(jnp.int32)

        xs = _sort_dispatch(pieces, nch, pend, hf, meta, n_pad)
        next_blk = pend[block_e] // BM
        next_e = jnp.where(next_blk < nvalid[0], block_e[jnp.minimum(next_blk, nb - 1)],
                           -1).astype(jnp.int32)
        yb = _expert_ffn(block_e, nvalid, next_e, xs,
                         moe_w_gate[l], moe_b_gate[l].reshape(N_EXPERTS, 1, d),
                         moe_w_up[l], moe_b_up[l].reshape(N_EXPERTS, 1, d),
                         moe_w_down[l], moe_b_down[l].reshape(N_EXPERTS, 1, d))
        x2 = _combine(pieces, nch, x1, meta, final_norm_g.reshape(1, d), yb)
    return x2.reshape(bsz, seq, d)
```

```python
import functools
import math

import numpy as np
import jax
import jax.numpy as jnp
from jax import lax
from jax.experimental import pallas as pl
from jax.experimental.pallas import tpu as pltpu

F32 = jnp.float32
BF16 = jnp.bfloat16

D_MODEL = 1024
D_LRU = 512
D_RET = 512
CONV_WIDTH = 4
LRU_C = 8.0
RET_HEADS = 4
RET_HEAD_DIM = D_RET // RET_HEADS
ROPE_THETA = 10000.0
D_IN = 2 * D_LRU + 4 * D_RET
N_EXPERTS = 32
TOP_K = 4
SWIGLU_ALPHA = 1.702
SWIGLU_LIMIT = 7.0
NORM_EPS = 1e-5

LANES = 128
SUBLANES = 8
VMEM_LIMIT = 48 * 1024 * 1024

TM_PROJ = 512
TC_MIX = 512
BM = 512
FFN_COLS = 1024
SEG = SUBLANES
RL = TM_PROJ * TOP_K + N_EXPERTS * SEG
PROWS = 768
BIG_SHIFT = 2
BIG = SEG << BIG_SHIFT
ISSUE_SHIFT = 2
ISSUE = 1 << ISSUE_SHIFT
MAX_BIG = RL // BIG
MAX_SML = N_EXPERTS * ((1 << BIG_SHIFT) - 1)
WAIT_SHIFT = 5
WAITBIG = SEG << WAIT_SHIFT

LOG_GAMMA = [math.log1p(-(2.0 ** (-5.0 - h))) for h in range(RET_HEADS)]


def _split_quarter_pi():
    rest = np.float64(np.pi) / 4.0
    parts = []
    for _ in range(3):
        m, e = np.frexp(rest)
        piece = np.ldexp(np.round(m * 1024.0) / 1024.0, e)
        parts.append(float(piece))
        rest = rest - piece
    parts.append(float(np.float32(rest)))
    return parts


_QPI = _split_quarter_pi()
_SIN_C = (-1.9515295891e-4, 8.3321608736e-3, -1.6666654611e-1)
_COS_C = (2.443315711809948e-5, -1.388731625493765e-3, 4.166664568298827e-2)


def _sincos(x):
    ax = jnp.abs(x)
    j = (ax * (4.0 / math.pi)).astype(jnp.int32)
    j = j + jnp.bitwise_and(j, 1)
    y = j.astype(F32)
    r = (((ax - y * _QPI[0]) - y * _QPI[1]) - y * _QPI[2]) - y * _QPI[3]
    z = r * r
    sin_p = r + r * z * ((_SIN_C[0] * z + _SIN_C[1]) * z + _SIN_C[2])
    cos_p = 1.0 - 0.5 * z + z * z * ((_COS_C[0] * z + _COS_C[1]) * z + _COS_C[2])
    quad = jnp.right_shift(j, 1)
    swap = jnp.bitwise_and(quad, 1) == 1
    sin_v = jnp.where(swap, cos_p, sin_p)
    cos_v = jnp.where(swap, sin_p, cos_p)
    sin_neg = jnp.logical_xor(jnp.bitwise_and(quad, 2) == 2, x < 0.0)
    cos_neg = jnp.bitwise_and(quad + 1, 2) == 2
    return jnp.where(sin_neg, -sin_v, sin_v), jnp.where(cos_neg, -cos_v, cos_v)


def _cparams(sem):
    return pltpu.CompilerParams(dimension_semantics=sem, vmem_limit_bytes=VMEM_LIMIT)


def _mixer_kernel(x_ref, pos_ref, ng_ref, win_ref, cw_ref, cb_ref, wg_ref, ba_ref, bx_ref,
                  sp_ref, lng_ref, invf_ref, sgn_ref, gg_ref, gb_ref,
                  ylru_ref, yret_ref,
                  proj_ref, xext_ref, h_ref, state_ref, decay_ref, *, ns):
    i = pl.program_id(0)
    tc = x_ref.shape[0]
    dh = RET_HEAD_DIM

    @pl.when(i == 0)
    def _():
        proj_ref[1] = jnp.zeros((tc, D_IN), BF16)
        ri = lax.broadcasted_iota(jnp.int32, (tc, tc), 0)
        ci = lax.broadcasted_iota(jnp.int32, (tc, tc), 1)
        rel = (ri - ci).astype(F32)
        causal = rel >= 0.0
        relc = jnp.where(causal, rel, 0.0)
        for h in range(RET_HEADS):
            decay_ref[h] = jnp.where(causal, jnp.exp(LOG_GAMMA[h] * relc), 0.0)

    @pl.when(lax.rem(jnp.maximum(i - 1, 0), ns) == 0)
    def _():
        xext_ref[0:SUBLANES, :] = jnp.zeros((SUBLANES, D_LRU), F32)
        h_ref[...] = jnp.zeros_like(h_ref)
        state_ref[...] = jnp.zeros_like(state_ref)

    x = x_ref[...]
    ms = jnp.mean(x * x, axis=-1, keepdims=True)
    hn = x * lax.rsqrt(ms + NORM_EPS) * ng_ref[...]
    proj_ref[i % 2] = jnp.dot(hn.astype(BF16), win_ref[...],
                              preferred_element_type=F32).astype(BF16)

    slot = (i + 1) % 2

    def pcol(lo, width):
        return proj_ref[slot, :, lo:lo + width]

    xext_ref[SUBLANES:SUBLANES + tc, :] = pcol(0, D_LRU).astype(F32)
    xc = cb_ref[...] + cw_ref[CONV_WIDTH - 1:CONV_WIDTH, :] * xext_ref[SUBLANES:SUBLANES + tc, :]
    for j in range(CONV_WIDTH - 1):
        off = SUBLANES - (CONV_WIDTH - 1) + j
        xc = xc + cw_ref[j:j + 1, :] * xext_ref[off:off + tc, :]
    xext_ref[0:SUBLANES, :] = xext_ref[tc:tc + SUBLANES, :]

    gates = jnp.dot(xc.astype(BF16), wg_ref[...], preferred_element_type=F32)
    r = jax.nn.sigmoid(gates[:, :D_LRU] + ba_ref[...])
    ig = jax.nn.sigmoid(gates[:, D_LRU:] + bx_ref[...])
    a = jnp.exp((-LRU_C) * r * sp_ref[...])
    z = 1.0 - a * a
    b = jnp.where(z > 0.0, z * lax.rsqrt(z), 0.0) * (ig * xc)

    ng = tc // SUBLANES
    a = a.reshape(ng, SUBLANES, D_LRU)
    b = b.reshape(ng, SUBLANES, D_LRU)
    in_group = lax.broadcasted_iota(jnp.int32, (1, SUBLANES, 1), 1)
    d = 1
    while d < SUBLANES:
        a_s = pltpu.roll(a, d, 1)
        b_s = pltpu.roll(b, d, 1)
        m = in_group >= d
        b = jnp.where(m, a * b_s + b, b)
        a = jnp.where(m, a * a_s, a)
        d *= 2
    h_prev = h_ref[0:1, :]
    groups = []
    for g in range(ng):
        hg = b[g] + a[g] * h_prev
        groups.append(hg)
        h_prev = hg[SUBLANES - 1:SUBLANES, :]
    hseq = jnp.concatenate(groups, axis=0)
    h_ref[0:1, :] = h_prev

    y = hseq * jax.nn.gelu(pcol(D_LRU, D_LRU).astype(F32))
    ms = jnp.mean(y * y, axis=-1, keepdims=True)
    ylru_ref[...] = (y * lax.rsqrt(ms + NORM_EPS) * lng_ref[...]).astype(ylru_ref.dtype)

    hr = tc // 2
    pos = pos_ref[...].astype(F32)
    first = lax.broadcasted_iota(jnp.int32, (hr, dh), 1) < dh // 2
    ang = jnp.where(first, pos[0:hr, :], pos[hr:tc, :]) * invf_ref[...]
    s2, c2 = _sincos(ang)
    c2r = pltpu.roll(c2, dh // 2, 1)
    s2r = pltpu.roll(s2, dh // 2, 1)
    cos = jnp.concatenate([jnp.where(first, c2, c2r), jnp.where(first, c2r, c2)], axis=0)
    sin = jnp.concatenate([jnp.where(first, s2, s2r), jnp.where(first, s2r, s2)], axis=0)
    sin_signed = sin * sgn_ref[...]
    rowf = lax.broadcasted_iota(jnp.int32, (tc, 1), 0).astype(F32)
    scale = dh ** -0.5
    q0, k0, v0, g0 = 2 * D_LRU, 2 * D_LRU + D_RET, 2 * D_LRU + 2 * D_RET, 2 * D_LRU + 3 * D_RET

    for h in range(RET_HEADS):
        sl = slice(h * dh, (h + 1) * dh)
        q = pcol(q0 + h * dh, dh).astype(F32)
        k = pcol(k0 + h * dh, dh).astype(F32)
        v = pcol(v0 + h * dh, dh)
        qr = q * cos + pltpu.roll(q, dh // 2, 1) * sin_signed
        kr = (k * cos + pltpu.roll(k, dh // 2, 1) * sin_signed) * scale
        qb = qr.astype(BF16)
        kb = kr.astype(BF16)
        lg = LOG_GAMMA[h]
        scores = lax.dot_general(qb, kb, (((1,), (1,)), ((), ())),
                                 preferred_element_type=F32) * decay_ref[h]
        intra = jnp.dot(scores.astype(BF16), v, preferred_element_type=F32)
        st = state_ref[h]
        xi = jnp.exp(lg * (rowf + 1.0))
        cross = jnp.dot(qb, st.astype(BF16), preferred_element_type=F32) * xi
        o = intra + cross
        zeta = jnp.exp(lg * (float(tc) - 1.0 - rowf))
        kz = (kr * zeta).astype(BF16)
        kv = lax.dot_general(kz, v, (((0,), (0,)), ((), ())), preferred_element_type=F32)
        state_ref[h] = math.exp(lg * tc) * st + kv

        mu = jnp.mean(o, axis=-1, keepdims=True)
        oc = o - mu
        var = jnp.mean(oc * oc, axis=-1, keepdims=True)
        on = oc * lax.rsqrt(var + NORM_EPS) * gg_ref[:, sl] + gb_ref[:, sl]
        gate = pcol(g0 + h * dh, dh).astype(F32)
        yret_ref[:, sl] = (gate * jax.nn.sigmoid(gate) * on).astype(yret_ref.dtype)


def _mixer(x2, pos_col, seq, norm_g, w_in_bf16, conv_w, conv_b, wg_bd, ba, bx, sp, lru_norm_g,
           inv_freq, sign, gn_g, gn_b):
    t = x2.shape[0]
    tc = TC_MIX
    n_tiles = t // tc
    const = lambda i: (0, 0)
    cur = lambda i: (jnp.minimum(i, n_tiles - 1), 0)
    prev = lambda i: (jnp.maximum(i - 1, 0), 0)
    vec = lambda n: pl.BlockSpec((1, n), const)
    return pl.pallas_call(
        functools.partial(_mixer_kernel, ns=seq // tc),
        grid=(n_tiles + 1,),
        in_specs=[
            pl.BlockSpec((tc, D_MODEL), cur),
            pl.BlockSpec((tc, 1), prev),
            vec(D_MODEL),
            pl.BlockSpec((D_MODEL, D_IN), const),
            pl.BlockSpec((CONV_WIDTH, D_LRU), const),
            vec(D_LRU),
            pl.BlockSpec((D_LRU, 2 * D_LRU), const),
            vec(D_LRU), vec(D_LRU), vec(D_LRU), vec(D_LRU),
            vec(RET_HEAD_DIM), vec(RET_HEAD_DIM),
            vec(D_RET), vec(D_RET),
        ],
        out_specs=[pl.BlockSpec((tc, D_LRU), prev), pl.BlockSpec((tc, D_RET), prev)],
        out_shape=[jax.ShapeDtypeStruct((t, D_LRU), BF16), jax.ShapeDtypeStruct((t, D_RET), BF16)],
        scratch_shapes=[
            pltpu.VMEM((2, tc, D_IN), BF16),
            pltpu.VMEM((tc + SUBLANES, D_LRU), F32),
            pltpu.VMEM((SUBLANES, D_LRU), F32),
            pltpu.VMEM((RET_HEADS, RET_HEAD_DIM, RET_HEAD_DIM), F32),
            pltpu.VMEM((RET_HEADS, tc, tc), F32),
        ],
        compiler_params=_cparams(("arbitrary",)),
        name="mixer",
    )(x2, pos_col, norm_g, w_in_bf16, conv_w, conv_b, wg_bd, ba, bx, sp, lru_norm_g,
      inv_freq, sign, gn_g, gn_b)


def _outproj_router_kernel(x_ref, yl_ref, yr_ref, wo_ref, ng_ref, rw_ref, rb_ref, tri_ref,
                           upper_ref, x1_ref, hf_ref, meta_ref, cnt_ref, logit_ref):
    i = pl.program_id(0)
    tm = x_ref.shape[0]

    @pl.when(i == 0)
    def _():
        logit_ref[1] = jnp.zeros((tm, LANES), F32)

    logits = logit_ref[(i + 1) % 2]

    y = jnp.dot(yl_ref[...], wo_ref[0:D_LRU, :], preferred_element_type=F32)
    y = y + jnp.dot(yr_ref[...], wo_ref[D_LRU:, :], preferred_element_type=F32)
    x1 = x_ref[...] + y
    x1_ref[...] = x1
    ms = jnp.mean(x1 * x1, axis=-1, keepdims=True)
    hf = x1 * lax.rsqrt(ms + NORM_EPS) * ng_ref[...]
    hf_ref[...] = hf.astype(hf_ref.dtype)

    logit_ref[i % 2] = (jnp.dot(hf.astype(BF16), rw_ref[...], preferred_element_type=F32)
                        + rb_ref[...])

    lane = lax.broadcasted_iota(jnp.int32, (tm, LANES), 1)
    lane_f = lane.astype(F32)
    work = logits
    vals, idxs = [], []
    onehot = jnp.zeros((tm, LANES), F32)
    for _ in range(TOP_K):
        m = jnp.max(work, axis=-1, keepdims=True)
        idx = jnp.min(jnp.where(work == m, lane_f, float(LANES)), axis=-1, keepdims=True)
        sel = lane_f == idx
        work = jnp.where(sel, -jnp.inf, work)
        onehot = jnp.where(sel, 1.0, onehot)
        vals.append(m)
        idxs.append(idx)
    exps = [jnp.exp(v - vals[0]) for v in vals]
    denom = exps[0] + exps[1] + exps[2] + exps[3]
    gates = [e / denom for e in exps]

    counts = jnp.sum(onehot, axis=0, keepdims=True)
    seg_rows = jnp.floor((counts + (SEG - 1.0)) * (1.0 / SEG)) * SEG
    seg_start = jnp.dot(jnp.broadcast_to(seg_rows, (SUBLANES, LANES)).astype(BF16), upper_ref[...],
                        preferred_element_type=F32)[0:1, :]
    pos_mat = jnp.dot(tri_ref[...], onehot.astype(BF16), preferred_element_type=F32) + seg_start
    meta = jnp.zeros((tm, LANES), F32)
    for kk in range(TOP_K):
        pos = jnp.sum(jnp.where(lane_f == idxs[kk], pos_mat, 0.0), axis=-1, keepdims=True)
        meta = jnp.where(lane == kk, idxs[kk], meta)
        meta = jnp.where(lane == TOP_K + kk, pos, meta)
        meta = jnp.where(lane == 2 * TOP_K + kk, gates[kk], meta)
    meta_ref[...] = meta
    cnt_ref[...] = jnp.broadcast_to(counts, cnt_ref.shape)


def _outproj_router(x2, y_lru, y_ret, w_out_bf16, ng, rw_pad, rb_pad, tri, upper):
    t = x2.shape[0]
    tm = TM_PROJ
    n_tiles = t // tm
    const = lambda i: (0, 0)
    cur = lambda i: (jnp.minimum(i, n_tiles - 1), 0)
    prev = lambda i: (jnp.maximum(i - 1, 0), 0)
    return pl.pallas_call(
        _outproj_router_kernel,
        grid=(n_tiles + 1,),
        in_specs=[
            pl.BlockSpec((tm, D_MODEL), cur),
            pl.BlockSpec((tm, D_LRU), cur),
            pl.BlockSpec((tm, D_RET), cur),
            pl.BlockSpec((D_MODEL, D_MODEL), const),
            pl.BlockSpec((1, D_MODEL), const),
            pl.BlockSpec((D_MODEL, LANES), const),
            pl.BlockSpec((1, LANES), const),
            pl.BlockSpec((tm, tm), const),
            pl.BlockSpec((LANES, LANES), const),
        ],
        out_specs=[
            pl.BlockSpec((tm, D_MODEL), cur),
            pl.BlockSpec((tm, D_MODEL), cur),
            pl.BlockSpec((tm, LANES), prev),
            pl.BlockSpec((None, SUBLANES, LANES), lambda i: (jnp.maximum(i - 1, 0), 0, 0)),
        ],
        out_shape=[
            jax.ShapeDtypeStruct((t, D_MODEL), F32),
            jax.ShapeDtypeStruct((t, D_MODEL), BF16),
            jax.ShapeDtypeStruct((t, LANES), F32),
            jax.ShapeDtypeStruct((n_tiles, SUBLANES, LANES), F32),
        ],
        scratch_shapes=[pltpu.VMEM((2, tm, LANES), F32)],
        compiler_params=_cparams(("arbitrary",)),
        name="outproj_router",
    )(x2, y_lru, y_ret, w_out_bf16, ng, rw_pad, rb_pad, tri, upper)


def _issue_pieces(copy, pieces, tile):
    blo_ref, bgo_ref, nbig_ref, slo_ref, sgo_ref, nsml_ref = pieces

    def start_list(lo_ref, go_ref, base, n, rows):
        groups = lax.shift_right_logical(n, ISSUE_SHIFT)

        def group(p, c):
            for u in range(ISSUE):
                q = base + p * ISSUE + u
                copy(lo_ref[q], go_ref[q], rows).start()
            return c

        lax.fori_loop(0, groups, group, 0)

        def single(p, c):
            copy(lo_ref[base + p], go_ref[base + p], rows).start()
            return c

        lax.fori_loop(lax.shift_left(groups, ISSUE_SHIFT), n, single, 0)

    start_list(blo_ref, bgo_ref, tile * MAX_BIG, nbig_ref[tile], BIG)
    start_list(slo_ref, sgo_ref, tile * MAX_SML, nsml_ref[tile], SEG)


def _wait_rows(copy, nseg):
    nbig = lax.shift_right_logical(nseg, WAIT_SHIFT)

    def big(j, c):
        copy(WAITBIG).wait()
        return c

    lax.fori_loop(0, nbig, big, 0)

    def small(j, c):
        copy(SEG).wait()
        return c

    lax.fori_loop(0, nseg - lax.shift_left(nbig, WAIT_SHIFT), small, 0)


def _sort_kernel(blo_ref, bgo_ref, nbig_ref, slo_ref, sgo_ref, nsml_ref, nch_ref, pend_ref,
                 hf_ref, meta_ref, xs_hbm, sorted_ref, sems, *, min_blocks):
    pieces = (blo_ref, bgo_ref, nbig_ref, slo_ref, sgo_ref, nsml_ref)
    i = pl.program_id(0)
    nt = pl.num_programs(0)
    b = i % 2
    tm = hf_ref.shape[0]
    nb = xs_hbm.shape[0] // BM

    def seg_copy(buf, lo, go, rows):
        return pltpu.make_async_copy(
            sorted_ref.at[buf, pl.ds(pl.multiple_of(lo, SEG), rows)],
            xs_hbm.at[pl.ds(pl.multiple_of(go, SEG), rows)], sems.at[buf])

    def wait_tile(tile, buf):
        _wait_rows(functools.partial(seg_copy, buf, 0, 0), nch_ref[tile])

    @pl.when(i == 0)
    def _():
        sorted_ref[1, 0:BM, :] = jnp.zeros((BM, D_MODEL), F32)

        def zcopy(start):
            return pltpu.make_async_copy(
                sorted_ref.at[1, pl.ds(0, BM)],
                xs_hbm.at[pl.ds(pl.multiple_of(start, BM), BM)], sems.at[1])

        for e in range(N_EXPERTS):
            zcopy(jnp.maximum(pend_ref[e] - BM, 0)).start()
        for e in range(N_EXPERTS):
            zcopy(jnp.maximum(pend_ref[e] - BM, 0)).wait()
        for blk in range(min_blocks, nb):
            @pl.when(blk * BM >= pend_ref[N_EXPERTS - 1])
            def _():
                zcopy(blk * BM).start()
                zcopy(blk * BM).wait()

    @pl.when(i >= 2)
    def _():
        wait_tile(i - 2, b)

    meta_t = meta_ref[...].T
    hfb = hf_ref[...]
    for j in range(RL // PROWS):
        r = (lax.broadcasted_iota(jnp.int32, (PROWS, tm), 0) + j * PROWS).astype(F32)
        perm = jnp.zeros((PROWS, tm), F32)
        for kk in range(TOP_K):
            perm = jnp.where(r == meta_t[TOP_K + kk:TOP_K + kk + 1, :], 1.0, perm)
        sorted_ref[b, j * PROWS:(j + 1) * PROWS, :] = jnp.dot(
            perm.astype(BF16), hfb, preferred_element_type=F32)

    _issue_pieces(functools.partial(seg_copy, b), pieces, i)

    @pl.when(i == nt - 1)
    def _():
        @pl.when(i >= 1)
        def _():
            wait_tile(i - 1, 1 - b)
        wait_tile(i, b)


def _sort_dispatch(pieces, nch, pend, hf, meta, n_pad_rows):
    t = hf.shape[0]
    tm = TM_PROJ
    row = lambda i, *_: (i, 0)
    return pl.pallas_call(
        functools.partial(_sort_kernel, min_blocks=t * TOP_K // BM),
        grid_spec=pltpu.PrefetchScalarGridSpec(
            num_scalar_prefetch=len(pieces) + 2,
            grid=(t // tm,),
            in_specs=[
                pl.BlockSpec((tm, D_MODEL), row),
                pl.BlockSpec((tm, LANES), row),
            ],
            out_specs=pl.BlockSpec(memory_space=pl.ANY),
            scratch_shapes=[pltpu.VMEM((2, RL, D_MODEL), F32), pltpu.SemaphoreType.DMA((2,))],
        ),
        out_shape=jax.ShapeDtypeStruct((n_pad_rows, D_MODEL), F32),
        compiler_params=_cparams(("arbitrary",)),
        name="sort_dispatch",
    )(*pieces, nch, pend, hf, meta)


def _ffn_kernel(be_ref, nv_ref, nxt_ref, x_ref, wg_hbm, bg_ref, wu_hbm, bu_ref, wd_hbm, bd_ref,
                o_ref, wstage_ref, wbf_ref, sems):
    i = pl.program_id(0)
    w_hbm = (wg_hbm, wu_hbm, wd_hbm)

    def fetch(e, m):
        return pltpu.make_async_copy(w_hbm[m].at[e], wstage_ref.at[m], sems.at[m])

    @pl.when(i < nv_ref[0])
    def _():
        @pl.when(i == 0)
        def _():
            for m in range(3):
                fetch(be_ref[0], m).start()

        @pl.when(jnp.logical_or(i == 0, be_ref[i] != be_ref[jnp.maximum(i - 1, 0)]))
        def _():
            for m in range(3):
                fetch(be_ref[i], m).wait()
                wbf_ref[m] = wstage_ref[m].astype(BF16)

            @pl.when(nxt_ref[i] >= 0)
            def _():
                for m in range(3):
                    fetch(nxt_ref[i], m).start()

        x = x_ref[...].astype(BF16)
        y = jnp.broadcast_to(bd_ref[...], o_ref.shape)
        for c in range(D_MODEL // FFN_COLS):
            cs = slice(c * FFN_COLS, (c + 1) * FFN_COLS)
            g = jnp.dot(x, wbf_ref[0, :, cs], preferred_element_type=F32) + bg_ref[:, cs]
            g = jnp.minimum(g, SWIGLU_LIMIT)
            u = jnp.dot(x, wbf_ref[1, :, cs], preferred_element_type=F32) + bu_ref[:, cs]
            u = jnp.clip(u, -SWIGLU_LIMIT, SWIGLU_LIMIT)
            act = g * jax.nn.sigmoid(SWIGLU_ALPHA * g) * (u + 1.0)
            y = y + jnp.dot(act.astype(BF16), wbf_ref[2, cs, :], preferred_element_type=F32)
        o_ref[...] = y

    @pl.when(pl.program_id(0) >= nv_ref[0])
    def _():
        o_ref[...] = jnp.zeros_like(o_ref)


def _expert_ffn(block_e, nvalid, next_e, xs, wg, bg, wu, bu, wd, bd):
    n_rows = xs.shape[0]
    nb = n_rows // BM

    def blk(i, be, nv, nx):
        return jnp.minimum(i, nv[0] - 1)

    xmap = lambda i, be, nv, nx: (blk(i, be, nv, nx), 0)
    bmap = lambda i, be, nv, nx: (be[blk(i, be, nv, nx)], 0, 0)
    hbm = pl.BlockSpec(memory_space=pl.ANY)
    return pl.pallas_call(
        _ffn_kernel,
        grid_spec=pltpu.PrefetchScalarGridSpec(
            num_scalar_prefetch=3,
            grid=(nb,),
            in_specs=[
                pl.BlockSpec((BM, D_MODEL), xmap),
                hbm,
                pl.BlockSpec((None, 1, D_MODEL), bmap),
                hbm,
                pl.BlockSpec((None, 1, D_MODEL), bmap),
                hbm,
                pl.BlockSpec((None, 1, D_MODEL), bmap),
            ],
            out_specs=pl.BlockSpec((BM, D_MODEL), lambda i, be, nv, nx: (i, 0)),
            scratch_shapes=[pltpu.VMEM((3, D_MODEL, D_MODEL), F32),
                            pltpu.VMEM((3, D_MODEL, D_MODEL), BF16),
                            pltpu.SemaphoreType.DMA((3,))],
        ),
        out_shape=jax.ShapeDtypeStruct((n_rows, D_MODEL), F32),
        compiler_params=_cparams(("arbitrary",)),
        name="expert_ffn",
    )(block_e, nvalid, next_e, xs, wg, bg, wu, bu, wd, bd)


def _combine_kernel(blo_ref, bgo_ref, nbig_ref, slo_ref, sgo_ref, nsml_ref, nch_ref,
                    x1_ref, meta_ref, g_ref, yb_hbm, o_ref, ybl_ref, sems):
    pieces = (blo_ref, bgo_ref, nbig_ref, slo_ref, sgo_ref, nsml_ref)
    i = pl.program_id(0)
    nt = pl.num_programs(0)
    b = i % 2
    tm = x1_ref.shape[0]

    def seg_copy(buf, lo, go, rows):
        return pltpu.make_async_copy(
            yb_hbm.at[pl.ds(pl.multiple_of(go, SEG), rows)],
            ybl_ref.at[buf, pl.ds(pl.multiple_of(lo, SEG), rows)], sems.at[buf])

    def issue_tile(tile, buf):
        _issue_pieces(functools.partial(seg_copy, buf), pieces, tile)

    def wait_tile(tile, buf):
        _wait_rows(functools.partial(seg_copy, buf, 0, 0), nch_ref[tile])

    @pl.when(i == 0)
    def _():
        ybl_ref[...] = jnp.zeros_like(ybl_ref)
        issue_tile(0, 0)

    @pl.when(i + 1 < nt)
    def _():
        issue_tile(i + 1, 1 - b)

    wait_tile(i, b)

    meta = meta_ref[...]
    lpos_b = [jnp.broadcast_to(meta[:, TOP_K + kk:TOP_K + kk + 1], (tm, LANES))
              for kk in range(TOP_K)]
    gate_b = [jnp.broadcast_to(meta[:, 2 * TOP_K + kk:2 * TOP_K + kk + 1], (tm, LANES))
              for kk in range(TOP_K)]
    lane_f = lax.broadcasted_iota(jnp.int32, (tm, LANES), 1).astype(F32)
    cols = []
    for j in range(RL // LANES):
        r = lane_f + float(j * LANES)
        gmat = jnp.zeros((tm, LANES), F32)
        for kk in range(TOP_K):
            gmat = jnp.where(r == lpos_b[kk], gate_b[kk], gmat)
        cols.append(gmat.astype(BF16))
    gates = jnp.concatenate(cols, axis=1)
    acc = x1_ref[...] + jnp.dot(gates, ybl_ref[b].astype(BF16), preferred_element_type=F32)
    ms = jnp.mean(acc * acc, axis=-1, keepdims=True)
    o_ref[...] = acc * lax.rsqrt(ms + NORM_EPS) * g_ref[...]


def _combine(pieces, nch, x1, meta, final_g, yb):
    t = x1.shape[0]
    tm = TM_PROJ
    row = lambda i, *_: (i, 0)
    return pl.pallas_call(
        _combine_kernel,
        grid_spec=pltpu.PrefetchScalarGridSpec(
            num_scalar_prefetch=len(pieces) + 1,
            grid=(t // tm,),
            in_specs=[
                pl.BlockSpec((tm, D_MODEL), row),
                pl.BlockSpec((tm, LANES), row),
                pl.BlockSpec((1, D_MODEL), lambda i, *_: (0, 0)),
                pl.BlockSpec(memory_space=pl.ANY),
            ],
            out_specs=pl.BlockSpec((tm, D_MODEL), row),
            scratch_shapes=[pltpu.VMEM((2, RL, D_MODEL), F32), pltpu.SemaphoreType.DMA((2,))],
        ),
        out_shape=jax.ShapeDtypeStruct((t, D_MODEL), F32),
        compiler_params=_cparams(("arbitrary",)),
        name="combine",
    )(*pieces, nch, x1, meta, final_g, yb)


def _expand_pieces(cnt, lo, go, step, cap):
    cum = jnp.cumsum(cnt, axis=1)[:, None, :]
    start = cum - cnt[:, None, :]
    p = jnp.arange(cap, dtype=jnp.int32)[None, :, None]
    mine = jnp.logical_and(p >= start, p < cum)
    off = (p - start) * step
    lo_p = jnp.sum(jnp.where(mine, lo[:, None, :] + off, 0), axis=2)
    go_p = jnp.sum(jnp.where(mine, go[:, None, :] + off, 0), axis=2)
    return (lo_p.astype(jnp.int32).reshape(-1), go_p.astype(jnp.int32).reshape(-1),
            cum[:, 0, -1].astype(jnp.int32))


def _piece_lists(nseg, lo8, goff):
    nbig = nseg >> BIG_SHIFT
    nsml = nseg - (nbig << BIG_SHIFT)
    done = nbig * BIG
    return (_expand_pieces(nbig, lo8, goff, BIG, MAX_BIG)
            + _expand_pieces(nsml, lo8 + done, goff + done, SEG, MAX_SML))


def _block_diag(w):
    h, d, _ = w.shape
    eye = jnp.eye(h, dtype=w.dtype)
    return (eye[:, None, :, None] * w[:, :, None, :]).reshape(h * d, h * d)


def kernel(x, positions, attn_norm_g, w_in, conv_w, conv_b, lru_wa, lru_ba, lru_wx, lru_bx, lru_lambda, lru_norm_g, ret_norm_g, ret_norm_b, w_out, ffn_norm_g, router_w, router_b, moe_w_gate, moe_b_gate, moe_w_up, moe_b_up, moe_w_down, moe_b_down, final_norm_g):
    bsz, seq, d = x.shape
    depth = w_in.shape[0]
    t = bsz * seq
    assert depth == 1 and d == D_MODEL and seq % TC_MIX == 0 and t % TM_PROJ == 0
    nt = t // TM_PROJ
    n_pad = t * TOP_K + nt * N_EXPERTS * SEG + N_EXPERTS * BM
    nb = n_pad // BM

    half = RET_HEAD_DIM // 2
    inv_freq = ROPE_THETA ** (-jnp.arange(half, dtype=F32) / half)
    inv_freq = jnp.concatenate([inv_freq, inv_freq]).reshape(1, RET_HEAD_DIM)
    sign = jnp.concatenate([-jnp.ones((half,), F32), jnp.ones((half,), F32)]).reshape(1, RET_HEAD_DIM)
    pos_col = positions.reshape(t, 1).astype(jnp.int32)
    tri = jnp.tril(jnp.ones((TM_PROJ, TM_PROJ), F32), -1).astype(BF16)
    upper = jnp.triu(jnp.ones((LANES, LANES), F32), 1).astype(BF16)

    x2 = x.reshape(t, d)
    for l in range(depth):
        wg_bd = jnp.concatenate([_block_diag(lru_wa[l]), _block_diag(lru_wx[l])], axis=1).astype(BF16)
        lam = lru_lambda[l].astype(F32)
        sp = (jnp.maximum(-lam, 0.0) + jnp.log1p(jnp.exp(-jnp.abs(lam)))).reshape(1, D_LRU)
        y_lru, y_ret = _mixer(
            x2, pos_col, seq, attn_norm_g[l].reshape(1, d), w_in[l].astype(BF16),
            conv_w[l], conv_b[l].reshape(1, D_LRU), wg_bd,
            lru_ba[l].reshape(1, D_LRU), lru_bx[l].reshape(1, D_LRU), sp,
            lru_norm_g[l].reshape(1, D_LRU), inv_freq, sign,
            ret_norm_g[l].reshape(1, D_RET), ret_norm_b[l].reshape(1, D_RET))

        rw_pad = jnp.zeros((d, LANES), F32).at[:, :N_EXPERTS].set(router_w[l]).astype(BF16)
        rb_pad = jnp.full((1, LANES), -1e30, F32).at[0, :N_EXPERTS].set(router_b[l])
        x1, hf, meta, cnt_tile = _outproj_router(x2, y_lru, y_ret, w_out[l].astype(BF16),
                                            ffn_norm_g[l].reshape(1, d), rw_pad, rb_pad, tri,
                                            upper)

        cnt = cnt_tile[:, 0, :N_EXPERTS].astype(jnp.int32)
        c8 = (cnt + SEG - 1) // SEG * SEG
        lo8 = jnp.cumsum(c8, axis=1) - c8
        padded = (jnp.sum(c8, axis=0) + BM - 1) // BM * BM
        pend = jnp.cumsum(padded).astype(jnp.int32)
        pstart = pend - padded
        goff = (pstart[None, :] + jnp.cumsum(c8, axis=0) - c8).astype(jnp.int32)
        nseg = (c8 // SEG).astype(jnp.int32)
        block_start = jnp.arange(nb, dtype=jnp.int32) * BM
        block_e = jnp.minimum(jnp.sum(block_start[:, None] >= pend[None, :], axis=1),
                              N_EXPERTS - 1).astype(jnp.int32)
        nvalid = (pend[N_EXPERTS - 1:] // BM).astype(jnp.int32)
        pieces = _piece_lists(nseg, lo8.astype(jnp.int32), goff)
        nch = jnp.sum(nseg, axis=1).astype(jnp.int32)

        xs = _sort_dispatch(pieces, nch, pend, hf, meta, n_pad)
        next_blk = pend[block_e] // BM
        next_e = jnp.where(next_blk < nvalid[0], block_e[jnp.minimum(next_blk, nb - 1)],
                           -1).astype(jnp.int32)
        yb = _expert_ffn(block_e, nvalid, next_e, xs,
                         moe_w_gate[l], moe_b_gate[l].reshape(N_EXPERTS, 1, d),
                         moe_w_up[l], moe_b_up[l].reshape(N_EXPERTS, 1, d),
                         moe_w_down[l], moe_b_down[l].reshape(N_EXPERTS, 1, d))
        x2 = _combine(pieces, nch, x1, meta, final_norm_g.reshape(1, d), yb)
    return x2.reshape(bsz, seq, d)
```

```python
import functools
import math

import numpy as np
import jax
import jax.numpy as jnp
from jax import lax
from jax.experimental import pallas as pl
from jax.experimental.pallas import tpu as pltpu

F32 = jnp.float32
BF16 = jnp.bfloat16

D_MODEL = 1024
D_LRU = 512
D_RET = 512
CONV_WIDTH = 4
LRU_C = 8.0
RET_HEADS = 4
RET_HEAD_DIM = D_RET // RET_HEADS
ROPE_THETA = 10000.0
D_IN = 2 * D_LRU + 4 * D_RET
N_EXPERTS = 32
TOP_K = 4
SWIGLU_ALPHA = 1.702
SWIGLU_LIMIT = 7.0
NORM_EPS = 1e-5

LANES = 128
SUBLANES = 8
VMEM_LIMIT = 48 * 1024 * 1024

TM_PROJ = 512
TC_MIX = 512
BM = 512
FFN_ROWS = 128
SEG = SUBLANES
RL = TM_PROJ * TOP_K + N_EXPERTS * SEG
PROWS = 768
BIG_SHIFT = 2
BIG = SEG << BIG_SHIFT
ISSUE_SHIFT = 2
ISSUE = 1 << ISSUE_SHIFT
MAX_BIG = RL // BIG
MAX_SML = N_EXPERTS * ((1 << BIG_SHIFT) - 1)
WAIT_SHIFT = 5
WAITBIG = SEG << WAIT_SHIFT

LOG_GAMMA = [math.log1p(-(2.0 ** (-5.0 - h))) for h in range(RET_HEADS)]


def _split_quarter_pi():
    rest = np.float64(np.pi) / 4.0
    parts = []
    for _ in range(3):
        m, e = np.frexp(rest)
        piece = np.ldexp(np.round(m * 1024.0) / 1024.0, e)
        parts.append(float(piece))
        rest = rest - piece
    parts.append(float(np.float32(rest)))
    return parts


_QPI = _split_quarter_pi()
_SIN_C = (-1.9515295891e-4, 8.3321608736e-3, -1.6666654611e-1)
_COS_C = (2.443315711809948e-5, -1.388731625493765e-3, 4.166664568298827e-2)


def _sincos(x):
    ax = jnp.abs(x)
    j = (ax * (4.0 / math.pi)).astype(jnp.int32)
    j = j + jnp.bitwise_and(j, 1)
    y = j.astype(F32)
    r = (((ax - y * _QPI[0]) - y * _QPI[1]) - y * _QPI[2]) - y * _QPI[3]
    z = r * r
    sin_p = r + r * z * ((_SIN_C[0] * z + _SIN_C[1]) * z + _SIN_C[2])
    cos_p = 1.0 - 0.5 * z + z * z * ((_COS_C[0] * z + _COS_C[1]) * z + _COS_C[2])
    quad = jnp.right_shift(j, 1)
    swap = jnp.bitwise_and(quad, 1) == 1
    sin_v = jnp.where(swap, cos_p, sin_p)
    cos_v = jnp.where(swap, sin_p, cos_p)
    sin_neg = jnp.logical_xor(jnp.bitwise_and(quad, 2) == 2, x < 0.0)
    cos_neg = jnp.bitwise_and(quad + 1, 2) == 2
    return jnp.where(sin_neg, -sin_v, sin_v), jnp.where(cos_neg, -cos_v, cos_v)


def _cparams(sem):
    return pltpu.CompilerParams(dimension_semantics=sem, vmem_limit_bytes=VMEM_LIMIT)


def _mixer_kernel(x_ref, pos_ref, ng_ref, win_ref, cw_ref, cb_ref, wg_ref, ba_ref, bx_ref,
                  sp_ref, lng_ref, invf_ref, sgn_ref, gg_ref, gb_ref,
                  ylru_ref, yret_ref,
                  proj_ref, xext_ref, h_ref, state_ref, decay_ref, *, ns):
    i = pl.program_id(0)
    tc = x_ref.shape[0]
    dh = RET_HEAD_DIM

    @pl.when(i == 0)
    def _():
        proj_ref[1] = jnp.zeros((tc, D_IN), BF16)
        ri = lax.broadcasted_iota(jnp.int32, (tc, tc), 0)
        ci = lax.broadcasted_iota(jnp.int32, (tc, tc), 1)
        rel = (ri - ci).astype(F32)
        causal = rel >= 0.0
        relc = jnp.where(causal, rel, 0.0)
        for h in range(RET_HEADS):
            decay_ref[h] = jnp.where(causal, jnp.exp(LOG_GAMMA[h] * relc), 0.0)

    @pl.when(lax.rem(jnp.maximum(i - 1, 0), ns) == 0)
    def _():
        xext_ref[0:SUBLANES, :] = jnp.zeros((SUBLANES, D_LRU), F32)
        h_ref[...] = jnp.zeros_like(h_ref)
        state_ref[...] = jnp.zeros_like(state_ref)

    x = x_ref[...]
    ms = jnp.mean(x * x, axis=-1, keepdims=True)
    hn = x * lax.rsqrt(ms + NORM_EPS) * ng_ref[...]
    proj_ref[i % 2] = jnp.dot(hn.astype(BF16), win_ref[...],
                              preferred_element_type=F32).astype(BF16)

    slot = (i + 1) % 2

    def pcol(lo, width):
        return proj_ref[slot, :, lo:lo + width]

    xext_ref[SUBLANES:SUBLANES + tc, :] = pcol(0, D_LRU).astype(F32)
    xc = cb_ref[...] + cw_ref[CONV_WIDTH - 1:CONV_WIDTH, :] * xext_ref[SUBLANES:SUBLANES + tc, :]
    for j in range(CONV_WIDTH - 1):
        off = SUBLANES - (CONV_WIDTH - 1) + j
        xc = xc + cw_ref[j:j + 1, :] * xext_ref[off:off + tc, :]
    xext_ref[0:SUBLANES, :] = xext_ref[tc:tc + SUBLANES, :]

    gates = jnp.dot(xc.astype(BF16), wg_ref[...], preferred_element_type=F32)
    r = jax.nn.sigmoid(gates[:, :D_LRU] + ba_ref[...])
    ig = jax.nn.sigmoid(gates[:, D_LRU:] + bx_ref[...])
    a = jnp.exp((-LRU_C) * r * sp_ref[...])
    z = 1.0 - a * a
    b = jnp.where(z > 0.0, z * lax.rsqrt(z), 0.0) * (ig * xc)

    ng = tc // SUBLANES
    a = a.reshape(ng, SUBLANES, D_LRU)
    b = b.reshape(ng, SUBLANES, D_LRU)
    in_group = lax.broadcasted_iota(jnp.int32, (1, SUBLANES, 1), 1)
    d = 1
    while d < SUBLANES:
        a_s = pltpu.roll(a, d, 1)
        b_s = pltpu.roll(b, d, 1)
        m = in_group >= d
        b = jnp.where(m, a * b_s + b, b)
        a = jnp.where(m, a * a_s, a)
        d *= 2
    h_prev = h_ref[0:1, :]
    groups = []
    for g in range(ng):
        hg = b[g] + a[g] * h_prev
        groups.append(hg)
        h_prev = hg[SUBLANES - 1:SUBLANES, :]
    hseq = jnp.concatenate(groups, axis=0)
    h_ref[0:1, :] = h_prev

    y = hseq * jax.nn.gelu(pcol(D_LRU, D_LRU).astype(F32))
    ms = jnp.mean(y * y, axis=-1, keepdims=True)
    ylru_ref[...] = (y * lax.rsqrt(ms + NORM_EPS) * lng_ref[...]).astype(ylru_ref.dtype)

    hr = tc // 2
    pos = pos_ref[...].astype(F32)
    first = lax.broadcasted_iota(jnp.int32, (hr, dh), 1) < dh // 2
    ang = jnp.where(first, pos[0:hr, :], pos[hr:tc, :]) * invf_ref[...]
    s2, c2 = _sincos(ang)
    c2r = pltpu.roll(c2, dh // 2, 1)
    s2r = pltpu.roll(s2, dh // 2, 1)
    cos = jnp.concatenate([jnp.where(first, c2, c2r), jnp.where(first, c2r, c2)], axis=0)
    sin = jnp.concatenate([jnp.where(first, s2, s2r), jnp.where(first, s2r, s2)], axis=0)
    sin_signed = sin * sgn_ref[...]
    rowf = lax.broadcasted_iota(jnp.int32, (tc, 1), 0).astype(F32)
    scale = dh ** -0.5
    q0, k0, v0, g0 = 2 * D_LRU, 2 * D_LRU + D_RET, 2 * D_LRU + 2 * D_RET, 2 * D_LRU + 3 * D_RET

    for h in range(RET_HEADS):
        sl = slice(h * dh, (h + 1) * dh)
        q = pcol(q0 + h * dh, dh).astype(F32)
        k = pcol(k0 + h * dh, dh).astype(F32)
        v = pcol(v0 + h * dh, dh)
        qr = q * cos + pltpu.roll(q, dh // 2, 1) * sin_signed
        kr = (k * cos + pltpu.roll(k, dh // 2, 1) * sin_signed) * scale
        qb = qr.astype(BF16)
        kb = kr.astype(BF16)
        lg = LOG_GAMMA[h]
        scores = lax.dot_general(qb, kb, (((1,), (1,)), ((), ())),
                                 preferred_element_type=F32) * decay_ref[h]
        intra = jnp.dot(scores.astype(BF16), v, preferred_element_type=F32)
        st = state_ref[h]
        xi = jnp.exp(lg * (rowf + 1.0))
        cross = jnp.dot(qb, st.astype(BF16), preferred_element_type=F32) * xi
        o = intra + cross
        zeta = jnp.exp(lg * (float(tc) - 1.0 - rowf))
        kz = (kr * zeta).astype(BF16)
        kv = lax.dot_general(kz, v, (((0,), (0,)), ((), ())), preferred_element_type=F32)
        state_ref[h] = math.exp(lg * tc) * st + kv

        mu = jnp.mean(o, axis=-1, keepdims=True)
        oc = o - mu
        var = jnp.mean(oc * oc, axis=-1, keepdims=True)
        on = oc * lax.rsqrt(var + NORM_EPS) * gg_ref[:, sl] + gb_ref[:, sl]
        gate = pcol(g0 + h * dh, dh).astype(F32)
        yret_ref[:, sl] = (gate * jax.nn.sigmoid(gate) * on).astype(yret_ref.dtype)


def _mixer(x2, pos_col, seq, norm_g, w_in_bf16, conv_w, conv_b, wg_bd, ba, bx, sp, lru_norm_g,
           inv_freq, sign, gn_g, gn_b):
    t = x2.shape[0]
    tc = TC_MIX
    n_tiles = t // tc
    const = lambda i: (0, 0)
    cur = lambda i: (jnp.minimum(i, n_tiles - 1), 0)
    prev = lambda i: (jnp.maximum(i - 1, 0), 0)
    vec = lambda n: pl.BlockSpec((1, n), const)
    return pl.pallas_call(
        functools.partial(_mixer_kernel, ns=seq // tc),
        grid=(n_tiles + 1,),
        in_specs=[
            pl.BlockSpec((tc, D_MODEL), cur),
            pl.BlockSpec((tc, 1), prev),
            vec(D_MODEL),
            pl.BlockSpec((D_MODEL, D_IN), const),
            pl.BlockSpec((CONV_WIDTH, D_LRU), const),
            vec(D_LRU),
            pl.BlockSpec((D_LRU, 2 * D_LRU), const),
            vec(D_LRU), vec(D_LRU), vec(D_LRU), vec(D_LRU),
            vec(RET_HEAD_DIM), vec(RET_HEAD_DIM),
            vec(D_RET), vec(D_RET),
        ],
        out_specs=[pl.BlockSpec((tc, D_LRU), prev), pl.BlockSpec((tc, D_RET), prev)],
        out_shape=[jax.ShapeDtypeStruct((t, D_LRU), BF16), jax.ShapeDtypeStruct((t, D_RET), BF16)],
        scratch_shapes=[
            pltpu.VMEM((2, tc, D_IN), BF16),
            pltpu.VMEM((tc + SUBLANES, D_LRU), F32),
            pltpu.VMEM((SUBLANES, D_LRU), F32),
            pltpu.VMEM((RET_HEADS, RET_HEAD_DIM, RET_HEAD_DIM), F32),
            pltpu.VMEM((RET_HEADS, tc, tc), F32),
        ],
        compiler_params=_cparams(("arbitrary",)),
        name="mixer",
    )(x2, pos_col, norm_g, w_in_bf16, conv_w, conv_b, wg_bd, ba, bx, sp, lru_norm_g,
      inv_freq, sign, gn_g, gn_b)


def _outproj_router_kernel(x_ref, yl_ref, yr_ref, wo_ref, ng_ref, rw_ref, rb_ref, tri_ref,
                           upper_ref, x1_ref, hf_ref, meta_ref, cnt_ref, logit_ref):
    i = pl.program_id(0)
    tm = x_ref.shape[0]

    @pl.when(i == 0)
    def _():
        logit_ref[1] = jnp.zeros((tm, LANES), F32)

    logits = logit_ref[(i + 1) % 2]

    y = jnp.dot(yl_ref[...], wo_ref[0:D_LRU, :], preferred_element_type=F32)
    y = y + jnp.dot(yr_ref[...], wo_ref[D_LRU:, :], preferred_element_type=F32)
    x1 = x_ref[...] + y
    x1_ref[...] = x1
    ms = jnp.mean(x1 * x1, axis=-1, keepdims=True)
    hf = x1 * lax.rsqrt(ms + NORM_EPS) * ng_ref[...]
    hf_ref[...] = hf.astype(hf_ref.dtype)

    logit_ref[i % 2] = (jnp.dot(hf.astype(BF16), rw_ref[...], preferred_element_type=F32)
                        + rb_ref[...])

    lane = lax.broadcasted_iota(jnp.int32, (tm, LANES), 1)
    lane_f = lane.astype(F32)
    work = logits
    vals, idxs = [], []
    onehot = jnp.zeros((tm, LANES), F32)
    for _ in range(TOP_K):
        m = jnp.max(work, axis=-1, keepdims=True)
        idx = jnp.min(jnp.where(work == m, lane_f, float(LANES)), axis=-1, keepdims=True)
        sel = lane_f == idx
        work = jnp.where(sel, -jnp.inf, work)
        onehot = jnp.where(sel, 1.0, onehot)
        vals.append(m)
        idxs.append(idx)
    exps = [jnp.exp(v - vals[0]) for v in vals]
    denom = exps[0] + exps[1] + exps[2] + exps[3]
    gates = [e / denom for e in exps]

    counts = jnp.sum(onehot, axis=0, keepdims=True)
    seg_rows = jnp.floor((counts + (SEG - 1.0)) * (1.0 / SEG)) * SEG
    seg_start = jnp.dot(jnp.broadcast_to(seg_rows, (SUBLANES, LANES)).astype(BF16), upper_ref[...],
                        preferred_element_type=F32)[0:1, :]
    pos_mat = jnp.dot(tri_ref[...], onehot.astype(BF16), preferred_element_type=F32) + seg_start
    meta = jnp.zeros((tm, LANES), F32)
    for kk in range(TOP_K):
        pos = jnp.sum(jnp.where(lane_f == idxs[kk], pos_mat, 0.0), axis=-1, keepdims=True)
        meta = jnp.where(lane == kk, idxs[kk], meta)
        meta = jnp.where(lane == TOP_K + kk, pos, meta)
        meta = jnp.where(lane == 2 * TOP_K + kk, gates[kk], meta)
    meta_ref[...] = meta
    cnt_ref[...] = jnp.broadcast_to(counts, cnt_ref.shape)


def _outproj_router(x2, y_lru, y_ret, w_out_bf16, ng, rw_pad, rb_pad, tri, upper):
    t = x2.shape[0]
    tm = TM_PROJ
    n_tiles = t // tm
    const = lambda i: (0, 0)
    cur = lambda i: (jnp.minimum(i, n_tiles - 1), 0)
    prev = lambda i: (jnp.maximum(i - 1, 0), 0)
    return pl.pallas_call(
        _outproj_router_kernel,
        grid=(n_tiles + 1,),
        in_specs=[
            pl.BlockSpec((tm, D_MODEL), cur),
            pl.BlockSpec((tm, D_LRU), cur),
            pl.BlockSpec((tm, D_RET), cur),
            pl.BlockSpec((D_MODEL, D_MODEL), const),
            pl.BlockSpec((1, D_MODEL), const),
            pl.BlockSpec((D_MODEL, LANES), const),
            pl.BlockSpec((1, LANES), const),
            pl.BlockSpec((tm, tm), const),
            pl.BlockSpec((LANES, LANES), const),
        ],
        out_specs=[
            pl.BlockSpec((tm, D_MODEL), cur),
            pl.BlockSpec((tm, D_MODEL), cur),
            pl.BlockSpec((tm, LANES), prev),
            pl.BlockSpec((None, SUBLANES, LANES), lambda i: (jnp.maximum(i - 1, 0), 0, 0)),
        ],
        out_shape=[
            jax.ShapeDtypeStruct((t, D_MODEL), F32),
            jax.ShapeDtypeStruct((t, D_MODEL), BF16),
            jax.ShapeDtypeStruct((t, LANES), F32),
            jax.ShapeDtypeStruct((n_tiles, SUBLANES, LANES), F32),
        ],
        scratch_shapes=[pltpu.VMEM((2, tm, LANES), F32)],
        compiler_params=_cparams(("arbitrary",)),
        name="outproj_router",
    )(x2, y_lru, y_ret, w_out_bf16, ng, rw_pad, rb_pad, tri, upper)


def _issue_pieces(copy, pieces, tile):
    blo_ref, bgo_ref, nbig_ref, slo_ref, sgo_ref, nsml_ref = pieces

    def start_list(lo_ref, go_ref, base, n, rows):
        groups = lax.shift_right_logical(n, ISSUE_SHIFT)

        def group(p, c):
            for u in range(ISSUE):
                q = base + p * ISSUE + u
                copy(lo_ref[q], go_ref[q], rows).start()
            return c

        lax.fori_loop(0, groups, group, 0)

        def single(p, c):
            copy(lo_ref[base + p], go_ref[base + p], rows).start()
            return c

        lax.fori_loop(lax.shift_left(groups, ISSUE_SHIFT), n, single, 0)

    start_list(blo_ref, bgo_ref, tile * MAX_BIG, nbig_ref[tile], BIG)
    start_list(slo_ref, sgo_ref, tile * MAX_SML, nsml_ref[tile], SEG)


def _wait_rows(copy, nseg):
    nbig = lax.shift_right_logical(nseg, WAIT_SHIFT)

    def big(j, c):
        copy(WAITBIG).wait()
        return c

    lax.fori_loop(0, nbig, big, 0)

    def small(j, c):
        copy(SEG).wait()
        return c

    lax.fori_loop(0, nseg - lax.shift_left(nbig, WAIT_SHIFT), small, 0)


def _sort_kernel(blo_ref, bgo_ref, nbig_ref, slo_ref, sgo_ref, nsml_ref, nch_ref, pend_ref,
                 hf_ref, meta_ref, xs_hbm, sorted_ref, sems, *, min_blocks):
    pieces = (blo_ref, bgo_ref, nbig_ref, slo_ref, sgo_ref, nsml_ref)
    i = pl.program_id(0)
    nt = pl.num_programs(0)
    b = i % 2
    tm = hf_ref.shape[0]
    nb = xs_hbm.shape[0] // BM

    def seg_copy(buf, lo, go, rows):
        return pltpu.make_async_copy(
            sorted_ref.at[buf, pl.ds(pl.multiple_of(lo, SEG), rows)],
            xs_hbm.at[pl.ds(pl.multiple_of(go, SEG), rows)], sems.at[buf])

    def wait_tile(tile, buf):
        _wait_rows(functools.partial(seg_copy, buf, 0, 0), nch_ref[tile])

    @pl.when(i == 0)
    def _():
        sorted_ref[1, 0:BM, :] = jnp.zeros((BM, D_MODEL), F32)

        def zcopy(start):
            return pltpu.make_async_copy(
                sorted_ref.at[1, pl.ds(0, BM)],
                xs_hbm.at[pl.ds(pl.multiple_of(start, BM), BM)], sems.at[1])

        for e in range(N_EXPERTS):
            zcopy(jnp.maximum(pend_ref[e] - BM, 0)).start()
        for e in range(N_EXPERTS):
            zcopy(jnp.maximum(pend_ref[e] - BM, 0)).wait()
        for blk in range(min_blocks, nb):
            @pl.when(blk * BM >= pend_ref[N_EXPERTS - 1])
            def _():
                zcopy(blk * BM).start()
                zcopy(blk * BM).wait()

    @pl.when(i >= 2)
    def _():
        wait_tile(i - 2, b)

    meta_t = meta_ref[...].T
    hfb = hf_ref[...]
    for j in range(RL // PROWS):
        r = (lax.broadcasted_iota(jnp.int32, (PROWS, tm), 0) + j * PROWS).astype(F32)
        perm = jnp.zeros((PROWS, tm), F32)
        for kk in range(TOP_K):
            perm = jnp.where(r == meta_t[TOP_K + kk:TOP_K + kk + 1, :], 1.0, perm)
        sorted_ref[b, j * PROWS:(j + 1) * PROWS, :] = jnp.dot(
            perm.astype(BF16), hfb, preferred_element_type=F32)

    _issue_pieces(functools.partial(seg_copy, b), pieces, i)

    @pl.when(i == nt - 1)
    def _():
        @pl.when(i >= 1)
        def _():
            wait_tile(i - 1, 1 - b)
        wait_tile(i, b)


def _sort_dispatch(pieces, nch, pend, hf, meta, n_pad_rows):
    t = hf.shape[0]
    tm = TM_PROJ
    row = lambda i, *_: (i, 0)
    return pl.pallas_call(
        functools.partial(_sort_kernel, min_blocks=t * TOP_K // BM),
        grid_spec=pltpu.PrefetchScalarGridSpec(
            num_scalar_prefetch=len(pieces) + 2,
            grid=(t // tm,),
            in_specs=[
                pl.BlockSpec((tm, D_MODEL), row),
                pl.BlockSpec((tm, LANES), row),
            ],
            out_specs=pl.BlockSpec(memory_space=pl.ANY),
            scratch_shapes=[pltpu.VMEM((2, RL, D_MODEL), F32), pltpu.SemaphoreType.DMA((2,))],
        ),
        out_shape=jax.ShapeDtypeStruct((n_pad_rows, D_MODEL), F32),
        compiler_params=_cparams(("arbitrary",)),
        name="sort_dispatch",
    )(*pieces, nch, pend, hf, meta)


def _ffn_kernel(be_ref, nv_ref, nxt_ref, rows_ref, x_ref, wg_hbm, bg_ref, wu_hbm, bu_ref, wd_hbm, bd_ref,
                o_ref, wstage_ref, wbf_ref, sems):
    i = pl.program_id(0)
    w_hbm = (wg_hbm, wu_hbm, wd_hbm)

    def fetch(e, m):
        return pltpu.make_async_copy(w_hbm[m].at[e], wstage_ref.at[m], sems.at[m])

    @pl.when(i < nv_ref[0])
    def _():
        @pl.when(i == 0)
        def _():
            for m in range(3):
                fetch(be_ref[0], m).start()

        @pl.when(jnp.logical_or(i == 0, be_ref[i] != be_ref[jnp.maximum(i - 1, 0)]))
        def _():
            for m in range(3):
                fetch(be_ref[i], m).wait()
                wbf_ref[m] = wstage_ref[m].astype(BF16)

            @pl.when(nxt_ref[i] >= 0)
            def _():
                for m in range(3):
                    fetch(nxt_ref[i], m).start()

        for rows in range(FFN_ROWS, BM + 1, FFN_ROWS):
            @pl.when(rows_ref[i] == rows)
            def _(rows=rows):
                x = x_ref[0:rows, :].astype(BF16)
                g = jnp.dot(x, wbf_ref[0], preferred_element_type=F32) + bg_ref[...]
                g = jnp.minimum(g, SWIGLU_LIMIT)
                u = jnp.dot(x, wbf_ref[1], preferred_element_type=F32) + bu_ref[...]
                u = jnp.clip(u, -SWIGLU_LIMIT, SWIGLU_LIMIT)
                act = g * jax.nn.sigmoid(SWIGLU_ALPHA * g) * (u + 1.0)
                o_ref[0:rows, :] = (jnp.dot(act.astype(BF16), wbf_ref[2],
                                            preferred_element_type=F32) + bd_ref[...])
                if rows < BM:
                    o_ref[rows:BM, :] = jnp.zeros((BM - rows, D_MODEL), F32)

    @pl.when(pl.program_id(0) >= nv_ref[0])
    def _():
        o_ref[...] = jnp.zeros_like(o_ref)


def _expert_ffn(block_e, nvalid, next_e, block_rows, xs, wg, bg, wu, bu, wd, bd):
    n_rows = xs.shape[0]
    nb = n_rows // BM

    def blk(i, be, nv, *_):
        return jnp.minimum(i, nv[0] - 1)

    xmap = lambda i, *s: (blk(i, *s), 0)
    bmap = lambda i, be, *s: (be[blk(i, be, *s)], 0, 0)
    hbm = pl.BlockSpec(memory_space=pl.ANY)
    return pl.pallas_call(
        _ffn_kernel,
        grid_spec=pltpu.PrefetchScalarGridSpec(
            num_scalar_prefetch=4,
            grid=(nb,),
            in_specs=[
                pl.BlockSpec((BM, D_MODEL), xmap),
                hbm,
                pl.BlockSpec((None, 1, D_MODEL), bmap),
                hbm,
                pl.BlockSpec((None, 1, D_MODEL), bmap),
                hbm,
                pl.BlockSpec((None, 1, D_MODEL), bmap),
            ],
            out_specs=pl.BlockSpec((BM, D_MODEL), lambda i, *s: (i, 0)),
            scratch_shapes=[pltpu.VMEM((3, D_MODEL, D_MODEL), F32),
                            pltpu.VMEM((3, D_MODEL, D_MODEL), BF16),
                            pltpu.SemaphoreType.DMA((3,))],
        ),
        out_shape=jax.ShapeDtypeStruct((n_rows, D_MODEL), F32),
        compiler_params=_cparams(("arbitrary",)),
        name="expert_ffn",
    )(block_e, nvalid, next_e, block_rows, xs, wg, bg, wu, bu, wd, bd)


def _combine_kernel(blo_ref, bgo_ref, nbig_ref, slo_ref, sgo_ref, nsml_ref, nch_ref,
                    x1_ref, meta_ref, g_ref, yb_hbm, o_ref, ybl_ref, sems):
    pieces = (blo_ref, bgo_ref, nbig_ref, slo_ref, sgo_ref, nsml_ref)
    i = pl.program_id(0)
    nt = pl.num_programs(0)
    b = i % 2
    tm = x1_ref.shape[0]

    def seg_copy(buf, lo, go, rows):
        return pltpu.make_async_copy(
            yb_hbm.at[pl.ds(pl.multiple_of(go, SEG), rows)],
            ybl_ref.at[buf, pl.ds(pl.multiple_of(lo, SEG), rows)], sems.at[buf])

    def issue_tile(tile, buf):
        _issue_pieces(functools.partial(seg_copy, buf), pieces, tile)

    def wait_tile(tile, buf):
        _wait_rows(functools.partial(seg_copy, buf, 0, 0), nch_ref[tile])

    @pl.when(i == 0)
    def _():
        ybl_ref[...] = jnp.zeros_like(ybl_ref)
        issue_tile(0, 0)

    @pl.when(i + 1 < nt)
    def _():
        issue_tile(i + 1, 1 - b)

    wait_tile(i, b)

    meta = meta_ref[...]
    lpos_b = [jnp.broadcast_to(meta[:, TOP_K + kk:TOP_K + kk + 1], (tm, LANES))
              for kk in range(TOP_K)]
    gate_b = [jnp.broadcast_to(meta[:, 2 * TOP_K + kk:2 * TOP_K + kk + 1], (tm, LANES))
              for kk in range(TOP_K)]
    lane_f = lax.broadcasted_iota(jnp.int32, (tm, LANES), 1).astype(F32)
    cols = []
    for j in range(RL // LANES):
        r = lane_f + float(j * LANES)
        gmat = jnp.zeros((tm, LANES), F32)
        for kk in range(TOP_K):
            gmat = jnp.where(r == lpos_b[kk], gate_b[kk], gmat)
        cols.append(gmat.astype(BF16))
    gates = jnp.concatenate(cols, axis=1)
    acc = x1_ref[...] + jnp.dot(gates, ybl_ref[b].astype(BF16), preferred_element_type=F32)
    ms = jnp.mean(acc * acc, axis=-1, keepdims=True)
    o_ref[...] = acc * lax.rsqrt(ms + NORM_EPS) * g_ref[...]


def _combine(pieces, nch, x1, meta, final_g, yb):
    t = x1.shape[0]
    tm = TM_PROJ
    row = lambda i, *_: (i, 0)
    return pl.pallas_call(
        _combine_kernel,
        grid_spec=pltpu.PrefetchScalarGridSpec(
            num_scalar_prefetch=len(pieces) + 1,
            grid=(t // tm,),
            in_specs=[
                pl.BlockSpec((tm, D_MODEL), row),
                pl.BlockSpec((tm, LANES), row),
                pl.BlockSpec((1, D_MODEL), lambda i, *_: (0, 0)),
                pl.BlockSpec(memory_space=pl.ANY),
            ],
            out_specs=pl.BlockSpec((tm, D_MODEL), row),
            scratch_shapes=[pltpu.VMEM((2, RL, D_MODEL), F32), pltpu.SemaphoreType.DMA((2,))],
        ),
        out_shape=jax.ShapeDtypeStruct((t, D_MODEL), F32),
        compiler_params=_cparams(("arbitrary",)),
        name="combine",
    )(*pieces, nch, x1, meta, final_g, yb)


def _expand_pieces(cnt, lo, go, step, cap):
    cum = jnp.cumsum(cnt, axis=1)[:, None, :]
    start = cum - cnt[:, None, :]
    p = jnp.arange(cap, dtype=jnp.int32)[None, :, None]
    mine = jnp.logical_and(p >= start, p < cum)
    off = (p - start) * step
    lo_p = jnp.sum(jnp.where(mine, lo[:, None, :] + off, 0), axis=2)
    go_p = jnp.sum(jnp.where(mine, go[:, None, :] + off, 0), axis=2)
    return (lo_p.astype(jnp.int32).reshape(-1), go_p.astype(jnp.int32).reshape(-1),
            cum[:, 0, -1].astype(jnp.int32))


def _piece_lists(nseg, lo8, goff):
    nbig = nseg >> BIG_SHIFT
    nsml = nseg - (nbig << BIG_SHIFT)
    done = nbig * BIG
    return (_expand_pieces(nbig, lo8, goff, BIG, MAX_BIG)
            + _expand_pieces(nsml, lo8 + done, goff + done, SEG, MAX_SML))


def _block_diag(w):
    h, d, _ = w.shape
    eye = jnp.eye(h, dtype=w.dtype)
    return (eye[:, None, :, None] * w[:, :, None, :]).reshape(h * d, h * d)


def kernel(x, positions, attn_norm_g, w_in, conv_w, conv_b, lru_wa, lru_ba, lru_wx, lru_bx, lru_lambda, lru_norm_g, ret_norm_g, ret_norm_b, w_out, ffn_norm_g, router_w, router_b, moe_w_gate, moe_b_gate, moe_w_up, moe_b_up, moe_w_down, moe_b_down, final_norm_g):
    bsz, seq, d = x.shape
    depth = w_in.shape[0]
    t = bsz * seq
    assert depth == 1 and d == D_MODEL and seq % TC_MIX == 0 and t % TM_PROJ == 0
    nt = t // TM_PROJ
    n_pad = t * TOP_K + nt * N_EXPERTS * SEG + N_EXPERTS * BM
    nb = n_pad // BM

    half = RET_HEAD_DIM // 2
    inv_freq = ROPE_THETA ** (-jnp.arange(half, dtype=F32) / half)
    inv_freq = jnp.concatenate([inv_freq, inv_freq]).reshape(1, RET_HEAD_DIM)
    sign = jnp.concatenate([-jnp.ones((half,), F32), jnp.ones((half,), F32)]).reshape(1, RET_HEAD_DIM)
    pos_col = positions.reshape(t, 1).astype(jnp.int32)
    tri = jnp.tril(jnp.ones((TM_PROJ, TM_PROJ), F32), -1).astype(BF16)
    upper = jnp.triu(jnp.ones((LANES, LANES), F32), 1).astype(BF16)

    x2 = x.reshape(t, d)
    for l in range(depth):
        wg_bd = jnp.concatenate([_block_diag(lru_wa[l]), _block_diag(lru_wx[l])], axis=1).astype(BF16)
        lam = lru_lambda[l].astype(F32)
        sp = (jnp.maximum(-lam, 0.0) + jnp.log1p(jnp.exp(-jnp.abs(lam)))).reshape(1, D_LRU)
        y_lru, y_ret = _mixer(
            x2, pos_col, seq, attn_norm_g[l].reshape(1, d), w_in[l].astype(BF16),
            conv_w[l], conv_b[l].reshape(1, D_LRU), wg_bd,
            lru_ba[l].reshape(1, D_LRU), lru_bx[l].reshape(1, D_LRU), sp,
            lru_norm_g[l].reshape(1, D_LRU), inv_freq, sign,
            ret_norm_g[l].reshape(1, D_RET), ret_norm_b[l].reshape(1, D_RET))

        rw_pad = jnp.zeros((d, LANES), F32).at[:, :N_EXPERTS].set(router_w[l]).astype(BF16)
        rb_pad = jnp.full((1, LANES), -1e30, F32).at[0, :N_EXPERTS].set(router_b[l])
        x1, hf, meta, cnt_tile = _outproj_router(x2, y_lru, y_ret, w_out[l].astype(BF16),
                                            ffn_norm_g[l].reshape(1, d), rw_pad, rb_pad, tri,
                                            upper)

        cnt = cnt_tile[:, 0, :N_EXPERTS].astype(jnp.int32)
        c8 = (cnt + SEG - 1) // SEG * SEG
        lo8 = jnp.cumsum(c8, axis=1) - c8
        padded = (jnp.sum(c8, axis=0) + BM - 1) // BM * BM
        pend = jnp.cumsum(padded).astype(jnp.int32)
        pstart = pend - padded
        goff = (pstart[None, :] + jnp.cumsum(c8, axis=0) - c8).astype(jnp.int32)
        nseg = (c8 // SEG).astype(jnp.int32)
        block_start = jnp.arange(nb, dtype=jnp.int32) * BM
        block_e = jnp.minimum(jnp.sum(block_start[:, None] >= pend[None, :], axis=1),
                              N_EXPERTS - 1).astype(jnp.int32)
        nvalid = (pend[N_EXPERTS - 1:] // BM).astype(jnp.int32)
        pieces = _piece_lists(nseg, lo8.astype(jnp.int32), goff)
        nch = jnp.sum(nseg, axis=1).astype(jnp.int32)

        xs = _sort_dispatch(pieces, nch, pend, hf, meta, n_pad)
        next_blk = pend[block_e] // BM
        next_e = jnp.where(next_blk < nvalid[0], block_e[jnp.minimum(next_blk, nb - 1)],
                           -1).astype(jnp.int32)
        used = (pstart + jnp.sum(c8, axis=0))[block_e] - block_start
        block_rows = ((jnp.clip(used, 1, BM) + FFN_ROWS - 1) // FFN_ROWS * FFN_ROWS).astype(jnp.int32)
        yb = _expert_ffn(block_e, nvalid, next_e, block_rows, xs,
                         moe_w_gate[l], moe_b_gate[l].reshape(N_EXPERTS, 1, d),
                         moe_w_up[l], moe_b_up[l].reshape(N_EXPERTS, 1, d),
                         moe_w_down[l], moe_b_down[l].reshape(N_EXPERTS, 1, d))
        x2 = _combine(pieces, nch, x1, meta, final_norm_g.reshape(1, d), yb)
    return x2.reshape(bsz, seq, d)
```

```python
import functools
import math

import numpy as np
import jax
import jax.numpy as jnp
from jax import lax
from jax.experimental import pallas as pl
from jax.experimental.pallas import tpu as pltpu

F32 = jnp.float32
BF16 = jnp.bfloat16

D_MODEL = 1024
D_LRU = 512
D_RET = 512
CONV_WIDTH = 4
LRU_C = 8.0
RET_HEADS = 4
RET_HEAD_DIM = D_RET // RET_HEADS
ROPE_THETA = 10000.0
D_IN = 2 * D_LRU + 4 * D_RET
N_EXPERTS = 32
TOP_K = 4
SWIGLU_ALPHA = 1.702
SWIGLU_LIMIT = 7.0
NORM_EPS = 1e-5

LANES = 128
SUBLANES = 8
VMEM_LIMIT = 48 * 1024 * 1024

TM_PROJ = 512
TC_MIX = 512
BM = 512
FFN_ROWS = 128
SEG = SUBLANES
RL = TM_PROJ * TOP_K + N_EXPERTS * SEG
PROWS = 768
BIG_SHIFT = 2
BIG = SEG << BIG_SHIFT
ISSUE_SHIFT = 2
ISSUE = 1 << ISSUE_SHIFT
MAX_BIG = RL // BIG
MAX_SML = N_EXPERTS * ((1 << BIG_SHIFT) - 1)
WAIT_SHIFT = 5
WAITBIG = SEG << WAIT_SHIFT

LOG_GAMMA = [math.log1p(-(2.0 ** (-5.0 - h))) for h in range(RET_HEADS)]


def _split_quarter_pi():
    rest = np.float64(np.pi) / 4.0
    parts = []
    for _ in range(3):
        m, e = np.frexp(rest)
        piece = np.ldexp(np.round(m * 1024.0) / 1024.0, e)
        parts.append(float(piece))
        rest = rest - piece
    parts.append(float(np.float32(rest)))
    return parts


_QPI = _split_quarter_pi()
_SIN_C = (-1.9515295891e-4, 8.3321608736e-3, -1.6666654611e-1)
_COS_C = (2.443315711809948e-5, -1.388731625493765e-3, 4.166664568298827e-2)


def _sincos(x):
    ax = jnp.abs(x)
    j = (ax * (4.0 / math.pi)).astype(jnp.int32)
    j = j + jnp.bitwise_and(j, 1)
    y = j.astype(F32)
    r = (((ax - y * _QPI[0]) - y * _QPI[1]) - y * _QPI[2]) - y * _QPI[3]
    z = r * r
    sin_p = r + r * z * ((_SIN_C[0] * z + _SIN_C[1]) * z + _SIN_C[2])
    cos_p = 1.0 - 0.5 * z + z * z * ((_COS_C[0] * z + _COS_C[1]) * z + _COS_C[2])
    quad = jnp.right_shift(j, 1)
    swap = jnp.bitwise_and(quad, 1) == 1
    sin_v = jnp.where(swap, cos_p, sin_p)
    cos_v = jnp.where(swap, sin_p, cos_p)
    sin_neg = jnp.logical_xor(jnp.bitwise_and(quad, 2) == 2, x < 0.0)
    cos_neg = jnp.bitwise_and(quad + 1, 2) == 2
    return jnp.where(sin_neg, -sin_v, sin_v), jnp.where(cos_neg, -cos_v, cos_v)


def _cparams(sem):
    return pltpu.CompilerParams(dimension_semantics=sem, vmem_limit_bytes=VMEM_LIMIT)


def _mixer_kernel(x_ref, pos_ref, ng_ref, win_ref, cw_ref, cb_ref, wg_ref, ba_ref, bx_ref,
                  sp_ref, lng_ref, invf_ref, sgn_ref, gg_ref, gb_ref,
                  ylru_ref, yret_ref,
                  proj_ref, xext_ref, h_ref, state_ref, decay_ref, *, ns):
    i = pl.program_id(0)
    tc = x_ref.shape[0]
    dh = RET_HEAD_DIM

    @pl.when(i == 0)
    def _():
        proj_ref[1] = jnp.zeros((tc, D_IN), BF16)
        ri = lax.broadcasted_iota(jnp.int32, (tc, tc), 0)
        ci = lax.broadcasted_iota(jnp.int32, (tc, tc), 1)
        rel = (ri - ci).astype(F32)
        causal = rel >= 0.0
        relc = jnp.where(causal, rel, 0.0)
        for h in range(RET_HEADS):
            decay_ref[h] = jnp.where(causal, jnp.exp(LOG_GAMMA[h] * relc), 0.0)

    @pl.when(lax.rem(jnp.maximum(i - 1, 0), ns) == 0)
    def _():
        xext_ref[0:SUBLANES, :] = jnp.zeros((SUBLANES, D_LRU), F32)
        h_ref[...] = jnp.zeros_like(h_ref)
        state_ref[...] = jnp.zeros_like(state_ref)

    x = x_ref[...]
    ms = jnp.mean(x * x, axis=-1, keepdims=True)
    hn = x * lax.rsqrt(ms + NORM_EPS) * ng_ref[...]
    proj_ref[i % 2] = jnp.dot(hn.astype(BF16), win_ref[...],
                              preferred_element_type=F32).astype(BF16)

    slot = (i + 1) % 2

    def pcol(lo, width):
        return proj_ref[slot, :, lo:lo + width]

    xext_ref[SUBLANES:SUBLANES + tc, :] = pcol(0, D_LRU).astype(F32)
    xc = cb_ref[...] + cw_ref[CONV_WIDTH - 1:CONV_WIDTH, :] * xext_ref[SUBLANES:SUBLANES + tc, :]
    for j in range(CONV_WIDTH - 1):
        off = SUBLANES - (CONV_WIDTH - 1) + j
        xc = xc + cw_ref[j:j + 1, :] * xext_ref[off:off + tc, :]
    xext_ref[0:SUBLANES, :] = xext_ref[tc:tc + SUBLANES, :]

    gates = jnp.dot(xc.astype(BF16), wg_ref[...], preferred_element_type=F32)
    r = jax.nn.sigmoid(gates[:, :D_LRU] + ba_ref[...])
    ig = jax.nn.sigmoid(gates[:, D_LRU:] + bx_ref[...])
    a = jnp.exp((-LRU_C) * r * sp_ref[...])
    z = 1.0 - a * a
    b = jnp.where(z > 0.0, z * lax.rsqrt(z), 0.0) * (ig * xc)

    ng = tc // SUBLANES
    a = a.reshape(ng, SUBLANES, D_LRU)
    b = b.reshape(ng, SUBLANES, D_LRU)
    in_group = lax.broadcasted_iota(jnp.int32, (1, SUBLANES, 1), 1)
    d = 1
    while d < SUBLANES:
        a_s = pltpu.roll(a, d, 1)
        b_s = pltpu.roll(b, d, 1)
        m = in_group >= d
        b = jnp.where(m, a * b_s + b, b)
        a = jnp.where(m, a * a_s, a)
        d *= 2
    h_prev = h_ref[0:1, :]
    groups = []
    for g in range(ng):
        hg = b[g] + a[g] * h_prev
        groups.append(hg)
        h_prev = hg[SUBLANES - 1:SUBLANES, :]
    hseq = jnp.concatenate(groups, axis=0)
    h_ref[0:1, :] = h_prev

    y = hseq * jax.nn.gelu(pcol(D_LRU, D_LRU).astype(F32))
    ms = jnp.mean(y * y, axis=-1, keepdims=True)
    ylru_ref[...] = (y * lax.rsqrt(ms + NORM_EPS) * lng_ref[...]).astype(ylru_ref.dtype)

    hr = tc // 2
    pos = pos_ref[...].astype(F32)
    first = lax.broadcasted_iota(jnp.int32, (hr, dh), 1) < dh // 2
    ang = jnp.where(first, pos[0:hr, :], pos[hr:tc, :]) * invf_ref[...]
    s2, c2 = _sincos(ang)
    c2r = pltpu.roll(c2, dh // 2, 1)
    s2r = pltpu.roll(s2, dh // 2, 1)
    cos = jnp.concatenate([jnp.where(first, c2, c2r), jnp.where(first, c2r, c2)], axis=0)
    sin = jnp.concatenate([jnp.where(first, s2, s2r), jnp.where(first, s2r, s2)], axis=0)
    sin_signed = sin * sgn_ref[...]
    rowf = lax.broadcasted_iota(jnp.int32, (tc, 1), 0).astype(F32)
    scale = dh ** -0.5
    q0, k0, v0, g0 = 2 * D_LRU, 2 * D_LRU + D_RET, 2 * D_LRU + 2 * D_RET, 2 * D_LRU + 3 * D_RET

    for h in range(RET_HEADS):
        sl = slice(h * dh, (h + 1) * dh)
        q = pcol(q0 + h * dh, dh).astype(F32)
        k = pcol(k0 + h * dh, dh).astype(F32)
        v = pcol(v0 + h * dh, dh)
        qr = q * cos + pltpu.roll(q, dh // 2, 1) * sin_signed
        kr = (k * cos + pltpu.roll(k, dh // 2, 1) * sin_signed) * scale
        qb = qr.astype(BF16)
        kb = kr.astype(BF16)
        lg = LOG_GAMMA[h]
        scores = lax.dot_general(qb, kb, (((1,), (1,)), ((), ())),
                                 preferred_element_type=F32) * decay_ref[h]
        intra = jnp.dot(scores.astype(BF16), v, preferred_element_type=F32)
        st = state_ref[h]
        xi = jnp.exp(lg * (rowf + 1.0))
        cross = jnp.dot(qb, st.astype(BF16), preferred_element_type=F32) * xi
        o = intra + cross
        zeta = jnp.exp(lg * (float(tc) - 1.0 - rowf))
        kz = (kr * zeta).astype(BF16)
        kv = lax.dot_general(kz, v, (((0,), (0,)), ((), ())), preferred_element_type=F32)
        state_ref[h] = math.exp(lg * tc) * st + kv

        mu = jnp.mean(o, axis=-1, keepdims=True)
        oc = o - mu
        var = jnp.mean(oc * oc, axis=-1, keepdims=True)
        on = oc * lax.rsqrt(var + NORM_EPS) * gg_ref[:, sl] + gb_ref[:, sl]
        gate = pcol(g0 + h * dh, dh).astype(F32)
        yret_ref[:, sl] = (gate * jax.nn.sigmoid(gate) * on).astype(yret_ref.dtype)


def _mixer(x2, pos_col, seq, norm_g, w_in_bf16, conv_w, conv_b, wg_bd, ba, bx, sp, lru_norm_g,
           inv_freq, sign, gn_g, gn_b):
    t = x2.shape[0]
    tc = TC_MIX
    n_tiles = t // tc
    const = lambda i: (0, 0)
    cur = lambda i: (jnp.minimum(i, n_tiles - 1), 0)
    prev = lambda i: (jnp.maximum(i - 1, 0), 0)
    vec = lambda n: pl.BlockSpec((1, n), const)
    return pl.pallas_call(
        functools.partial(_mixer_kernel, ns=seq // tc),
        grid=(n_tiles + 1,),
        in_specs=[
            pl.BlockSpec((tc, D_MODEL), cur),
            pl.BlockSpec((tc, 1), prev),
            vec(D_MODEL),
            pl.BlockSpec((D_MODEL, D_IN), const),
            pl.BlockSpec((CONV_WIDTH, D_LRU), const),
            vec(D_LRU),
            pl.BlockSpec((D_LRU, 2 * D_LRU), const),
            vec(D_LRU), vec(D_LRU), vec(D_LRU), vec(D_LRU),
            vec(RET_HEAD_DIM), vec(RET_HEAD_DIM),
            vec(D_RET), vec(D_RET),
        ],
        out_specs=[pl.BlockSpec((tc, D_LRU), prev), pl.BlockSpec((tc, D_RET), prev)],
        out_shape=[jax.ShapeDtypeStruct((t, D_LRU), BF16), jax.ShapeDtypeStruct((t, D_RET), BF16)],
        scratch_shapes=[
            pltpu.VMEM((2, tc, D_IN), BF16),
            pltpu.VMEM((tc + SUBLANES, D_LRU), F32),
            pltpu.VMEM((SUBLANES, D_LRU), F32),
            pltpu.VMEM((RET_HEADS, RET_HEAD_DIM, RET_HEAD_DIM), F32),
            pltpu.VMEM((RET_HEADS, tc, tc), F32),
        ],
        compiler_params=_cparams(("arbitrary",)),
        name="mixer",
    )(x2, pos_col, norm_g, w_in_bf16, conv_w, conv_b, wg_bd, ba, bx, sp, lru_norm_g,
      inv_freq, sign, gn_g, gn_b)


def _outproj_router_kernel(x_ref, yl_ref, yr_ref, wo_ref, ng_ref, rw_ref, rb_ref, tri_ref,
                           upper_ref, x1_ref, hf_ref, meta_ref, cnt_ref, logit_ref):
    i = pl.program_id(0)
    tm = x_ref.shape[0]

    @pl.when(i == 0)
    def _():
        logit_ref[1] = jnp.zeros((tm, LANES), F32)

    logits = logit_ref[(i + 1) % 2]

    y = jnp.dot(yl_ref[...], wo_ref[0:D_LRU, :], preferred_element_type=F32)
    y = y + jnp.dot(yr_ref[...], wo_ref[D_LRU:, :], preferred_element_type=F32)
    x1 = x_ref[...] + y
    x1_ref[...] = x1
    ms = jnp.mean(x1 * x1, axis=-1, keepdims=True)
    hf = x1 * lax.rsqrt(ms + NORM_EPS) * ng_ref[...]
    hf_ref[...] = hf.astype(hf_ref.dtype)

    logit_ref[i % 2] = (jnp.dot(hf.astype(BF16), rw_ref[...], preferred_element_type=F32)
                        + rb_ref[...])

    lane = lax.broadcasted_iota(jnp.int32, (tm, LANES), 1)
    lane_f = lane.astype(F32)
    work = logits
    vals, idxs = [], []
    onehot = jnp.zeros((tm, LANES), F32)
    for _ in range(TOP_K):
        m = jnp.max(work, axis=-1, keepdims=True)
        idx = jnp.min(jnp.where(work == m, lane_f, float(LANES)), axis=-1, keepdims=True)
        sel = lane_f == idx
        work = jnp.where(sel, -jnp.inf, work)
        onehot = jnp.where(sel, 1.0, onehot)
        vals.append(m)
        idxs.append(idx)
    exps = [jnp.exp(v - vals[0]) for v in vals]
    denom = exps[0] + exps[1] + exps[2] + exps[3]
    gates = [e / denom for e in exps]

    counts = jnp.sum(onehot, axis=0, keepdims=True)
    seg_rows = jnp.floor((counts + (SEG - 1.0)) * (1.0 / SEG)) * SEG
    seg_start = jnp.dot(jnp.broadcast_to(seg_rows, (SUBLANES, LANES)).astype(BF16), upper_ref[...],
                        preferred_element_type=F32)[0:1, :]
    pos_mat = jnp.dot(tri_ref[...], onehot.astype(BF16), preferred_element_type=F32) + seg_start
    meta = jnp.zeros((tm, LANES), F32)
    for kk in range(TOP_K):
        pos = jnp.sum(jnp.where(lane_f == idxs[kk], pos_mat, 0.0), axis=-1, keepdims=True)
        meta = jnp.where(lane == kk, idxs[kk], meta)
        meta = jnp.where(lane == TOP_K + kk, pos, meta)
        meta = jnp.where(lane == 2 * TOP_K + kk, gates[kk], meta)
    meta_ref[...] = meta
    cnt_ref[...] = jnp.broadcast_to(counts, cnt_ref.shape)


def _outproj_router(x2, y_lru, y_ret, w_out_bf16, ng, rw_pad, rb_pad, tri, upper):
    t = x2.shape[0]
    tm = TM_PROJ
    n_tiles = t // tm
    const = lambda i: (0, 0)
    cur = lambda i: (jnp.minimum(i, n_tiles - 1), 0)
    prev = lambda i: (jnp.maximum(i - 1, 0), 0)
    return pl.pallas_call(
        _outproj_router_kernel,
        grid=(n_tiles + 1,),
        in_specs=[
            pl.BlockSpec((tm, D_MODEL), cur),
            pl.BlockSpec((tm, D_LRU), cur),
            pl.BlockSpec((tm, D_RET), cur),
            pl.BlockSpec((D_MODEL, D_MODEL), const),
            pl.BlockSpec((1, D_MODEL), const),
            pl.BlockSpec((D_MODEL, LANES), const),
            pl.BlockSpec((1, LANES), const),
            pl.BlockSpec((tm, tm), const),
            pl.BlockSpec((LANES, LANES), const),
        ],
        out_specs=[
            pl.BlockSpec((tm, D_MODEL), cur),
            pl.BlockSpec((tm, D_MODEL), cur),
            pl.BlockSpec((tm, LANES), prev),
            pl.BlockSpec((None, SUBLANES, LANES), lambda i: (jnp.maximum(i - 1, 0), 0, 0)),
        ],
        out_shape=[
            jax.ShapeDtypeStruct((t, D_MODEL), F32),
            jax.ShapeDtypeStruct((t, D_MODEL), BF16),
            jax.ShapeDtypeStruct((t, LANES), F32),
            jax.ShapeDtypeStruct((n_tiles, SUBLANES, LANES), F32),
        ],
        scratch_shapes=[pltpu.VMEM((2, tm, LANES), F32)],
        compiler_params=_cparams(("arbitrary",)),
        name="outproj_router",
    )(x2, y_lru, y_ret, w_out_bf16, ng, rw_pad, rb_pad, tri, upper)


def _issue_pieces(copy, pieces, tile):
    blo_ref, bgo_ref, nbig_ref, slo_ref, sgo_ref, nsml_ref = pieces

    def start_list(lo_ref, go_ref, base, n, rows):
        groups = lax.shift_right_logical(n, ISSUE_SHIFT)

        def group(p, c):
            for u in range(ISSUE):
                q = base + p * ISSUE + u
                copy(lo_ref[q], go_ref[q], rows).start()
            return c

        lax.fori_loop(0, groups, group, 0)

        def single(p, c):
            copy(lo_ref[base + p], go_ref[base + p], rows).start()
            return c

        lax.fori_loop(lax.shift_left(groups, ISSUE_SHIFT), n, single, 0)

    start_list(blo_ref, bgo_ref, tile * MAX_BIG, nbig_ref[tile], BIG)
    start_list(slo_ref, sgo_ref, tile * MAX_SML, nsml_ref[tile], SEG)


def _wait_rows(copy, nseg):
    nbig = lax.shift_right_logical(nseg, WAIT_SHIFT)

    def big(j, c):
        copy(WAITBIG).wait()
        return c

    lax.fori_loop(0, nbig, big, 0)

    def small(j, c):
        copy(SEG).wait()
        return c

    lax.fori_loop(0, nseg - lax.shift_left(nbig, WAIT_SHIFT), small, 0)


def _sort_kernel(blo_ref, bgo_ref, nbig_ref, slo_ref, sgo_ref, nsml_ref, nch_ref, pend_ref,
                 hf_ref, meta_ref, xs_hbm, sorted_ref, sems, *, min_blocks):
    pieces = (blo_ref, bgo_ref, nbig_ref, slo_ref, sgo_ref, nsml_ref)
    i = pl.program_id(0)
    nt = pl.num_programs(0)
    b = i % 2
    tm = hf_ref.shape[0]
    nb = xs_hbm.shape[0] // BM

    def seg_copy(buf, lo, go, rows):
        return pltpu.make_async_copy(
            sorted_ref.at[buf, pl.ds(pl.multiple_of(lo, SEG), rows)],
            xs_hbm.at[pl.ds(pl.multiple_of(go, SEG), rows)], sems.at[buf])

    def wait_tile(tile, buf):
        _wait_rows(functools.partial(seg_copy, buf, 0, 0), nch_ref[tile])

    @pl.when(i == 0)
    def _():
        sorted_ref[1, 0:BM, :] = jnp.zeros((BM, D_MODEL), F32)

        def zcopy(start):
            return pltpu.make_async_copy(
                sorted_ref.at[1, pl.ds(0, BM)],
                xs_hbm.at[pl.ds(pl.multiple_of(start, BM), BM)], sems.at[1])

        for e in range(N_EXPERTS):
            zcopy(jnp.maximum(pend_ref[e] - BM, 0)).start()
        for e in range(N_EXPERTS):
            zcopy(jnp.maximum(pend_ref[e] - BM, 0)).wait()
        for blk in range(min_blocks, nb):
            @pl.when(blk * BM >= pend_ref[N_EXPERTS - 1])
            def _():
                zcopy(blk * BM).start()
                zcopy(blk * BM).wait()

    @pl.when(i >= 2)
    def _():
        wait_tile(i - 2, b)

    meta_t = meta_ref[...].T
    hfb = hf_ref[...]
    for j in range(RL // PROWS):
        r = (lax.broadcasted_iota(jnp.int32, (PROWS, tm), 0) + j * PROWS).astype(F32)
        perm = jnp.zeros((PROWS, tm), F32)
        for kk in range(TOP_K):
            perm = jnp.where(r == meta_t[TOP_K + kk:TOP_K + kk + 1, :], 1.0, perm)
        sorted_ref[b, j * PROWS:(j + 1) * PROWS, :] = jnp.dot(
            perm.astype(BF16), hfb, preferred_element_type=F32)

    _issue_pieces(functools.partial(seg_copy, b), pieces, i)

    @pl.when(i == nt - 1)
    def _():
        @pl.when(i >= 1)
        def _():
            wait_tile(i - 1, 1 - b)
        wait_tile(i, b)


def _sort_dispatch(pieces, nch, pend, hf, meta, n_pad_rows):
    t = hf.shape[0]
    tm = TM_PROJ
    row = lambda i, *_: (i, 0)
    return pl.pallas_call(
        functools.partial(_sort_kernel, min_blocks=t * TOP_K // BM),
        grid_spec=pltpu.PrefetchScalarGridSpec(
            num_scalar_prefetch=len(pieces) + 2,
            grid=(t // tm,),
            in_specs=[
                pl.BlockSpec((tm, D_MODEL), row),
                pl.BlockSpec((tm, LANES), row),
            ],
            out_specs=pl.BlockSpec(memory_space=pl.ANY),
            scratch_shapes=[pltpu.VMEM((2, RL, D_MODEL), F32), pltpu.SemaphoreType.DMA((2,))],
        ),
        out_shape=jax.ShapeDtypeStruct((n_pad_rows, D_MODEL), F32),
        compiler_params=_cparams(("arbitrary",)),
        name="sort_dispatch",
    )(*pieces, nch, pend, hf, meta)


def _ffn_kernel(be_ref, nv_ref, nxt_ref, rows_ref, x_ref, wg_hbm, bg_ref, wu_hbm, bu_ref, wd_hbm, bd_ref,
                o_ref, wstage_ref, wbf_ref, sems):
    i = pl.program_id(0)
    w_hbm = (wg_hbm, wu_hbm, wd_hbm)

    def fetch(e, m):
        return pltpu.make_async_copy(w_hbm[m].at[e], wstage_ref.at[m], sems.at[m])

    @pl.when(i < nv_ref[0])
    def _():
        @pl.when(i == 0)
        def _():
            for m in range(3):
                fetch(be_ref[0], m).start()

        @pl.when(jnp.logical_or(i == 0, be_ref[i] != be_ref[jnp.maximum(i - 1, 0)]))
        def _():
            for m in range(3):
                fetch(be_ref[i], m).wait()
                wbf_ref[m] = wstage_ref[m].astype(BF16)

            @pl.when(nxt_ref[i] >= 0)
            def _():
                for m in range(3):
                    fetch(nxt_ref[i], m).start()

        for rows in range(FFN_ROWS, BM + 1, FFN_ROWS):
            @pl.when(rows_ref[i] == rows)
            def _(rows=rows):
                x = x_ref[0:rows, :].astype(BF16)
                g = jnp.dot(x, wbf_ref[0], preferred_element_type=F32) + bg_ref[...]
                g = jnp.minimum(g, SWIGLU_LIMIT)
                u = jnp.dot(x, wbf_ref[1], preferred_element_type=F32) + bu_ref[...]
                u = jnp.clip(u, -SWIGLU_LIMIT, SWIGLU_LIMIT)
                act = g * jax.nn.sigmoid(SWIGLU_ALPHA * g) * (u + 1.0)
                o_ref[0:rows, :] = (jnp.dot(act.astype(BF16), wbf_ref[2],
                                            preferred_element_type=F32) + bd_ref[...])
                if rows < BM:
                    o_ref[rows:BM, :] = jnp.zeros((BM - rows, D_MODEL), F32)

    @pl.when(pl.program_id(0) >= nv_ref[0])
    def _():
        o_ref[...] = jnp.zeros_like(o_ref)


def _expert_ffn(block_e, nvalid, next_e, block_rows, xs, wg, bg, wu, bu, wd, bd):
    n_rows = xs.shape[0]
    nb = n_rows // BM

    def blk(i, be, nv, *_):
        return jnp.minimum(i, nv[0] - 1)

    xmap = lambda i, *s: (blk(i, *s), 0)
    bmap = lambda i, be, *s: (be[blk(i, be, *s)], 0, 0)
    hbm = pl.BlockSpec(memory_space=pl.ANY)
    return pl.pallas_call(
        _ffn_kernel,
        grid_spec=pltpu.PrefetchScalarGridSpec(
            num_scalar_prefetch=4,
            grid=(nb,),
            in_specs=[
                pl.BlockSpec((BM, D_MODEL), xmap),
                hbm,
                pl.BlockSpec((None, 1, D_MODEL), bmap),
                hbm,
                pl.BlockSpec((None, 1, D_MODEL), bmap),
                hbm,
                pl.BlockSpec((None, 1, D_MODEL), bmap),
            ],
            out_specs=pl.BlockSpec((BM, D_MODEL), lambda i, *s: (i, 0)),
            scratch_shapes=[pltpu.VMEM((3, D_MODEL, D_MODEL), F32),
                            pltpu.VMEM((3, D_MODEL, D_MODEL), BF16),
                            pltpu.SemaphoreType.DMA((3,))],
        ),
        out_shape=jax.ShapeDtypeStruct((n_rows, D_MODEL), F32),
        compiler_params=_cparams(("arbitrary",)),
        name="expert_ffn",
    )(block_e, nvalid, next_e, block_rows, xs, wg, bg, wu, bu, wd, bd)


def _combine_kernel(blo_ref, bgo_ref, nbig_ref, slo_ref, sgo_ref, nsml_ref, nch_ref,
                    x1_ref, meta_ref, g_ref, yb_hbm, o_ref, ybl_ref, sems):
    pieces = (blo_ref, bgo_ref, nbig_ref, slo_ref, sgo_ref, nsml_ref)
    i = pl.program_id(0)
    nt = pl.num_programs(0)
    b = i % 2
    tm = x1_ref.shape[0]

    def seg_copy(buf, lo, go, rows):
        return pltpu.make_async_copy(
            yb_hbm.at[pl.ds(pl.multiple_of(go, SEG), rows)],
            ybl_ref.at[buf, pl.ds(pl.multiple_of(lo, SEG), rows)], sems.at[buf])

    def issue_tile(tile, buf):
        _issue_pieces(functools.partial(seg_copy, buf), pieces, tile)

    def wait_tile(tile, buf):
        _wait_rows(functools.partial(seg_copy, buf, 0, 0), nch_ref[tile])

    @pl.when(i == 0)
    def _():
        ybl_ref[...] = jnp.zeros_like(ybl_ref)
        issue_tile(0, 0)

    @pl.when(i + 1 < nt)
    def _():
        issue_tile(i + 1, 1 - b)

    wait_tile(i, b)

    meta = meta_ref[...]
    lpos_b = [jnp.broadcast_to(meta[:, TOP_K + kk:TOP_K + kk + 1], (tm, LANES))
              for kk in range(TOP_K)]
    gate_b = [jnp.broadcast_to(meta[:, 2 * TOP_K + kk:2 * TOP_K + kk + 1], (tm, LANES))
              for kk in range(TOP_K)]
    lane_f = lax.broadcasted_iota(jnp.int32, (tm, LANES), 1).astype(F32)
    cols = []
    for j in range(RL // LANES):
        r = lane_f + float(j * LANES)
        gmat = jnp.zeros((tm, LANES), F32)
        for kk in range(TOP_K):
            gmat = jnp.where(r == lpos_b[kk], gate_b[kk], gmat)
        cols.append(gmat.astype(BF16))
    gates = jnp.concatenate(cols, axis=1)
    acc = x1_ref[...] + jnp.dot(gates, ybl_ref[b].astype(BF16), preferred_element_type=F32)
    ms = jnp.mean(acc * acc, axis=-1, keepdims=True)
    o_ref[...] = acc * lax.rsqrt(ms + NORM_EPS) * g_ref[...]


def _combine(pieces, nch, x1, meta, final_g, yb):
    t = x1.shape[0]
    tm = TM_PROJ
    row = lambda i, *_: (i, 0)
    return pl.pallas_call(
        _combine_kernel,
        grid_spec=pltpu.PrefetchScalarGridSpec(
            num_scalar_prefetch=len(pieces) + 1,
            grid=(t // tm,),
            in_specs=[
                pl.BlockSpec((tm, D_MODEL), row),
                pl.BlockSpec((tm, LANES), row),
                pl.BlockSpec((1, D_MODEL), lambda i, *_: (0, 0)),
                pl.BlockSpec(memory_space=pl.ANY),
            ],
            out_specs=pl.BlockSpec((tm, D_MODEL), row),
            scratch_shapes=[pltpu.VMEM((2, RL, D_MODEL), F32), pltpu.SemaphoreType.DMA((2,))],
        ),
        out_shape=jax.ShapeDtypeStruct((t, D_MODEL), F32),
        compiler_params=_cparams(("arbitrary",)),
        name="combine",
    )(*pieces, nch, x1, meta, final_g, yb)


def _expand_pieces(cnt, lo, go, step, cap):
    cum = jnp.cumsum(cnt, axis=1)[:, None, :]
    start = cum - cnt[:, None, :]
    p = jnp.arange(cap, dtype=jnp.int32)[None, :, None]
    mine = jnp.logical_and(p >= start, p < cum)
    off = (p - start) * step
    lo_p = jnp.sum(jnp.where(mine, lo[:, None, :] + off, 0), axis=2)
    go_p = jnp.sum(jnp.where(mine, go[:, None, :] + off, 0), axis=2)
    return (lo_p.astype(jnp.int32).reshape(-1), go_p.astype(jnp.int32).reshape(-1),
            cum[:, 0, -1].astype(jnp.int32))


def _piece_lists(nseg, lo8, goff):
    nbig = nseg >> BIG_SHIFT
    nsml = nseg - (nbig << BIG_SHIFT)
    done = nbig * BIG
    return (_expand_pieces(nbig, lo8, goff, BIG, MAX_BIG)
            + _expand_pieces(nsml, lo8 + done, goff + done, SEG, MAX_SML))


def _block_diag(w):
    h, d, _ = w.shape
    eye = jnp.eye(h, dtype=w.dtype)
    return (eye[:, None, :, None] * w[:, :, None, :]).reshape(h * d, h * d)


def kernel(x, positions, attn_norm_g, w_in, conv_w, conv_b, lru_wa, lru_ba, lru_wx, lru_bx, lru_lambda, lru_norm_g, ret_norm_g, ret_norm_b, w_out, ffn_norm_g, router_w, router_b, moe_w_gate, moe_b_gate, moe_w_up, moe_b_up, moe_w_down, moe_b_down, final_norm_g):
    bsz, seq, d = x.shape
    depth = w_in.shape[0]
    t = bsz * seq
    assert depth == 1 and d == D_MODEL and seq % TC_MIX == 0 and t % TM_PROJ == 0
    nt = t // TM_PROJ
    n_pad = t * TOP_K + nt * N_EXPERTS * SEG + N_EXPERTS * BM
    nb = n_pad // BM

    half = RET_HEAD_DIM // 2
    inv_freq = ROPE_THETA ** (-jnp.arange(half, dtype=F32) / half)
    inv_freq = jnp.concatenate([inv_freq, inv_freq]).reshape(1, RET_HEAD_DIM)
    sign = jnp.concatenate([-jnp.ones((half,), F32), jnp.ones((half,), F32)]).reshape(1, RET_HEAD_DIM)
    pos_col = positions.reshape(t, 1).astype(jnp.int32)
    tri = jnp.tril(jnp.ones((TM_PROJ, TM_PROJ), F32), -1).astype(BF16)
    upper = jnp.triu(jnp.ones((LANES, LANES), F32), 1).astype(BF16)

    x2 = x.reshape(t, d)
    for l in range(depth):
        wg_bd = jnp.concatenate([_block_diag(lru_wa[l]), _block_diag(lru_wx[l])], axis=1).astype(BF16)
        lam = lru_lambda[l].astype(F32)
        sp = (jnp.maximum(-lam, 0.0) + jnp.log1p(jnp.exp(-jnp.abs(lam)))).reshape(1, D_LRU)
        y_lru, y_ret = _mixer(
            x2, pos_col, seq, attn_norm_g[l].reshape(1, d), w_in[l].astype(BF16),
            conv_w[l], conv_b[l].reshape(1, D_LRU), wg_bd,
            lru_ba[l].reshape(1, D_LRU), lru_bx[l].reshape(1, D_LRU), sp,
            lru_norm_g[l].reshape(1, D_LRU), inv_freq, sign,
            ret_norm_g[l].reshape(1, D_RET), ret_norm_b[l].reshape(1, D_RET))

        rw_pad = jnp.zeros((d, LANES), F32).at[:, :N_EXPERTS].set(router_w[l]).astype(BF16)
        rb_pad = jnp.full((1, LANES), -1e30, F32).at[0, :N_EXPERTS].set(router_b[l])
        x1, hf, meta, cnt_tile = _outproj_router(x2, y_lru, y_ret, w_out[l].astype(BF16),
                                            ffn_norm_g[l].reshape(1, d), rw_pad, rb_pad, tri,
                                            upper)

        cnt = cnt_tile[:, 0, :N_EXPERTS].astype(jnp.int32)
        c8 = (cnt + SEG - 1) // SEG * SEG
        lo8 = jnp.cumsum(c8, axis=1) - c8
        padded = (jnp.sum(c8, axis=0) + BM - 1) // BM * BM
        pend = jnp.cumsum(padded).astype(jnp.int32)
        pstart = pend - padded
        goff = (pstart[None, :] + jnp.cumsum(c8, axis=0) - c8).astype(jnp.int32)
        nseg = (c8 // SEG).astype(jnp.int32)
        block_start = jnp.arange(nb, dtype=jnp.int32) * BM
        block_e = jnp.minimum(jnp.sum(block_start[:, None] >= pend[None, :], axis=1),
                              N_EXPERTS - 1).astype(jnp.int32)
        nvalid = (pend[N_EXPERTS - 1:] // BM).astype(jnp.int32)
        pieces = _piece_lists(nseg, lo8.astype(jnp.int32), goff)
        nch = jnp.sum(nseg, axis=1).astype(jnp.int32)

        xs = _sort_dispatch(pieces, nch, pend, hf, meta, n_pad)
        mine = block_e[:, None] == jnp.arange(N_EXPERTS, dtype=jnp.int32)[None, :]
        next_start = jnp.sum(jnp.where(mine, pend[None, :], 0), axis=1)
        next_e = jnp.minimum(jnp.sum(next_start[:, None] >= pend[None, :], axis=1), N_EXPERTS - 1)
        next_e = jnp.where(next_start < nvalid[0] * BM, next_e, -1).astype(jnp.int32)
        row_end = pstart + jnp.sum(c8, axis=0)
        used = jnp.sum(jnp.where(mine, row_end[None, :], 0), axis=1) - block_start
        block_rows = ((jnp.clip(used, 1, BM) + FFN_ROWS - 1) // FFN_ROWS * FFN_ROWS).astype(jnp.int32)
        yb = _expert_ffn(block_e, nvalid, next_e, block_rows, xs,
                         moe_w_gate[l], moe_b_gate[l].reshape(N_EXPERTS, 1, d),
                         moe_w_up[l], moe_b_up[l].reshape(N_EXPERTS, 1, d),
                         moe_w_down[l], moe_b_down[l].reshape(N_EXPERTS, 1, d))
        x2 = _combine(pieces, nch, x1, meta, final_norm_g.reshape(1, d), yb)
    return x2.reshape(bsz, seq, d)
```

```python
import functools
import math

import numpy as np
import jax
import jax.numpy as jnp
from jax import lax
from jax.experimental import pallas as pl
from jax.experimental.pallas import tpu as pltpu

F32 = jnp.float32
BF16 = jnp.bfloat16

D_MODEL = 1024
D_LRU = 512
D_RET = 512
CONV_WIDTH = 4
LRU_C = 8.0
RET_HEADS = 4
RET_HEAD_DIM = D_RET // RET_HEADS
ROPE_THETA = 10000.0
D_IN = 2 * D_LRU + 4 * D_RET
N_EXPERTS = 32
TOP_K = 4
SWIGLU_ALPHA = 1.702
SWIGLU_LIMIT = 7.0
NORM_EPS = 1e-5

LANES = 128
SUBLANES = 8
VMEM_LIMIT = 48 * 1024 * 1024

TM_PROJ = 512
TC_MIX = 512
BM = 1024
FFN_ROWS = 128
SEG = SUBLANES
RL = TM_PROJ * TOP_K + N_EXPERTS * SEG
PROWS = 768
BIG_SHIFT = 2
BIG = SEG << BIG_SHIFT
ISSUE_SHIFT = 2
ISSUE = 1 << ISSUE_SHIFT
MAX_BIG = RL // BIG
MAX_SML = N_EXPERTS * ((1 << BIG_SHIFT) - 1)
WAIT_SHIFT = 5
WAITBIG = SEG << WAIT_SHIFT

LOG_GAMMA = [math.log1p(-(2.0 ** (-5.0 - h))) for h in range(RET_HEADS)]


def _split_quarter_pi():
    rest = np.float64(np.pi) / 4.0
    parts = []
    for _ in range(3):
        m, e = np.frexp(rest)
        piece = np.ldexp(np.round(m * 1024.0) / 1024.0, e)
        parts.append(float(piece))
        rest = rest - piece
    parts.append(float(np.float32(rest)))
    return parts


_QPI = _split_quarter_pi()
_SIN_C = (-1.9515295891e-4, 8.3321608736e-3, -1.6666654611e-1)
_COS_C = (2.443315711809948e-5, -1.388731625493765e-3, 4.166664568298827e-2)


def _sincos(x):
    ax = jnp.abs(x)
    j = (ax * (4.0 / math.pi)).astype(jnp.int32)
    j = j + jnp.bitwise_and(j, 1)
    y = j.astype(F32)
    r = (((ax - y * _QPI[0]) - y * _QPI[1]) - y * _QPI[2]) - y * _QPI[3]
    z = r * r
    sin_p = r + r * z * ((_SIN_C[0] * z + _SIN_C[1]) * z + _SIN_C[2])
    cos_p = 1.0 - 0.5 * z + z * z * ((_COS_C[0] * z + _COS_C[1]) * z + _COS_C[2])
    quad = jnp.right_shift(j, 1)
    swap = jnp.bitwise_and(quad, 1) == 1
    sin_v = jnp.where(swap, cos_p, sin_p)
    cos_v = jnp.where(swap, sin_p, cos_p)
    sin_neg = jnp.logical_xor(jnp.bitwise_and(quad, 2) == 2, x < 0.0)
    cos_neg = jnp.bitwise_and(quad + 1, 2) == 2
    return jnp.where(sin_neg, -sin_v, sin_v), jnp.where(cos_neg, -cos_v, cos_v)


def _cparams(sem):
    return pltpu.CompilerParams(dimension_semantics=sem, vmem_limit_bytes=VMEM_LIMIT)


def _mixer_kernel(x_ref, pos_ref, ng_ref, win_ref, cw_ref, cb_ref, wg_ref, ba_ref, bx_ref,
                  sp_ref, lng_ref, invf_ref, sgn_ref, gg_ref, gb_ref,
                  ylru_ref, yret_ref,
                  proj_ref, xext_ref, h_ref, state_ref, decay_ref, *, ns):
    i = pl.program_id(0)
    tc = x_ref.shape[0]
    dh = RET_HEAD_DIM

    @pl.when(i == 0)
    def _():
        proj_ref[1] = jnp.zeros((tc, D_IN), BF16)
        ri = lax.broadcasted_iota(jnp.int32, (tc, tc), 0)
        ci = lax.broadcasted_iota(jnp.int32, (tc, tc), 1)
        rel = (ri - ci).astype(F32)
        causal = rel >= 0.0
        relc = jnp.where(causal, rel, 0.0)
        for h in range(RET_HEADS):
            decay_ref[h] = jnp.where(causal, jnp.exp(LOG_GAMMA[h] * relc), 0.0)

    @pl.when(lax.rem(jnp.maximum(i - 1, 0), ns) == 0)
    def _():
        xext_ref[0:SUBLANES, :] = jnp.zeros((SUBLANES, D_LRU), F32)
        h_ref[...] = jnp.zeros_like(h_ref)
        state_ref[...] = jnp.zeros_like(state_ref)

    x = x_ref[...]
    ms = jnp.mean(x * x, axis=-1, keepdims=True)
    hn = x * lax.rsqrt(ms + NORM_EPS) * ng_ref[...]
    proj_ref[i % 2] = jnp.dot(hn.astype(BF16), win_ref[...],
                              preferred_element_type=F32).astype(BF16)

    slot = (i + 1) % 2

    def pcol(lo, width):
        return proj_ref[slot, :, lo:lo + width]

    xext_ref[SUBLANES:SUBLANES + tc, :] = pcol(0, D_LRU).astype(F32)
    xc = cb_ref[...] + cw_ref[CONV_WIDTH - 1:CONV_WIDTH, :] * xext_ref[SUBLANES:SUBLANES + tc, :]
    for j in range(CONV_WIDTH - 1):
        off = SUBLANES - (CONV_WIDTH - 1) + j
        xc = xc + cw_ref[j:j + 1, :] * xext_ref[off:off + tc, :]
    xext_ref[0:SUBLANES, :] = xext_ref[tc:tc + SUBLANES, :]

    gates = jnp.dot(xc.astype(BF16), wg_ref[...], preferred_element_type=F32)
    r = jax.nn.sigmoid(gates[:, :D_LRU] + ba_ref[...])
    ig = jax.nn.sigmoid(gates[:, D_LRU:] + bx_ref[...])
    a = jnp.exp((-LRU_C) * r * sp_ref[...])
    z = 1.0 - a * a
    b = jnp.where(z > 0.0, z * lax.rsqrt(z), 0.0) * (ig * xc)

    ng = tc // SUBLANES
    a = a.reshape(ng, SUBLANES, D_LRU)
    b = b.reshape(ng, SUBLANES, D_LRU)
    in_group = lax.broadcasted_iota(jnp.int32, (1, SUBLANES, 1), 1)
    d = 1
    while d < SUBLANES:
        a_s = pltpu.roll(a, d, 1)
        b_s = pltpu.roll(b, d, 1)
        m = in_group >= d
        b = jnp.where(m, a * b_s + b, b)
        a = jnp.where(m, a * a_s, a)
        d *= 2
    h_prev = h_ref[0:1, :]
    groups = []
    for g in range(ng):
        hg = b[g] + a[g] * h_prev
        groups.append(hg)
        h_prev = hg[SUBLANES - 1:SUBLANES, :]
    hseq = jnp.concatenate(groups, axis=0)
    h_ref[0:1, :] = h_prev

    y = hseq * jax.nn.gelu(pcol(D_LRU, D_LRU).astype(F32))
    ms = jnp.mean(y * y, axis=-1, keepdims=True)
    ylru_ref[...] = (y * lax.rsqrt(ms + NORM_EPS) * lng_ref[...]).astype(ylru_ref.dtype)

    hr = tc // 2
    pos = pos_ref[...].astype(F32)
    first = lax.broadcasted_iota(jnp.int32, (hr, dh), 1) < dh // 2
    ang = jnp.where(first, pos[0:hr, :], pos[hr:tc, :]) * invf_ref[...]
    s2, c2 = _sincos(ang)
    c2r = pltpu.roll(c2, dh // 2, 1)
    s2r = pltpu.roll(s2, dh // 2, 1)
    cos = jnp.concatenate([jnp.where(first, c2, c2r), jnp.where(first, c2r, c2)], axis=0)
    sin = jnp.concatenate([jnp.where(first, s2, s2r), jnp.where(first, s2r, s2)], axis=0)
    sin_signed = sin * sgn_ref[...]
    rowf = lax.broadcasted_iota(jnp.int32, (tc, 1), 0).astype(F32)
    scale = dh ** -0.5
    q0, k0, v0, g0 = 2 * D_LRU, 2 * D_LRU + D_RET, 2 * D_LRU + 2 * D_RET, 2 * D_LRU + 3 * D_RET

    for h in range(RET_HEADS):
        sl = slice(h * dh, (h + 1) * dh)
        q = pcol(q0 + h * dh, dh).astype(F32)
        k = pcol(k0 + h * dh, dh).astype(F32)
        v = pcol(v0 + h * dh, dh)
        qr = q * cos + pltpu.roll(q, dh // 2, 1) * sin_signed
        kr = (k * cos + pltpu.roll(k, dh // 2, 1) * sin_signed) * scale
        qb = qr.astype(BF16)
        kb = kr.astype(BF16)
        lg = LOG_GAMMA[h]
        scores = lax.dot_general(qb, kb, (((1,), (1,)), ((), ())),
                                 preferred_element_type=F32) * decay_ref[h]
        intra = jnp.dot(scores.astype(BF16), v, preferred_element_type=F32)
        st = state_ref[h]
        xi = jnp.exp(lg * (rowf + 1.0))
        cross = jnp.dot(qb, st.astype(BF16), preferred_element_type=F32) * xi
        o = intra + cross
        zeta = jnp.exp(lg * (float(tc) - 1.0 - rowf))
        kz = (kr * zeta).astype(BF16)
        kv = lax.dot_general(kz, v, (((0,), (0,)), ((), ())), preferred_element_type=F32)
        state_ref[h] = math.exp(lg * tc) * st + kv

        mu = jnp.mean(o, axis=-1, keepdims=True)
        oc = o - mu
        var = jnp.mean(oc * oc, axis=-1, keepdims=True)
        on = oc * lax.rsqrt(var + NORM_EPS) * gg_ref[:, sl] + gb_ref[:, sl]
        gate = pcol(g0 + h * dh, dh).astype(F32)
        yret_ref[:, sl] = (gate * jax.nn.sigmoid(gate) * on).astype(yret_ref.dtype)


def _mixer(x2, pos_col, seq, norm_g, w_in_bf16, conv_w, conv_b, wg_bd, ba, bx, sp, lru_norm_g,
           inv_freq, sign, gn_g, gn_b):
    t = x2.shape[0]
    tc = TC_MIX
    n_tiles = t // tc
    const = lambda i: (0, 0)
    cur = lambda i: (jnp.minimum(i, n_tiles - 1), 0)
    prev = lambda i: (jnp.maximum(i - 1, 0), 0)
    vec = lambda n: pl.BlockSpec((1, n), const)
    return pl.pallas_call(
        functools.partial(_mixer_kernel, ns=seq // tc),
        grid=(n_tiles + 1,),
        in_specs=[
            pl.BlockSpec((tc, D_MODEL), cur),
            pl.BlockSpec((tc, 1), prev),
            vec(D_MODEL),
            pl.BlockSpec((D_MODEL, D_IN), const),
            pl.BlockSpec((CONV_WIDTH, D_LRU), const),
            vec(D_LRU),
            pl.BlockSpec((D_LRU, 2 * D_LRU), const),
            vec(D_LRU), vec(D_LRU), vec(D_LRU), vec(D_LRU),
            vec(RET_HEAD_DIM), vec(RET_HEAD_DIM),
            vec(D_RET), vec(D_RET),
        ],
        out_specs=[pl.BlockSpec((tc, D_LRU), prev), pl.BlockSpec((tc, D_RET), prev)],
        out_shape=[jax.ShapeDtypeStruct((t, D_LRU), BF16), jax.ShapeDtypeStruct((t, D_RET), BF16)],
        scratch_shapes=[
            pltpu.VMEM((2, tc, D_IN), BF16),
            pltpu.VMEM((tc + SUBLANES, D_LRU), F32),
            pltpu.VMEM((SUBLANES, D_LRU), F32),
            pltpu.VMEM((RET_HEADS, RET_HEAD_DIM, RET_HEAD_DIM), F32),
            pltpu.VMEM((RET_HEADS, tc, tc), F32),
        ],
        compiler_params=_cparams(("arbitrary",)),
        name="mixer",
    )(x2, pos_col, norm_g, w_in_bf16, conv_w, conv_b, wg_bd, ba, bx, sp, lru_norm_g,
      inv_freq, sign, gn_g, gn_b)


def _outproj_router_kernel(x_ref, yl_ref, yr_ref, wo_ref, ng_ref, rw_ref, rb_ref, tri_ref,
                           upper_ref, x1_ref, hf_ref, meta_ref, cnt_ref, logit_ref):
    i = pl.program_id(0)
    tm = x_ref.shape[0]

    @pl.when(i == 0)
    def _():
        logit_ref[1] = jnp.zeros((tm, LANES), F32)

    logits = logit_ref[(i + 1) % 2]

    y = jnp.dot(yl_ref[...], wo_ref[0:D_LRU, :], preferred_element_type=F32)
    y = y + jnp.dot(yr_ref[...], wo_ref[D_LRU:, :], preferred_element_type=F32)
    x1 = x_ref[...] + y
    x1_ref[...] = x1
    ms = jnp.mean(x1 * x1, axis=-1, keepdims=True)
    hf = x1 * lax.rsqrt(ms + NORM_EPS) * ng_ref[...]
    hf_ref[...] = hf.astype(hf_ref.dtype)

    logit_ref[i % 2] = (jnp.dot(hf.astype(BF16), rw_ref[...], preferred_element_type=F32)
                        + rb_ref[...])

    lane = lax.broadcasted_iota(jnp.int32, (tm, LANES), 1)
    lane_f = lane.astype(F32)
    work = logits
    vals, idxs = [], []
    onehot = jnp.zeros((tm, LANES), F32)
    for _ in range(TOP_K):
        m = jnp.max(work, axis=-1, keepdims=True)
        idx = jnp.min(jnp.where(work == m, lane_f, float(LANES)), axis=-1, keepdims=True)
        sel = lane_f == idx
        work = jnp.where(sel, -jnp.inf, work)
        onehot = jnp.where(sel, 1.0, onehot)
        vals.append(m)
        idxs.append(idx)
    exps = [jnp.exp(v - vals[0]) for v in vals]
    denom = exps[0] + exps[1] + exps[2] + exps[3]
    gates = [e / denom for e in exps]

    counts = jnp.sum(onehot, axis=0, keepdims=True)
    seg_rows = jnp.floor((counts + (SEG - 1.0)) * (1.0 / SEG)) * SEG
    seg_start = jnp.dot(jnp.broadcast_to(seg_rows, (SUBLANES, LANES)).astype(BF16), upper_ref[...],
                        preferred_element_type=F32)[0:1, :]
    pos_mat = jnp.dot(tri_ref[...], onehot.astype(BF16), preferred_element_type=F32) + seg_start
    meta = jnp.zeros((tm, LANES), F32)
    for kk in range(TOP_K):
        pos = jnp.sum(jnp.where(lane_f == idxs[kk], pos_mat, 0.0), axis=-1, keepdims=True)
        meta = jnp.where(lane == kk, idxs[kk], meta)
        meta = jnp.where(lane == TOP_K + kk, pos, meta)
        meta = jnp.where(lane == 2 * TOP_K + kk, gates[kk], meta)
    meta_ref[...] = meta
    cnt_ref[...] = jnp.broadcast_to(counts, cnt_ref.shape)


def _outproj_router(x2, y_lru, y_ret, w_out_bf16, ng, rw_pad, rb_pad, tri, upper):
    t = x2.shape[0]
    tm = TM_PROJ
    n_tiles = t // tm
    const = lambda i: (0, 0)
    cur = lambda i: (jnp.minimum(i, n_tiles - 1), 0)
    prev = lambda i: (jnp.maximum(i - 1, 0), 0)
    return pl.pallas_call(
        _outproj_router_kernel,
        grid=(n_tiles + 1,),
        in_specs=[
            pl.BlockSpec((tm, D_MODEL), cur),
            pl.BlockSpec((tm, D_LRU), cur),
            pl.BlockSpec((tm, D_RET), cur),
            pl.BlockSpec((D_MODEL, D_MODEL), const),
            pl.BlockSpec((1, D_MODEL), const),
            pl.BlockSpec((D_MODEL, LANES), const),
            pl.BlockSpec((1, LANES), const),
            pl.BlockSpec((tm, tm), const),
            pl.BlockSpec((LANES, LANES), const),
        ],
        out_specs=[
            pl.BlockSpec((tm, D_MODEL), cur),
            pl.BlockSpec((tm, D_MODEL), cur),
            pl.BlockSpec((tm, LANES), prev),
            pl.BlockSpec((None, SUBLANES, LANES), lambda i: (jnp.maximum(i - 1, 0), 0, 0)),
        ],
        out_shape=[
            jax.ShapeDtypeStruct((t, D_MODEL), F32),
            jax.ShapeDtypeStruct((t, D_MODEL), BF16),
            jax.ShapeDtypeStruct((t, LANES), F32),
            jax.ShapeDtypeStruct((n_tiles, SUBLANES, LANES), F32),
        ],
        scratch_shapes=[pltpu.VMEM((2, tm, LANES), F32)],
        compiler_params=_cparams(("arbitrary",)),
        name="outproj_router",
    )(x2, y_lru, y_ret, w_out_bf16, ng, rw_pad, rb_pad, tri, upper)


def _issue_pieces(copy, pieces, tile):
    blo_ref, bgo_ref, nbig_ref, slo_ref, sgo_ref, nsml_ref = pieces

    def start_list(lo_ref, go_ref, base, n, rows):
        groups = lax.shift_right_logical(n, ISSUE_SHIFT)

        def group(p, c):
            for u in range(ISSUE):
                q = base + p * ISSUE + u
                copy(lo_ref[q], go_ref[q], rows).start()
            return c

        lax.fori_loop(0, groups, group, 0)

        def single(p, c):
            copy(lo_ref[base + p], go_ref[base + p], rows).start()
            return c

        lax.fori_loop(lax.shift_left(groups, ISSUE_SHIFT), n, single, 0)

    start_list(blo_ref, bgo_ref, tile * MAX_BIG, nbig_ref[tile], BIG)
    start_list(slo_ref, sgo_ref, tile * MAX_SML, nsml_ref[tile], SEG)


def _wait_rows(copy, nseg):
    nbig = lax.shift_right_logical(nseg, WAIT_SHIFT)

    def big(j, c):
        copy(WAITBIG).wait()
        return c

    lax.fori_loop(0, nbig, big, 0)

    def small(j, c):
        copy(SEG).wait()
        return c

    lax.fori_loop(0, nseg - lax.shift_left(nbig, WAIT_SHIFT), small, 0)


def _sort_kernel(blo_ref, bgo_ref, nbig_ref, slo_ref, sgo_ref, nsml_ref, nch_ref, pend_ref,
                 hf_ref, meta_ref, xs_hbm, sorted_ref, sems, *, min_blocks):
    pieces = (blo_ref, bgo_ref, nbig_ref, slo_ref, sgo_ref, nsml_ref)
    i = pl.program_id(0)
    nt = pl.num_programs(0)
    b = i % 2
    tm = hf_ref.shape[0]
    nb = xs_hbm.shape[0] // BM

    def seg_copy(buf, lo, go, rows):
        return pltpu.make_async_copy(
            sorted_ref.at[buf, pl.ds(pl.multiple_of(lo, SEG), rows)],
            xs_hbm.at[pl.ds(pl.multiple_of(go, SEG), rows)], sems.at[buf])

    def wait_tile(tile, buf):
        _wait_rows(functools.partial(seg_copy, buf, 0, 0), nch_ref[tile])

    @pl.when(i == 0)
    def _():
        sorted_ref[1, 0:BM, :] = jnp.zeros((BM, D_MODEL), F32)

        def zcopy(start):
            return pltpu.make_async_copy(
                sorted_ref.at[1, pl.ds(0, BM)],
                xs_hbm.at[pl.ds(pl.multiple_of(start, BM), BM)], sems.at[1])

        for e in range(N_EXPERTS):
            zcopy(jnp.maximum(pend_ref[e] - BM, 0)).start()
        for e in range(N_EXPERTS):
            zcopy(jnp.maximum(pend_ref[e] - BM, 0)).wait()
        for blk in range(min_blocks, nb):
            @pl.when(blk * BM >= pend_ref[N_EXPERTS - 1])
            def _():
                zcopy(blk * BM).start()
                zcopy(blk * BM).wait()

    @pl.when(i >= 2)
    def _():
        wait_tile(i - 2, b)

    meta_t = meta_ref[...].T
    hfb = hf_ref[...]
    for j in range(RL // PROWS):
        r = (lax.broadcasted_iota(jnp.int32, (PROWS, tm), 0) + j * PROWS).astype(F32)
        perm = jnp.zeros((PROWS, tm), F32)
        for kk in range(TOP_K):
            perm = jnp.where(r == meta_t[TOP_K + kk:TOP_K + kk + 1, :], 1.0, perm)
        sorted_ref[b, j * PROWS:(j + 1) * PROWS, :] = jnp.dot(
            perm.astype(BF16), hfb, preferred_element_type=F32)

    _issue_pieces(functools.partial(seg_copy, b), pieces, i)

    @pl.when(i == nt - 1)
    def _():
        @pl.when(i >= 1)
        def _():
            wait_tile(i - 1, 1 - b)
        wait_tile(i, b)


def _sort_dispatch(pieces, nch, pend, hf, meta, n_pad_rows):
    t = hf.shape[0]
    tm = TM_PROJ
    row = lambda i, *_: (i, 0)
    return pl.pallas_call(
        functools.partial(_sort_kernel, min_blocks=t * TOP_K // BM),
        grid_spec=pltpu.PrefetchScalarGridSpec(
            num_scalar_prefetch=len(pieces) + 2,
            grid=(t // tm,),
            in_specs=[
                pl.BlockSpec((tm, D_MODEL), row),
                pl.BlockSpec((tm, LANES), row),
            ],
            out_specs=pl.BlockSpec(memory_space=pl.ANY),
            scratch_shapes=[pltpu.VMEM((2, RL, D_MODEL), F32), pltpu.SemaphoreType.DMA((2,))],
        ),
        out_shape=jax.ShapeDtypeStruct((n_pad_rows, D_MODEL), F32),
        compiler_params=_cparams(("arbitrary",)),
        name="sort_dispatch",
    )(*pieces, nch, pend, hf, meta)


def _ffn_kernel(be_ref, nv_ref, nxt_ref, rows_ref, x_ref, wg_hbm, bg_ref, wu_hbm, bu_ref, wd_hbm, bd_ref,
                o_ref, wstage_ref, wbf_ref, sems):
    i = pl.program_id(0)
    w_hbm = (wg_hbm, wu_hbm, wd_hbm)

    def fetch(e, m):
        return pltpu.make_async_copy(w_hbm[m].at[e], wstage_ref.at[m], sems.at[m])

    @pl.when(i < nv_ref[0])
    def _():
        @pl.when(i == 0)
        def _():
            for m in range(3):
                fetch(be_ref[0], m).start()

        @pl.when(jnp.logical_or(i == 0, be_ref[i] != be_ref[jnp.maximum(i - 1, 0)]))
        def _():
            for m in range(3):
                fetch(be_ref[i], m).wait()
                wbf_ref[m] = wstage_ref[m].astype(BF16)

            @pl.when(nxt_ref[i] >= 0)
            def _():
                for m in range(3):
                    fetch(nxt_ref[i], m).start()

        for rows in range(FFN_ROWS, BM + 1, FFN_ROWS):
            @pl.when(rows_ref[i] == rows)
            def _(rows=rows):
                x = x_ref[0:rows, :].astype(BF16)
                g = jnp.dot(x, wbf_ref[0], preferred_element_type=F32) + bg_ref[...]
                g = jnp.minimum(g, SWIGLU_LIMIT)
                u = jnp.dot(x, wbf_ref[1], preferred_element_type=F32) + bu_ref[...]
                u = jnp.clip(u, -SWIGLU_LIMIT, SWIGLU_LIMIT)
                act = g * jax.nn.sigmoid(SWIGLU_ALPHA * g) * (u + 1.0)
                o_ref[0:rows, :] = (jnp.dot(act.astype(BF16), wbf_ref[2],
                                            preferred_element_type=F32) + bd_ref[...])
                if rows < BM:
                    o_ref[rows:BM, :] = jnp.zeros((BM - rows, D_MODEL), F32)

    @pl.when(pl.program_id(0) >= nv_ref[0])
    def _():
        o_ref[...] = jnp.zeros_like(o_ref)


def _expert_ffn(block_e, nvalid, next_e, block_rows, xs, wg, bg, wu, bu, wd, bd):
    n_rows = xs.shape[0]
    nb = n_rows // BM

    def blk(i, be, nv, *_):
        return jnp.minimum(i, nv[0] - 1)

    xmap = lambda i, *s: (blk(i, *s), 0)
    bmap = lambda i, be, *s: (be[blk(i, be, *s)], 0, 0)
    hbm = pl.BlockSpec(memory_space=pl.ANY)
    return pl.pallas_call(
        _ffn_kernel,
        grid_spec=pltpu.PrefetchScalarGridSpec(
            num_scalar_prefetch=4,
            grid=(nb,),
            in_specs=[
                pl.BlockSpec((BM, D_MODEL), xmap),
                hbm,
                pl.BlockSpec((None, 1, D_MODEL), bmap),
                hbm,
                pl.BlockSpec((None, 1, D_MODEL), bmap),
                hbm,
                pl.BlockSpec((None, 1, D_MODEL), bmap),
            ],
            out_specs=pl.BlockSpec((BM, D_MODEL), lambda i, *s: (i, 0)),
            scratch_shapes=[pltpu.VMEM((3, D_MODEL, D_MODEL), F32),
                            pltpu.VMEM((3, D_MODEL, D_MODEL), BF16),
                            pltpu.SemaphoreType.DMA((3,))],
        ),
        out_shape=jax.ShapeDtypeStruct((n_rows, D_MODEL), F32),
        compiler_params=_cparams(("arbitrary",)),
        name="expert_ffn",
    )(block_e, nvalid, next_e, block_rows, xs, wg, bg, wu, bu, wd, bd)


def _combine_kernel(blo_ref, bgo_ref, nbig_ref, slo_ref, sgo_ref, nsml_ref, nch_ref,
                    x1_ref, meta_ref, g_ref, yb_hbm, o_ref, ybl_ref, sems):
    pieces = (blo_ref, bgo_ref, nbig_ref, slo_ref, sgo_ref, nsml_ref)
    i = pl.program_id(0)
    nt = pl.num_programs(0)
    b = i % 2
    tm = x1_ref.shape[0]

    def seg_copy(buf, lo, go, rows):
        return pltpu.make_async_copy(
            yb_hbm.at[pl.ds(pl.multiple_of(go, SEG), rows)],
            ybl_ref.at[buf, pl.ds(pl.multiple_of(lo, SEG), rows)], sems.at[buf])

    def issue_tile(tile, buf):
        _issue_pieces(functools.partial(seg_copy, buf), pieces, tile)

    def wait_tile(tile, buf):
        _wait_rows(functools.partial(seg_copy, buf, 0, 0), nch_ref[tile])

    @pl.when(i == 0)
    def _():
        ybl_ref[...] = jnp.zeros_like(ybl_ref)
        issue_tile(0, 0)

    @pl.when(i + 1 < nt)
    def _():
        issue_tile(i + 1, 1 - b)

    wait_tile(i, b)

    meta = meta_ref[...]
    lpos_b = [jnp.broadcast_to(meta[:, TOP_K + kk:TOP_K + kk + 1], (tm, LANES))
              for kk in range(TOP_K)]
    gate_b = [jnp.broadcast_to(meta[:, 2 * TOP_K + kk:2 * TOP_K + kk + 1], (tm, LANES))
              for kk in range(TOP_K)]
    lane_f = lax.broadcasted_iota(jnp.int32, (tm, LANES), 1).astype(F32)
    cols = []
    for j in range(RL // LANES):
        r = lane_f + float(j * LANES)
        gmat = jnp.zeros((tm, LANES), F32)
        for kk in range(TOP_K):
            gmat = jnp.where(r == lpos_b[kk], gate_b[kk], gmat)
        cols.append(gmat.astype(BF16))
    gates = jnp.concatenate(cols, axis=1)
    acc = x1_ref[...] + jnp.dot(gates, ybl_ref[b].astype(BF16), preferred_element_type=F32)
    ms = jnp.mean(acc * acc, axis=-1, keepdims=True)
    o_ref[...] = acc * lax.rsqrt(ms + NORM_EPS) * g_ref[...]


def _combine(pieces, nch, x1, meta, final_g, yb):
    t = x1.shape[0]
    tm = TM_PROJ
    row = lambda i, *_: (i, 0)
    return pl.pallas_call(
        _combine_kernel,
        grid_spec=pltpu.PrefetchScalarGridSpec(
            num_scalar_prefetch=len(pieces) + 1,
            grid=(t // tm,),
            in_specs=[
                pl.BlockSpec((tm, D_MODEL), row),
                pl.BlockSpec((tm, LANES), row),
                pl.BlockSpec((1, D_MODEL), lambda i, *_: (0, 0)),
                pl.BlockSpec(memory_space=pl.ANY),
            ],
            out_specs=pl.BlockSpec((tm, D_MODEL), row),
            scratch_shapes=[pltpu.VMEM((2, RL, D_MODEL), F32), pltpu.SemaphoreType.DMA((2,))],
        ),
        out_shape=jax.ShapeDtypeStruct((t, D_MODEL), F32),
        compiler_params=_cparams(("arbitrary",)),
        name="combine",
    )(*pieces, nch, x1, meta, final_g, yb)


def _expand_pieces(cnt, lo, go, step, cap):
    cum = jnp.cumsum(cnt, axis=1)[:, None, :]
    start = cum - cnt[:, None, :]
    p = jnp.arange(cap, dtype=jnp.int32)[None, :, None]
    mine = jnp.logical_and(p >= start, p < cum)
    off = (p - start) * step
    lo_p = jnp.sum(jnp.where(mine, lo[:, None, :] + off, 0), axis=2)
    go_p = jnp.sum(jnp.where(mine, go[:, None, :] + off, 0), axis=2)
    return (lo_p.astype(jnp.int32).reshape(-1), go_p.astype(jnp.int32).reshape(-1),
            cum[:, 0, -1].astype(jnp.int32))


def _piece_lists(nseg, lo8, goff):
    nbig = nseg >> BIG_SHIFT
    nsml = nseg - (nbig << BIG_SHIFT)
    done = nbig * BIG
    return (_expand_pieces(nbig, lo8, goff, BIG, MAX_BIG)
            + _expand_pieces(nsml, lo8 + done, goff + done, SEG, MAX_SML))


def _block_diag(w):
    h, d, _ = w.shape
    eye = jnp.eye(h, dtype=w.dtype)
    return (eye[:, None, :, None] * w[:, :, None, :]).reshape(h * d, h * d)


def kernel(x, positions, attn_norm_g, w_in, conv_w, conv_b, lru_wa, lru_ba, lru_wx, lru_bx, lru_lambda, lru_norm_g, ret_norm_g, ret_norm_b, w_out, ffn_norm_g, router_w, router_b, moe_w_gate, moe_b_gate, moe_w_up, moe_b_up, moe_w_down, moe_b_down, final_norm_g):
    bsz, seq, d = x.shape
    depth = w_in.shape[0]
    t = bsz * seq
    assert depth == 1 and d == D_MODEL and seq % TC_MIX == 0 and t % TM_PROJ == 0
    nt = t // TM_PROJ
    n_pad = t * TOP_K + nt * N_EXPERTS * SEG + N_EXPERTS * BM
    nb = n_pad // BM

    half = RET_HEAD_DIM // 2
    inv_freq = ROPE_THETA ** (-jnp.arange(half, dtype=F32) / half)
    inv_freq = jnp.concatenate([inv_freq, inv_freq]).reshape(1, RET_HEAD_DIM)
    sign = jnp.concatenate([-jnp.ones((half,), F32), jnp.ones((half,), F32)]).reshape(1, RET_HEAD_DIM)
    pos_col = positions.reshape(t, 1).astype(jnp.int32)
    tri = jnp.tril(jnp.ones((TM_PROJ, TM_PROJ), F32), -1).astype(BF16)
    upper = jnp.triu(jnp.ones((LANES, LANES), F32), 1).astype(BF16)

    x2 = x.reshape(t, d)
    for l in range(depth):
        wg_bd = jnp.concatenate([_block_diag(lru_wa[l]), _block_diag(lru_wx[l])], axis=1).astype(BF16)
        lam = lru_lambda[l].astype(F32)
        sp = (jnp.maximum(-lam, 0.0) + jnp.log1p(jnp.exp(-jnp.abs(lam)))).reshape(1, D_LRU)
        y_lru, y_ret = _mixer(
            x2, pos_col, seq, attn_norm_g[l].reshape(1, d), w_in[l].astype(BF16),
            conv_w[l], conv_b[l].reshape(1, D_LRU), wg_bd,
            lru_ba[l].reshape(1, D_LRU), lru_bx[l].reshape(1, D_LRU), sp,
            lru_norm_g[l].reshape(1, D_LRU), inv_freq, sign,
            ret_norm_g[l].reshape(1, D_RET), ret_norm_b[l].reshape(1, D_RET))

        rw_pad = jnp.zeros((d, LANES), F32).at[:, :N_EXPERTS].set(router_w[l]).astype(BF16)
        rb_pad = jnp.full((1, LANES), -1e30, F32).at[0, :N_EXPERTS].set(router_b[l])
        x1, hf, meta, cnt_tile = _outproj_router(x2, y_lru, y_ret, w_out[l].astype(BF16),
                                            ffn_norm_g[l].reshape(1, d), rw_pad, rb_pad, tri,
                                            upper)

        cnt = cnt_tile[:, 0, :N_EXPERTS].astype(jnp.int32)
        c8 = (cnt + SEG - 1) // SEG * SEG
        lo8 = jnp.cumsum(c8, axis=1) - c8
        padded = (jnp.sum(c8, axis=0) + BM - 1) // BM * BM
        pend = jnp.cumsum(padded).astype(jnp.int32)
        pstart = pend - padded
        goff = (pstart[None, :] + jnp.cumsum(c8, axis=0) - c8).astype(jnp.int32)
        nseg = (c8 // SEG).astype(jnp.int32)
        block_start = jnp.arange(nb, dtype=jnp.int32) * BM
        block_e = jnp.minimum(jnp.sum(block_start[:, None] >= pend[None, :], axis=1),
                              N_EXPERTS - 1).astype(jnp.int32)
        nvalid = (pend[N_EXPERTS - 1:] // BM).astype(jnp.int32)
        pieces = _piece_lists(nseg, lo8.astype(jnp.int32), goff)
        nch = jnp.sum(nseg, axis=1).astype(jnp.int32)

        xs = _sort_dispatch(pieces, nch, pend, hf, meta, n_pad)
        mine = block_e[:, None] == jnp.arange(N_EXPERTS, dtype=jnp.int32)[None, :]
        next_start = jnp.sum(jnp.where(mine, pend[None, :], 0), axis=1)
        next_e = jnp.minimum(jnp.sum(next_start[:, None] >= pend[None, :], axis=1), N_EXPERTS - 1)
        next_e = jnp.where(next_start < nvalid[0] * BM, next_e, -1).astype(jnp.int32)
        row_end = pstart + jnp.sum(c8, axis=0)
        used = jnp.sum(jnp.where(mine, row_end[None, :], 0), axis=1) - block_start
        block_rows = ((jnp.clip(used, 1, BM) + FFN_ROWS - 1) // FFN_ROWS * FFN_ROWS).astype(jnp.int32)
        yb = _expert_ffn(block_e, nvalid, next_e, block_rows, xs,
                         moe_w_gate[l], moe_b_gate[l].reshape(N_EXPERTS, 1, d),
                         moe_w_up[l], moe_b_up[l].reshape(N_EXPERTS, 1, d),
                         moe_w_down[l], moe_b_down[l].reshape(N_EXPERTS, 1, d))
        x2 = _combine(pieces, nch, x1, meta, final_norm_g.reshape(1, d), yb)
    return x2.reshape(bsz, seq, d)
```

```python
import functools
import math

import numpy as np
import jax
import jax.numpy as jnp
from jax import lax
from jax.experimental import pallas as pl
from jax.experimental.pallas import tpu as pltpu

F32 = jnp.float32
BF16 = jnp.bfloat16

D_MODEL = 1024
D_LRU = 512
D_RET = 512
CONV_WIDTH = 4
LRU_C = 8.0
RET_HEADS = 4
RET_HEAD_DIM = D_RET // RET_HEADS
ROPE_THETA = 10000.0
D_IN = 2 * D_LRU + 4 * D_RET
N_EXPERTS = 32
TOP_K = 4
SWIGLU_ALPHA = 1.702
SWIGLU_LIMIT = 7.0
NORM_EPS = 1e-5

LANES = 128
SUBLANES = 8
VMEM_LIMIT = 48 * 1024 * 1024

TM_PROJ = 512
TC_MIX = 512
BM = 1024
FFN_ROWS = 128
SEG = SUBLANES
RL = TM_PROJ * TOP_K + N_EXPERTS * SEG
PROWS = 768
BIG_SHIFT = 2
BIG = SEG << BIG_SHIFT
ISSUE_SHIFT = 2
ISSUE = 1 << ISSUE_SHIFT
MAX_BIG = RL // BIG
MAX_SML = N_EXPERTS * ((1 << BIG_SHIFT) - 1)
WAIT_SHIFT = 5
WAITBIG = SEG << WAIT_SHIFT

LOG_GAMMA = [math.log1p(-(2.0 ** (-5.0 - h))) for h in range(RET_HEADS)]


def _split_quarter_pi():
    rest = np.float64(np.pi) / 4.0
    parts = []
    for _ in range(3):
        m, e = np.frexp(rest)
        piece = np.ldexp(np.round(m * 1024.0) / 1024.0, e)
        parts.append(float(piece))
        rest = rest - piece
    parts.append(float(np.float32(rest)))
    return parts


_QPI = _split_quarter_pi()
_SIN_C = (-1.9515295891e-4, 8.3321608736e-3, -1.6666654611e-1)
_COS_C = (2.443315711809948e-5, -1.388731625493765e-3, 4.166664568298827e-2)


def _sincos(x):
    ax = jnp.abs(x)
    j = (ax * (4.0 / math.pi)).astype(jnp.int32)
    j = j + jnp.bitwise_and(j, 1)
    y = j.astype(F32)
    r = (((ax - y * _QPI[0]) - y * _QPI[1]) - y * _QPI[2]) - y * _QPI[3]
    z = r * r
    sin_p = r + r * z * ((_SIN_C[0] * z + _SIN_C[1]) * z + _SIN_C[2])
    cos_p = 1.0 - 0.5 * z + z * z * ((_COS_C[0] * z + _COS_C[1]) * z + _COS_C[2])
    quad = jnp.right_shift(j, 1)
    swap = jnp.bitwise_and(quad, 1) == 1
    sin_v = jnp.where(swap, cos_p, sin_p)
    cos_v = jnp.where(swap, sin_p, cos_p)
    sin_neg = jnp.logical_xor(jnp.bitwise_and(quad, 2) == 2, x < 0.0)
    cos_neg = jnp.bitwise_and(quad + 1, 2) == 2
    return jnp.where(sin_neg, -sin_v, sin_v), jnp.where(cos_neg, -cos_v, cos_v)


def _cparams(sem):
    return pltpu.CompilerParams(dimension_semantics=sem, vmem_limit_bytes=VMEM_LIMIT)


def _mixer_kernel(x_ref, pos_ref, ng_ref, win_ref, cw_ref, cb_ref, wg_ref, ba_ref, bx_ref,
                  sp_ref, lng_ref, invf_ref, sgn_ref, gg_ref, gb_ref,
                  ylru_ref, yret_ref,
                  proj_ref, xext_ref, h_ref, state_ref, decay_ref, *, ns):
    i = pl.program_id(0)
    tc = x_ref.shape[0]
    dh = RET_HEAD_DIM

    @pl.when(i == 0)
    def _():
        proj_ref[1] = jnp.zeros((tc, D_IN), BF16)
        ri = lax.broadcasted_iota(jnp.int32, (tc, tc), 0)
        ci = lax.broadcasted_iota(jnp.int32, (tc, tc), 1)
        rel = (ri - ci).astype(F32)
        causal = rel >= 0.0
        relc = jnp.where(causal, rel, 0.0)
        for h in range(RET_HEADS):
            decay_ref[h] = jnp.where(causal, jnp.exp(LOG_GAMMA[h] * relc), 0.0)

    @pl.when(lax.rem(jnp.maximum(i - 1, 0), ns) == 0)
    def _():
        xext_ref[0:SUBLANES, :] = jnp.zeros((SUBLANES, D_LRU), F32)
        h_ref[...] = jnp.zeros_like(h_ref)
        state_ref[...] = jnp.zeros_like(state_ref)

    x = x_ref[...]
    ms = jnp.mean(x * x, axis=-1, keepdims=True)
    hn = x * lax.rsqrt(ms + NORM_EPS) * ng_ref[...]
    proj_ref[i % 2] = jnp.dot(hn.astype(BF16), win_ref[...],
                              preferred_element_type=F32).astype(BF16)

    slot = (i + 1) % 2

    def pcol(lo, width):
        return proj_ref[slot, :, lo:lo + width]

    xext_ref[SUBLANES:SUBLANES + tc, :] = pcol(0, D_LRU).astype(F32)
    xc = cb_ref[...] + cw_ref[CONV_WIDTH - 1:CONV_WIDTH, :] * xext_ref[SUBLANES:SUBLANES + tc, :]
    for j in range(CONV_WIDTH - 1):
        off = SUBLANES - (CONV_WIDTH - 1) + j
        xc = xc + cw_ref[j:j + 1, :] * xext_ref[off:off + tc, :]
    xext_ref[0:SUBLANES, :] = xext_ref[tc:tc + SUBLANES, :]

    gates = jnp.dot(xc.astype(BF16), wg_ref[...], preferred_element_type=F32)
    r = jax.nn.sigmoid(gates[:, :D_LRU] + ba_ref[...])
    ig = jax.nn.sigmoid(gates[:, D_LRU:] + bx_ref[...])
    a = jnp.exp((-LRU_C) * r * sp_ref[...])
    z = 1.0 - a * a
    b = jnp.where(z > 0.0, z * lax.rsqrt(z), 0.0) * (ig * xc)

    ng = tc // SUBLANES
    a = a.reshape(ng, SUBLANES, D_LRU)
    b = b.reshape(ng, SUBLANES, D_LRU)
    in_group = lax.broadcasted_iota(jnp.int32, (1, SUBLANES, 1), 1)
    d = 1
    while d < SUBLANES:
        a_s = pltpu.roll(a, d, 1)
        b_s = pltpu.roll(b, d, 1)
        m = in_group >= d
        b = jnp.where(m, a * b_s + b, b)
        a = jnp.where(m, a * a_s, a)
        d *= 2
    h_prev = h_ref[0:1, :]
    groups = []
    for g in range(ng):
        hg = b[g] + a[g] * h_prev
        groups.append(hg)
        h_prev = hg[SUBLANES - 1:SUBLANES, :]
    hseq = jnp.concatenate(groups, axis=0)
    h_ref[0:1, :] = h_prev

    y = hseq * jax.nn.gelu(pcol(D_LRU, D_LRU).astype(F32))
    ms = jnp.mean(y * y, axis=-1, keepdims=True)
    ylru_ref[...] = (y * lax.rsqrt(ms + NORM_EPS) * lng_ref[...]).astype(ylru_ref.dtype)

    hr = tc // 2
    pos = pos_ref[...].astype(F32)
    first = lax.broadcasted_iota(jnp.int32, (hr, dh), 1) < dh // 2
    ang = jnp.where(first, pos[0:hr, :], pos[hr:tc, :]) * invf_ref[...]
    s2, c2 = _sincos(ang)
    c2r = pltpu.roll(c2, dh // 2, 1)
    s2r = pltpu.roll(s2, dh // 2, 1)
    cos = jnp.concatenate([jnp.where(first, c2, c2r), jnp.where(first, c2r, c2)], axis=0)
    sin = jnp.concatenate([jnp.where(first, s2, s2r), jnp.where(first, s2r, s2)], axis=0)
    sin_signed = sin * sgn_ref[...]
    rowf = lax.broadcasted_iota(jnp.int32, (tc, 1), 0).astype(F32)
    scale = dh ** -0.5
    q0, k0, v0, g0 = 2 * D_LRU, 2 * D_LRU + D_RET, 2 * D_LRU + 2 * D_RET, 2 * D_LRU + 3 * D_RET

    for h in range(RET_HEADS):
        sl = slice(h * dh, (h + 1) * dh)
        q = pcol(q0 + h * dh, dh).astype(F32)
        k = pcol(k0 + h * dh, dh).astype(F32)
        v = pcol(v0 + h * dh, dh)
        qr = q * cos + pltpu.roll(q, dh // 2, 1) * sin_signed
        kr = (k * cos + pltpu.roll(k, dh // 2, 1) * sin_signed) * scale
        qb = qr.astype(BF16)
        kb = kr.astype(BF16)
        lg = LOG_GAMMA[h]
        scores = lax.dot_general(qb, kb, (((1,), (1,)), ((), ())),
                                 preferred_element_type=F32) * decay_ref[h]
        intra = jnp.dot(scores.astype(BF16), v, preferred_element_type=F32)
        st = state_ref[h]
        xi = jnp.exp(lg * (rowf + 1.0))
        cross = jnp.dot(qb, st.astype(BF16), preferred_element_type=F32) * xi
        o = intra + cross
        zeta = jnp.exp(lg * (float(tc) - 1.0 - rowf))
        kz = (kr * zeta).astype(BF16)
        kv = lax.dot_general(kz, v, (((0,), (0,)), ((), ())), preferred_element_type=F32)
        state_ref[h] = math.exp(lg * tc) * st + kv

        mu = jnp.mean(o, axis=-1, keepdims=True)
        oc = o - mu
        var = jnp.mean(oc * oc, axis=-1, keepdims=True)
        on = oc * lax.rsqrt(var + NORM_EPS) * gg_ref[:, sl] + gb_ref[:, sl]
        gate = pcol(g0 + h * dh, dh).astype(F32)
        yret_ref[:, sl] = (gate * jax.nn.sigmoid(gate) * on).astype(yret_ref.dtype)


def _mixer(x2, pos_col, seq, norm_g, w_in_bf16, conv_w, conv_b, wg_bd, ba, bx, sp, lru_norm_g,
           inv_freq, sign, gn_g, gn_b):
    t = x2.shape[0]
    tc = TC_MIX
    n_tiles = t // tc
    const = lambda i: (0, 0)
    cur = lambda i: (jnp.minimum(i, n_tiles - 1), 0)
    prev = lambda i: (jnp.maximum(i - 1, 0), 0)
    vec = lambda n: pl.BlockSpec((1, n), const)
    return pl.pallas_call(
        functools.partial(_mixer_kernel, ns=seq // tc),
        grid=(n_tiles + 1,),
        in_specs=[
            pl.BlockSpec((tc, D_MODEL), cur),
            pl.BlockSpec((tc, 1), prev),
            vec(D_MODEL),
            pl.BlockSpec((D_MODEL, D_IN), const),
            pl.BlockSpec((CONV_WIDTH, D_LRU), const),
            vec(D_LRU),
            pl.BlockSpec((D_LRU, 2 * D_LRU), const),
            vec(D_LRU), vec(D_LRU), vec(D_LRU), vec(D_LRU),
            vec(RET_HEAD_DIM), vec(RET_HEAD_DIM),
            vec(D_RET), vec(D_RET),
        ],
        out_specs=[pl.BlockSpec((tc, D_LRU), prev), pl.BlockSpec((tc, D_RET), prev)],
        out_shape=[jax.ShapeDtypeStruct((t, D_LRU), BF16), jax.ShapeDtypeStruct((t, D_RET), BF16)],
        scratch_shapes=[
            pltpu.VMEM((2, tc, D_IN), BF16),
            pltpu.VMEM((tc + SUBLANES, D_LRU), F32),
            pltpu.VMEM((SUBLANES, D_LRU), F32),
            pltpu.VMEM((RET_HEADS, RET_HEAD_DIM, RET_HEAD_DIM), F32),
            pltpu.VMEM((RET_HEADS, tc, tc), F32),
        ],
        compiler_params=_cparams(("arbitrary",)),
        name="mixer",
    )(x2, pos_col, norm_g, w_in_bf16, conv_w, conv_b, wg_bd, ba, bx, sp, lru_norm_g,
      inv_freq, sign, gn_g, gn_b)


def _outproj_router_kernel(x_ref, yl_ref, yr_ref, wo_ref, ng_ref, rw_ref, rb_ref, tri_ref,
                           upper_ref, x1_ref, hf_ref, meta_ref, cnt_ref, logit_ref):
    i = pl.program_id(0)
    tm = x_ref.shape[0]

    @pl.when(i == 0)
    def _():
        logit_ref[1] = jnp.zeros((tm, LANES), F32)

    logits = logit_ref[(i + 1) % 2]

    y = jnp.dot(yl_ref[...], wo_ref[0:D_LRU, :], preferred_element_type=F32)
    y = y + jnp.dot(yr_ref[...], wo_ref[D_LRU:, :], preferred_element_type=F32)
    x1 = x_ref[...] + y
    x1_ref[...] = x1
    ms = jnp.mean(x1 * x1, axis=-1, keepdims=True)
    hf = x1 * lax.rsqrt(ms + NORM_EPS) * ng_ref[...]
    hf_ref[...] = hf.astype(hf_ref.dtype)

    logit_ref[i % 2] = (jnp.dot(hf.astype(BF16), rw_ref[...], preferred_element_type=F32)
                        + rb_ref[...])

    lane = lax.broadcasted_iota(jnp.int32, (tm, LANES), 1)
    lane_f = lane.astype(F32)
    work = logits
    vals, idxs = [], []
    onehot = jnp.zeros((tm, LANES), F32)
    for _ in range(TOP_K):
        m = jnp.max(work, axis=-1, keepdims=True)
        idx = jnp.min(jnp.where(work == m, lane_f, float(LANES)), axis=-1, keepdims=True)
        sel = lane_f == idx
        work = jnp.where(sel, -jnp.inf, work)
        onehot = jnp.where(sel, 1.0, onehot)
        vals.append(m)
        idxs.append(idx)
    exps = [jnp.exp(v - vals[0]) for v in vals]
    denom = exps[0] + exps[1] + exps[2] + exps[3]
    gates = [e / denom for e in exps]

    counts = jnp.sum(onehot, axis=0, keepdims=True)
    seg_rows = jnp.floor((counts + (SEG - 1.0)) * (1.0 / SEG)) * SEG
    seg_start = jnp.dot(jnp.broadcast_to(seg_rows, (SUBLANES, LANES)).astype(BF16), upper_ref[...],
                        preferred_element_type=F32)[0:1, :]
    pos_mat = jnp.dot(tri_ref[...], onehot.astype(BF16), preferred_element_type=F32) + seg_start
    meta = jnp.zeros((tm, LANES), F32)
    for kk in range(TOP_K):
        pos = jnp.sum(jnp.where(lane_f == idxs[kk], pos_mat, 0.0), axis=-1, keepdims=True)
        meta = jnp.where(lane == kk, idxs[kk], meta)
        meta = jnp.where(lane == TOP_K + kk, pos, meta)
        meta = jnp.where(lane == 2 * TOP_K + kk, gates[kk], meta)
    meta_ref[...] = meta
    cnt_ref[...] = jnp.broadcast_to(counts, cnt_ref.shape)


def _outproj_router(x2, y_lru, y_ret, w_out_bf16, ng, rw_pad, rb_pad, tri, upper):
    t = x2.shape[0]
    tm = TM_PROJ
    n_tiles = t // tm
    const = lambda i: (0, 0)
    cur = lambda i: (jnp.minimum(i, n_tiles - 1), 0)
    prev = lambda i: (jnp.maximum(i - 1, 0), 0)
    return pl.pallas_call(
        _outproj_router_kernel,
        grid=(n_tiles + 1,),
        in_specs=[
            pl.BlockSpec((tm, D_MODEL), cur),
            pl.BlockSpec((tm, D_LRU), cur),
            pl.BlockSpec((tm, D_RET), cur),
            pl.BlockSpec((D_MODEL, D_MODEL), const),
            pl.BlockSpec((1, D_MODEL), const),
            pl.BlockSpec((D_MODEL, LANES), const),
            pl.BlockSpec((1, LANES), const),
            pl.BlockSpec((tm, tm), const),
            pl.BlockSpec((LANES, LANES), const),
        ],
        out_specs=[
            pl.BlockSpec((tm, D_MODEL), cur),
            pl.BlockSpec((tm, D_MODEL), cur),
            pl.BlockSpec((tm, LANES), prev),
            pl.BlockSpec((None, SUBLANES, LANES), lambda i: (jnp.maximum(i - 1, 0), 0, 0)),
        ],
        out_shape=[
            jax.ShapeDtypeStruct((t, D_MODEL), F32),
            jax.ShapeDtypeStruct((t, D_MODEL), BF16),
            jax.ShapeDtypeStruct((t, LANES), F32),
            jax.ShapeDtypeStruct((n_tiles, SUBLANES, LANES), F32),
        ],
        scratch_shapes=[pltpu.VMEM((2, tm, LANES), F32)],
        compiler_params=_cparams(("arbitrary",)),
        name="outproj_router",
    )(x2, y_lru, y_ret, w_out_bf16, ng, rw_pad, rb_pad, tri, upper)


def _issue_pieces(copy, pieces, tile):
    blo_ref, bgo_ref, nbig_ref, slo_ref, sgo_ref, nsml_ref = pieces

    def start_list(lo_ref, go_ref, base, n, rows):
        groups = lax.shift_right_logical(n, ISSUE_SHIFT)

        def group(p, c):
            for u in range(ISSUE):
                q = base + p * ISSUE + u
                copy(lo_ref[q], go_ref[q], rows).start()
            return c

        lax.fori_loop(0, groups, group, 0)

        def single(p, c):
            copy(lo_ref[base + p], go_ref[base + p], rows).start()
            return c

        lax.fori_loop(lax.shift_left(groups, ISSUE_SHIFT), n, single, 0)

    start_list(blo_ref, bgo_ref, tile * MAX_BIG, nbig_ref[tile], BIG)
    start_list(slo_ref, sgo_ref, tile * MAX_SML, nsml_ref[tile], SEG)


def _wait_rows(copy, nseg):
    nbig = lax.shift_right_logical(nseg, WAIT_SHIFT)

    def big(j, c):
        copy(WAITBIG).wait()
        return c

    lax.fori_loop(0, nbig, big, 0)

    def small(j, c):
        copy(SEG).wait()
        return c

    lax.fori_loop(0, nseg - lax.shift_left(nbig, WAIT_SHIFT), small, 0)


def _sort_kernel(blo_ref, bgo_ref, nbig_ref, slo_ref, sgo_ref, nsml_ref, nch_ref,
                 zend_ref, zfree_ref, znum_ref,
                 hf_ref, meta_ref, xs_hbm, sorted_ref, zero_ref, sems, zsem):
    pieces = (blo_ref, bgo_ref, nbig_ref, slo_ref, sgo_ref, nsml_ref)
    i = pl.program_id(0)
    nt = pl.num_programs(0)
    b = i % 2
    tm = hf_ref.shape[0]

    def seg_copy(buf, lo, go, rows):
        return pltpu.make_async_copy(
            sorted_ref.at[buf, pl.ds(pl.multiple_of(lo, SEG), rows)],
            xs_hbm.at[pl.ds(pl.multiple_of(go, SEG), rows)], sems.at[buf])

    def wait_tile(tile, buf):
        _wait_rows(functools.partial(seg_copy, buf, 0, 0), nch_ref[tile])

    def zcopy(start):
        return pltpu.make_async_copy(
            zero_ref, xs_hbm.at[pl.ds(pl.multiple_of(start, FFN_ROWS), FFN_ROWS)], zsem)

    def free_chunks(fn):
        for e in range(N_EXPERTS + 1):
            def body(j, c, e=e):
                fn(zfree_ref[e] + j * FFN_ROWS)
                return c
            lax.fori_loop(0, znum_ref[e], body, 0)

    @pl.when(i == 0)
    def _():
        zero_ref[...] = jnp.zeros_like(zero_ref)
        for e in range(N_EXPERTS):
            zcopy(zend_ref[e]).start()
        for e in range(N_EXPERTS):
            zcopy(zend_ref[e]).wait()
        free_chunks(lambda r: zcopy(r).start())

    @pl.when(i >= 2)
    def _():
        wait_tile(i - 2, b)

    meta_t = meta_ref[...].T
    hfb = hf_ref[...]
    for j in range(RL // PROWS):
        r = (lax.broadcasted_iota(jnp.int32, (PROWS, tm), 0) + j * PROWS).astype(F32)
        perm = jnp.zeros((PROWS, tm), F32)
        for kk in range(TOP_K):
            perm = jnp.where(r == meta_t[TOP_K + kk:TOP_K + kk + 1, :], 1.0, perm)
        sorted_ref[b, j * PROWS:(j + 1) * PROWS, :] = jnp.dot(
            perm.astype(BF16), hfb, preferred_element_type=F32)

    _issue_pieces(functools.partial(seg_copy, b), pieces, i)

    @pl.when(i == nt - 1)
    def _():
        @pl.when(i >= 1)
        def _():
            wait_tile(i - 1, 1 - b)
        wait_tile(i, b)
        free_chunks(lambda r: zcopy(r).wait())


def _sort_dispatch(pieces, nch, zero_tables, hf, meta, n_pad_rows):
    t = hf.shape[0]
    tm = TM_PROJ
    row = lambda i, *_: (i, 0)
    return pl.pallas_call(
        _sort_kernel,
        grid_spec=pltpu.PrefetchScalarGridSpec(
            num_scalar_prefetch=len(pieces) + 1 + len(zero_tables),
            grid=(t // tm,),
            in_specs=[
                pl.BlockSpec((tm, D_MODEL), row),
                pl.BlockSpec((tm, LANES), row),
            ],
            out_specs=pl.BlockSpec(memory_space=pl.ANY),
            scratch_shapes=[pltpu.VMEM((2, RL, D_MODEL), F32),
                            pltpu.VMEM((FFN_ROWS, D_MODEL), F32),
                            pltpu.SemaphoreType.DMA((2,)),
                            pltpu.SemaphoreType.DMA(())],
        ),
        out_shape=jax.ShapeDtypeStruct((n_pad_rows, D_MODEL), F32),
        compiler_params=_cparams(("arbitrary",)),
        name="sort_dispatch",
    )(*pieces, nch, *zero_tables, hf, meta)


def _ffn_kernel(be_ref, nv_ref, nxt_ref, rows_ref, x_ref, wg_hbm, bg_ref, wu_hbm, bu_ref, wd_hbm, bd_ref,
                o_ref, wstage_ref, wbf_ref, sems):
    i = pl.program_id(0)
    w_hbm = (wg_hbm, wu_hbm, wd_hbm)

    def fetch(e, m):
        return pltpu.make_async_copy(w_hbm[m].at[e], wstage_ref.at[m], sems.at[m])

    @pl.when(i < nv_ref[0])
    def _():
        @pl.when(i == 0)
        def _():
            for m in range(3):
                fetch(be_ref[0], m).start()

        @pl.when(jnp.logical_or(i == 0, be_ref[i] != be_ref[jnp.maximum(i - 1, 0)]))
        def _():
            for m in range(3):
                fetch(be_ref[i], m).wait()
                wbf_ref[m] = wstage_ref[m].astype(BF16)

            @pl.when(nxt_ref[i] >= 0)
            def _():
                for m in range(3):
                    fetch(nxt_ref[i], m).start()

        for rows in range(FFN_ROWS, BM + 1, FFN_ROWS):
            @pl.when(rows_ref[i] == rows)
            def _(rows=rows):
                x = x_ref[0:rows, :].astype(BF16)
                g = jnp.dot(x, wbf_ref[0], preferred_element_type=F32) + bg_ref[...]
                g = jnp.minimum(g, SWIGLU_LIMIT)
                u = jnp.dot(x, wbf_ref[1], preferred_element_type=F32) + bu_ref[...]
                u = jnp.clip(u, -SWIGLU_LIMIT, SWIGLU_LIMIT)
                act = g * jax.nn.sigmoid(SWIGLU_ALPHA * g) * (u + 1.0)
                o_ref[0:rows, :] = (jnp.dot(act.astype(BF16), wbf_ref[2],
                                            preferred_element_type=F32) + bd_ref[...])
                if rows < BM:
                    o_ref[rows:BM, :] = jnp.zeros((BM - rows, D_MODEL), F32)

    @pl.when(pl.program_id(0) >= nv_ref[0])
    def _():
        o_ref[...] = jnp.zeros_like(o_ref)


def _expert_ffn(block_e, nvalid, next_e, block_rows, xs, wg, bg, wu, bu, wd, bd):
    n_rows = xs.shape[0]
    nb = n_rows // BM

    def blk(i, be, nv, *_):
        return jnp.minimum(i, nv[0] - 1)

    xmap = lambda i, *s: (blk(i, *s), 0)
    bmap = lambda i, be, *s: (be[blk(i, be, *s)], 0, 0)
    hbm = pl.BlockSpec(memory_space=pl.ANY)
    return pl.pallas_call(
        _ffn_kernel,
        grid_spec=pltpu.PrefetchScalarGridSpec(
            num_scalar_prefetch=4,
            grid=(nb,),
            in_specs=[
                pl.BlockSpec((BM, D_MODEL), xmap),
                hbm,
                pl.BlockSpec((None, 1, D_MODEL), bmap),
                hbm,
                pl.BlockSpec((None, 1, D_MODEL), bmap),
                hbm,
                pl.BlockSpec((None, 1, D_MODEL), bmap),
            ],
            out_specs=pl.BlockSpec((BM, D_MODEL), lambda i, *s: (i, 0)),
            scratch_shapes=[pltpu.VMEM((3, D_MODEL, D_MODEL), F32),
                            pltpu.VMEM((3, D_MODEL, D_MODEL), BF16),
                            pltpu.SemaphoreType.DMA((3,))],
        ),
        out_shape=jax.ShapeDtypeStruct((n_rows, D_MODEL), F32),
        compiler_params=_cparams(("arbitrary",)),
        name="expert_ffn",
    )(block_e, nvalid, next_e, block_rows, xs, wg, bg, wu, bu, wd, bd)


def _combine_kernel(blo_ref, bgo_ref, nbig_ref, slo_ref, sgo_ref, nsml_ref, nch_ref,
                    x1_ref, meta_ref, g_ref, yb_hbm, o_ref, ybl_ref, sems):
    pieces = (blo_ref, bgo_ref, nbig_ref, slo_ref, sgo_ref, nsml_ref)
    i = pl.program_id(0)
    nt = pl.num_programs(0)
    b = i % 2
    tm = x1_ref.shape[0]

    def seg_copy(buf, lo, go, rows):
        return pltpu.make_async_copy(
            yb_hbm.at[pl.ds(pl.multiple_of(go, SEG), rows)],
            ybl_ref.at[buf, pl.ds(pl.multiple_of(lo, SEG), rows)], sems.at[buf])

    def issue_tile(tile, buf):
        _issue_pieces(functools.partial(seg_copy, buf), pieces, tile)

    def wait_tile(tile, buf):
        _wait_rows(functools.partial(seg_copy, buf, 0, 0), nch_ref[tile])

    @pl.when(i == 0)
    def _():
        ybl_ref[...] = jnp.zeros_like(ybl_ref)
        issue_tile(0, 0)

    @pl.when(i + 1 < nt)
    def _():
        issue_tile(i + 1, 1 - b)

    wait_tile(i, b)

    meta = meta_ref[...]
    lpos_b = [jnp.broadcast_to(meta[:, TOP_K + kk:TOP_K + kk + 1], (tm, LANES))
              for kk in range(TOP_K)]
    gate_b = [jnp.broadcast_to(meta[:, 2 * TOP_K + kk:2 * TOP_K + kk + 1], (tm, LANES))
              for kk in range(TOP_K)]
    lane_f = lax.broadcasted_iota(jnp.int32, (tm, LANES), 1).astype(F32)
    cols = []
    for j in range(RL // LANES):
        r = lane_f + float(j * LANES)
        gmat = jnp.zeros((tm, LANES), F32)
        for kk in range(TOP_K):
            gmat = jnp.where(r == lpos_b[kk], gate_b[kk], gmat)
        cols.append(gmat.astype(BF16))
    gates = jnp.concatenate(cols, axis=1)
    acc = x1_ref[...] + jnp.dot(gates, ybl_ref[b].astype(BF16), preferred_element_type=F32)
    ms = jnp.mean(acc * acc, axis=-1, keepdims=True)
    o_ref[...] = acc * lax.rsqrt(ms + NORM_EPS) * g_ref[...]


def _combine(pieces, nch, x1, meta, final_g, yb):
    t = x1.shape[0]
    tm = TM_PROJ
    row = lambda i, *_: (i, 0)
    return pl.pallas_call(
        _combine_kernel,
        grid_spec=pltpu.PrefetchScalarGridSpec(
            num_scalar_prefetch=len(pieces) + 1,
            grid=(t // tm,),
            in_specs=[
                pl.BlockSpec((tm, D_MODEL), row),
                pl.BlockSpec((tm, LANES), row),
                pl.BlockSpec((1, D_MODEL), lambda i, *_: (0, 0)),
                pl.BlockSpec(memory_space=pl.ANY),
            ],
            out_specs=pl.BlockSpec((tm, D_MODEL), row),
            scratch_shapes=[pltpu.VMEM((2, RL, D_MODEL), F32), pltpu.SemaphoreType.DMA((2,))],
        ),
        out_shape=jax.ShapeDtypeStruct((t, D_MODEL), F32),
        compiler_params=_cparams(("arbitrary",)),
        name="combine",
    )(*pieces, nch, x1, meta, final_g, yb)


def _expand_pieces(cnt, lo, go, step, cap):
    cum = jnp.cumsum(cnt, axis=1)[:, None, :]
    start = cum - cnt[:, None, :]
    p = jnp.arange(cap, dtype=jnp.int32)[None, :, None]
    mine = jnp.logical_and(p >= start, p < cum)
    off = (p - start) * step
    lo_p = jnp.sum(jnp.where(mine, lo[:, None, :] + off, 0), axis=2)
    go_p = jnp.sum(jnp.where(mine, go[:, None, :] + off, 0), axis=2)
    return (lo_p.astype(jnp.int32).reshape(-1), go_p.astype(jnp.int32).reshape(-1),
            cum[:, 0, -1].astype(jnp.int32))


def _piece_lists(nseg, lo8, goff):
    nbig = nseg >> BIG_SHIFT
    nsml = nseg - (nbig << BIG_SHIFT)
    done = nbig * BIG
    return (_expand_pieces(nbig, lo8, goff, BIG, MAX_BIG)
            + _expand_pieces(nsml, lo8 + done, goff + done, SEG, MAX_SML))


def _block_diag(w):
    h, d, _ = w.shape
    eye = jnp.eye(h, dtype=w.dtype)
    return (eye[:, None, :, None] * w[:, :, None, :]).reshape(h * d, h * d)


def kernel(x, positions, attn_norm_g, w_in, conv_w, conv_b, lru_wa, lru_ba, lru_wx, lru_bx, lru_lambda, lru_norm_g, ret_norm_g, ret_norm_b, w_out, ffn_norm_g, router_w, router_b, moe_w_gate, moe_b_gate, moe_w_up, moe_b_up, moe_w_down, moe_b_down, final_norm_g):
    bsz, seq, d = x.shape
    depth = w_in.shape[0]
    t = bsz * seq
    assert depth == 1 and d == D_MODEL and seq % TC_MIX == 0 and t % TM_PROJ == 0
    nt = t // TM_PROJ
    n_pad = t * TOP_K + nt * N_EXPERTS * SEG + N_EXPERTS * BM
    nb = n_pad // BM

    half = RET_HEAD_DIM // 2
    inv_freq = ROPE_THETA ** (-jnp.arange(half, dtype=F32) / half)
    inv_freq = jnp.concatenate([inv_freq, inv_freq]).reshape(1, RET_HEAD_DIM)
    sign = jnp.concatenate([-jnp.ones((half,), F32), jnp.ones((half,), F32)]).reshape(1, RET_HEAD_DIM)
    pos_col = positions.reshape(t, 1).astype(jnp.int32)
    tri = jnp.tril(jnp.ones((TM_PROJ, TM_PROJ), F32), -1).astype(BF16)
    upper = jnp.triu(jnp.ones((LANES, LANES), F32), 1).astype(BF16)

    x2 = x.reshape(t, d)
    for l in range(depth):
        wg_bd = jnp.concatenate([_block_diag(lru_wa[l]), _block_diag(lru_wx[l])], axis=1).astype(BF16)
        lam = lru_lambda[l].astype(F32)
        sp = (jnp.maximum(-lam, 0.0) + jnp.log1p(jnp.exp(-jnp.abs(lam)))).reshape(1, D_LRU)
        y_lru, y_ret = _mixer(
            x2, pos_col, seq, attn_norm_g[l].reshape(1, d), w_in[l].astype(BF16),
            conv_w[l], conv_b[l].reshape(1, D_LRU), wg_bd,
            lru_ba[l].reshape(1, D_LRU), lru_bx[l].reshape(1, D_LRU), sp,
            lru_norm_g[l].reshape(1, D_LRU), inv_freq, sign,
            ret_norm_g[l].reshape(1, D_RET), ret_norm_b[l].reshape(1, D_RET))

        rw_pad = jnp.zeros((d, LANES), F32).at[:, :N_EXPERTS].set(router_w[l]).astype(BF16)
        rb_pad = jnp.full((1, LANES), -1e30, F32).at[0, :N_EXPERTS].set(router_b[l])
        x1, hf, meta, cnt_tile = _outproj_router(x2, y_lru, y_ret, w_out[l].astype(BF16),
                                            ffn_norm_g[l].reshape(1, d), rw_pad, rb_pad, tri,
                                            upper)

        cnt = cnt_tile[:, 0, :N_EXPERTS].astype(jnp.int32)
        c8 = (cnt + SEG - 1) // SEG * SEG
        lo8 = jnp.cumsum(c8, axis=1) - c8
        padded = (jnp.sum(c8, axis=0) + BM - 1) // BM * BM
        pend = jnp.cumsum(padded).astype(jnp.int32)
        pstart = pend - padded
        goff = (pstart[None, :] + jnp.cumsum(c8, axis=0) - c8).astype(jnp.int32)
        nseg = (c8 // SEG).astype(jnp.int32)
        block_start = jnp.arange(nb, dtype=jnp.int32) * BM
        block_e = jnp.minimum(jnp.sum(block_start[:, None] >= pend[None, :], axis=1),
                              N_EXPERTS - 1).astype(jnp.int32)
        nvalid = (pend[N_EXPERTS - 1:] // BM).astype(jnp.int32)
        pieces = _piece_lists(nseg, lo8.astype(jnp.int32), goff)
        nch = jnp.sum(nseg, axis=1).astype(jnp.int32)

        row_end = pstart + jnp.sum(c8, axis=0)
        row_end_up = (row_end + FFN_ROWS - 1) // FFN_ROWS * FFN_ROWS
        zero_tables = (
            jnp.maximum(row_end_up - FFN_ROWS, 0).astype(jnp.int32),
            jnp.concatenate([row_end_up, pend[-1:]]).astype(jnp.int32),
            jnp.concatenate([(pend - row_end_up) // FFN_ROWS,
                             (n_pad - pend[-1:]) // FFN_ROWS]).astype(jnp.int32))
        xs = _sort_dispatch(pieces, nch, zero_tables, hf, meta, n_pad)
        mine = block_e[:, None] == jnp.arange(N_EXPERTS, dtype=jnp.int32)[None, :]
        next_start = jnp.sum(jnp.where(mine, pend[None, :], 0), axis=1)
        next_e = jnp.minimum(jnp.sum(next_start[:, None] >= pend[None, :], axis=1), N_EXPERTS - 1)
        next_e = jnp.where(next_start < nvalid[0] * BM, next_e, -1).astype(jnp.int32)
        used = jnp.sum(jnp.where(mine, row_end[None, :], 0), axis=1) - block_start
        block_rows = ((jnp.clip(used, 1, BM) + FFN_ROWS - 1) // FFN_ROWS * FFN_ROWS).astype(jnp.int32)
        yb = _expert_ffn(block_e, nvalid, next_e, block_rows, xs,
                         moe_w_gate[l], moe_b_gate[l].reshape(N_EXPERTS, 1, d),
                         moe_w_up[l], moe_b_up[l].reshape(N_EXPERTS, 1, d),
                         moe_w_down[l], moe_b_down[l].reshape(N_EXPERTS, 1, d))
        x2 = _combine(pieces, nch, x1, meta, final_norm_g.reshape(1, d), yb)
    return x2.reshape(bsz, seq, d)
```
